```python
import math
import jax, jax.numpy as jnp
from jax import lax
import numpy as np

D_MODEL = 2048
BATCH = 16
SEQ = 256
DEPTH = 1
DEC_BATCH = 2
DEC_SEQ = 4096
PAST_LEN = 256

GRID_W = 64
HG_WIDTH = D_MODEL // 2
HG_DK = 128
HG_HEADS = HG_WIDTH // HG_DK
HG_DV = HG_WIDTH // HG_HEADS
HG_CHUNK = 32
DA_WIDTH = D_MODEL - HG_WIDTH
DA_HEADS = 8
DA_DK = DA_WIDTH // DA_HEADS // 2
DA_DV = DA_WIDTH // DA_HEADS
DA_Q_BLOCK = 128
ROPE_THETA = 10000.0
N_GROUPS = 4
EXP_PER_GROUP = 8
N_EXPERTS = N_GROUPS * EXP_PER_GROUP
TOP_K_IN_GROUP = 2
D_EXPERT = D_MODEL // 4
N_ADA = 6
RMS_EPS = 1e-6
PROJ_COLS = 5 * HG_WIDTH + 3 * DA_WIDTH

kernel_name = 'hybrid_hgrn2_diffattn_hiermoe_dit_step'


def rms_norm(x, g):
    xf = x.astype(jnp.float32)
    y = xf * lax.rsqrt(jnp.mean(xf * xf, axis=-1, keepdims=True) + RMS_EPS)
    return (y * g.astype(jnp.float32)).astype(x.dtype)


def ada_modulation(cond, w_ada, b_ada):
    m = jax.nn.silu(cond) @ w_ada + b_ada
    return [t[:, None, :] for t in jnp.split(m, N_ADA, axis=-1)]


def modulate(h, shift, scale):
    return h * (1 + scale) + shift


def split_heads(x, n_heads):
    b, t, _ = x.shape
    return x.reshape(b, t, n_heads, -1).transpose(0, 2, 1, 3)


def merge_heads(x):
    b, h, t, d = x.shape
    return x.transpose(0, 2, 1, 3).reshape(b, t, h * d)


def split_projection(p):
    sizes = [HG_WIDTH] * 5 + [DA_WIDTH] * 3
    idx = list(np.cumsum(sizes)[:-1])
    return jnp.split(p, idx, axis=-1)


def forget_gate(f_logits, lb_param, l):
    lb = jnp.cumsum(jax.nn.softmax(lb_param.astype(jnp.float32), axis=0), axis=0)[l]
    lb = lb.reshape(HG_HEADS, HG_DK)[None, :, None, :]
    fg = lb + (1.0 - lb) * jax.nn.sigmoid(f_logits.astype(jnp.float32))
    return jnp.log(fg), 1.0 - fg


def chunk_scan(q, k, v, log_f, s0):
    b_, h_, t_, _ = q.shape
    n = t_ // HG_CHUNK

    def to_chunks(t):
        return jnp.moveaxis(t.reshape(b_, h_, n, HG_CHUNK, t.shape[-1]), 2, 0)

    causal = jnp.tril(jnp.ones((HG_CHUNK, HG_CHUNK), dtype=bool))[:, :, None]

    def step(S, inp):
        qc, kc, vc, gc = inp
        bcum = jnp.cumsum(gc, axis=2)
        o_inter = jnp.einsum('bhtk,bhkv->bhtv', qc * jnp.exp(bcum), S)
        rel = bcum[:, :, :, None, :] - bcum[:, :, None, :, :]
        decay = jnp.where(causal, jnp.exp(jnp.minimum(rel, 0.0)), 0.0)
        scores = jnp.einsum('bhtk,bhtsk,bhsk->bhts', qc, decay, kc)
        o_intra = jnp.einsum('bhts,bhsv->bhtv', scores, vc)
        b_end = bcum[:, :, -1:, :]
        S_new = jnp.exp(b_end[:, :, 0, :])[..., None] * S + jnp.einsum('bhsk,bhsv->bhkv', kc * jnp.exp(b_end - bcum), vc)
        return S_new, o_inter + o_intra

    s_fin, o = lax.scan(step, s0, (to_chunks(q), to_chunks(k), to_chunks(v), to_chunks(log_f)))
    o = jnp.moveaxis(o, 0, 2).reshape(b_, h_, t_, v.shape[-1])
    return o, s_fin


def hgrn_bidir(q_in, f_fw, f_bw, i_in, g_in, lb_fwd, lb_bwd, norm_g, l, s_f0, s_b0):
    q = split_heads(jax.nn.silu(q_in), HG_HEADS).astype(jnp.float32)
    v = split_heads(i_in, HG_HEADS).astype(jnp.float32)
    logf_f, k_f = forget_gate(split_heads(f_fw, HG_HEADS), lb_fwd, l)
    logf_b, k_b = forget_gate(split_heads(f_bw, HG_HEADS), lb_bwd, l)
    rev = lambda t: jnp.flip(t, axis=2)
    o_f, s_f = chunk_scan(q, k_f, v, logf_f, s_f0)
    o_b, s_b = chunk_scan(rev(q), rev(k_b), rev(v), rev(logf_b), s_b0)
    o = rms_norm(o_f + rev(o_b), norm_g)
    o = merge_heads(o).astype(g_in.dtype) * jax.nn.silu(g_in)
    return o, s_f, s_b


def axial_rope(x, row, col):
    half = DA_DK // 2
    nf = half // 2
    inv = ROPE_THETA ** (-jnp.arange(nf, dtype=jnp.float32) / nf)
    ang = jnp.stack([row[:, None] * inv, col[:, None] * inv], axis=1)
    cos = jnp.cos(ang)[:, None].astype(x.dtype)
    sin = jnp.sin(ang)[:, None].astype(x.dtype)
    xs = x.reshape(x.shape[:-1] + (2, half))
    x1, x2 = xs[..., :nf], xs[..., nf:]
    out = jnp.concatenate([x1 * cos - x2 * sin, x2 * cos + x1 * sin], axis=-1)
    return out.reshape(x.shape)


def diff_lambda(lq1, lk1, lq2, lk2, l):
    lam_init = 0.8 - 0.6 * math.exp(-0.3 * l)
    lam = (jnp.exp(jnp.sum(lq1.astype(jnp.float32) * lk1.astype(jnp.float32)))
           - jnp.exp(jnp.sum(lq2.astype(jnp.float32) * lk2.astype(jnp.float32))) + lam_init)
    return lam, lam_init


def diff_attend(q, k, v, lam):
    s = jnp.einsum('bhqid,bhkid->ibhqk', q, k).astype(jnp.float32) * (DA_DK ** -0.5)
    p = jax.nn.softmax(s, axis=-1)
    a = p[0] - lam * p[1]
    return jnp.einsum('bhqk,bhkv->bhqv', a.astype(v.dtype), v)


def diff_out(o, norm_g, lam_init):
    return merge_heads(rms_norm(o, norm_g) * (1.0 - lam_init))


def mix_context(h, p, l):
    q_hg, f_fw, f_bw, i_hg, g_hg, q_da, k_da, v_da = split_projection(h @ p['w_in'])
    b, t, _ = h.shape
    zero = jnp.zeros((b, HG_HEADS, HG_DK, HG_DV), jnp.float32)
    o_hg, s_f, s_b = hgrn_bidir(q_hg, f_fw, f_bw, i_hg, g_hg, p['lb_fwd'], p['lb_bwd'], p['hg_norm_g'], l, zero, zero)
    q = split_heads(q_da, DA_HEADS).reshape(b, DA_HEADS, t, 2, DA_DK)
    k = split_heads(k_da, DA_HEADS).reshape(b, DA_HEADS, t, 2, DA_DK)
    v = split_heads(v_da, DA_HEADS)
    lam, lam_init = diff_lambda(p['lq1'], p['lk1'], p['lq2'], p['lk2'], l)
    o_da = diff_out(diff_attend(q, k, v, lam), p['da_norm_g'], lam_init).astype(h.dtype)
    out = jnp.concatenate([o_hg, o_da], axis=-1) @ p['w_out']
    return out, k.reshape(b, DA_HEADS, t, 2 * DA_DK), v, s_f, s_b


def mix_latent(h, p, l, k_ctx, v_ctx, s_f0, s_b0, row, col):
    q_hg, f_fw, f_bw, i_hg, g_hg, q_da, k_da, v_da = split_projection(h @ p['w_in'])
    b, t, _ = h.shape
    o_hg, _, _ = hgrn_bidir(q_hg, f_fw, f_bw, i_hg, g_hg, p['lb_fwd'], p['lb_bwd'], p['hg_norm_g'], l,
                            s_f0.astype(jnp.float32), s_b0.astype(jnp.float32))
    q = axial_rope(split_heads(q_da, DA_HEADS).reshape(b, DA_HEADS, t, 2, DA_DK), row, col)
    k = axial_rope(split_heads(k_da, DA_HEADS).reshape(b, DA_HEADS, t, 2, DA_DK), row, col)
    v = split_heads(v_da, DA_HEADS)
    n_ctx = k_ctx.shape[2]
    k_all = jnp.concatenate([k, k_ctx.reshape(b, DA_HEADS, n_ctx, 2, DA_DK).astype(k.dtype)], axis=2)
    v_all = jnp.concatenate([v, v_ctx.astype(v.dtype)], axis=2)
    lam, lam_init = diff_lambda(p['lq1'], p['lk1'], p['lq2'], p['lk2'], l)
    nb = t // DA_Q_BLOCK
    q_blocks = jnp.moveaxis(q.reshape(b, DA_HEADS, nb, DA_Q_BLOCK, 2, DA_DK), 2, 0)
    o = lax.map(lambda qb: diff_attend(qb, k_all, v_all, lam), q_blocks)
    o = jnp.moveaxis(o, 0, 2).reshape(b, DA_HEADS, t, DA_DV)
    o_da = diff_out(o, p['da_norm_g'], lam_init).astype(h.dtype)
    return jnp.concatenate([o_hg, o_da], axis=-1) @ p['w_out']


def hier_moe(h, p):
    b, t, d = h.shape
    x = h.reshape(b * t, d)
    lg = (x @ p['rg_w']).astype(jnp.float32) + p['rg_b'].astype(jnp.float32)
    pg = jax.nn.softmax(lg, axis=-1)
    _, gsel = lax.top_k(lg, 1)
    p_grp = jnp.take_along_axis(pg, gsel, axis=1)
    le = ((x @ p['re_w']).astype(jnp.float32) + p['re_b'].astype(jnp.float32)).reshape(-1, N_GROUPS, EXP_PER_GROUP)
    le_g = jnp.take_along_axis(le, gsel[:, :, None], axis=1)[:, 0]
    top_v, top_i = lax.top_k(le_g, TOP_K_IN_GROUP)
    w_sel = jax.nn.softmax(top_v, axis=-1) * p_grp
    gid = gsel * EXP_PER_GROUP + top_i
    comb = jnp.sum(jax.nn.one_hot(gid, N_EXPERTS, dtype=jnp.float32) * w_sel[..., None], axis=1)

    def expert_step(y, ew):
        e1, e3, e2, ce = ew
        hid = jax.nn.silu(x @ e1) * (x @ e3)
        return y + ce[:, None].astype(x.dtype) * (hid @ e2), None

    y, _ = lax.scan(expert_step, jnp.zeros_like(x), (p['w1'], p['w3'], p['w2'], comb.T))
    return y.reshape(b, t, d)


def setup_inputs(seed: int = 0) -> dict:
    key = jax.random.key(seed)
    ks = jax.random.split(key, 32)
    f32 = jnp.float32
    nrm = lambda k, shape, s: s * jax.random.normal(k, shape, f32)
    D = D_MODEL
    return {
        'x_prompt': nrm(ks[0], (BATCH, SEQ, D), 1.0),
        'x_sample': nrm(ks[1], (DEC_BATCH, DEC_SEQ, D), 1.0),
        'cache_k': nrm(ks[2], (DEC_BATCH, DEPTH, DA_HEADS, PAST_LEN, 2 * DA_DK), 1.0),
        'cache_v': nrm(ks[3], (DEC_BATCH, DEPTH, DA_HEADS, PAST_LEN, DA_DV), 1.0),
        'state_hgrn_fwd': nrm(ks[4], (DEC_BATCH, DEPTH, HG_HEADS, HG_DK, HG_DV), 0.5),
        'state_hgrn_bwd': nrm(ks[5], (DEC_BATCH, DEPTH, HG_HEADS, HG_DK, HG_DV), 0.5),
        'c': nrm(ks[6], (DEC_BATCH, D), 1.0),
        'c_ctx': nrm(ks[7], (D,), 1.0),
        'w_ada': nrm(ks[8], (DEPTH, D, N_ADA * D), 0.5 * D ** -0.5),
        'b_ada': nrm(ks[9], (DEPTH, N_ADA * D), 0.02),
        'norm1_g': 1.0 + nrm(ks[10], (DEPTH, D), 0.02),
        'norm2_g': 1.0 + nrm(ks[11], (DEPTH, D), 0.02),
        'norm_final_g': 1.0 + nrm(ks[12], (D,), 0.02),
        'w_in': nrm(ks[13], (DEPTH, D, PROJ_COLS), D ** -0.5),
        'hg_lb_fwd': nrm(ks[14], (DEPTH + 1, HG_WIDTH), 0.5),
        'hg_lb_bwd': nrm(ks[15], (DEPTH + 1, HG_WIDTH), 0.5),
        'hg_norm_g': 1.0 + nrm(ks[16], (DEPTH, HG_DV), 0.02),
        'da_lambda_q1': nrm(ks[17], (DEPTH, DA_DK), 0.1),
        'da_lambda_k1': nrm(ks[18], (DEPTH, DA_DK), 0.1),
        'da_lambda_q2': nrm(ks[19], (DEPTH, DA_DK), 0.1),
        'da_lambda_k2': nrm(ks[20], (DEPTH, DA_DK), 0.1),
        'da_norm_g': 1.0 + nrm(ks[21], (DEPTH, DA_DV), 0.02),
        'w_out': nrm(ks[22], (DEPTH, D, D), D ** -0.5),
        'router_g_w': nrm(ks[23], (DEPTH, D, N_GROUPS), D ** -0.5),
        'router_g_b': nrm(ks[24], (DEPTH, N_GROUPS), 0.01),
        'router_e_w': nrm(ks[25], (DEPTH, D, N_EXPERTS), D ** -0.5),
        'router_e_b': nrm(ks[26], (DEPTH, N_EXPERTS), 0.01),
        'exp_w1': nrm(ks[27], (DEPTH, N_EXPERTS, D, D_EXPERT), D ** -0.5),
        'exp_w3': nrm(ks[28], (DEPTH, N_EXPERTS, D, D_EXPERT), D ** -0.5),
        'exp_w2': nrm(ks[29], (DEPTH, N_EXPERTS, D_EXPERT, D), D_EXPERT ** -0.5),
    }


def reference(x_prompt, x_sample, cache_k, cache_v, state_hgrn_fwd, state_hgrn_bwd, c, c_ctx,
              w_ada, b_ada, norm1_g, norm2_g, norm_final_g, w_in, hg_lb_fwd, hg_lb_bwd, hg_norm_g,
              da_lambda_q1, da_lambda_k1, da_lambda_q2, da_lambda_k2, da_norm_g, w_out,
              router_g_w, router_g_b, router_e_w, router_e_b, exp_w1, exp_w3, exp_w2):
    n_lat = x_sample.shape[1]
    rows = n_lat // GRID_W
    t_idx = jnp.arange(rows * GRID_W)
    row = (t_idx // GRID_W).astype(jnp.float32)
    col = (t_idx % GRID_W).astype(jnp.float32)

    x_ctx = x_prompt
    x_lat = x_sample
    ks_out, vs_out, sf_out, sb_out = [], [], [], []
    for l in range(DEPTH):
        p = {
            'w_in': w_in[l], 'w_out': w_out[l], 'lb_fwd': hg_lb_fwd, 'lb_bwd': hg_lb_bwd,
            'hg_norm_g': hg_norm_g[l], 'da_norm_g': da_norm_g[l],
            'lq1': da_lambda_q1[l], 'lk1': da_lambda_k1[l], 'lq2': da_lambda_q2[l], 'lk2': da_lambda_k2[l],
            'rg_w': router_g_w[l], 'rg_b': router_g_b[l], 're_w': router_e_w[l], 're_b': router_e_b[l],
            'w1': exp_w1[l], 'w3': exp_w3[l], 'w2': exp_w2[l],
        }
        sh1, sc1, gt1, sh2, sc2, gt2 = ada_modulation(c_ctx[None, :], w_ada[l], b_ada[l])
        h = modulate(rms_norm(x_ctx, norm1_g[l]), sh1, sc1)
        mix, k_c, v_c, s_f, s_b = mix_context(h, p, l)
        x_ctx = x_ctx + gt1 * mix
        h = modulate(rms_norm(x_ctx, norm2_g[l]), sh2, sc2)
        x_ctx = x_ctx + gt2 * hier_moe(h, p)
        ks_out.append(k_c)
        vs_out.append(v_c)
        sf_out.append(s_f.astype(x_prompt.dtype))
        sb_out.append(s_b.astype(x_prompt.dtype))
        sh1, sc1, gt1, sh2, sc2, gt2 = ada_modulation(c, w_ada[l], b_ada[l])
        h = modulate(rms_norm(x_lat, norm1_g[l]), sh1, sc1)
        mix = mix_latent(h, p, l, cache_k[:, l], cache_v[:, l], state_hgrn_fwd[:, l], state_hgrn_bwd[:, l], row, col)
        x_lat = x_lat + gt1 * mix
        h = modulate(rms_norm(x_lat, norm2_g[l]), sh2, sc2)
        x_lat = x_lat + gt2 * hier_moe(h, p)

    y_prompt = rms_norm(x_ctx, norm_final_g)
    y_sample = rms_norm(x_lat, norm_final_g)
    new_cache_k = jnp.stack(ks_out, axis=1)
    new_cache_v = jnp.stack(vs_out, axis=1)
    new_state_hgrn_fwd = jnp.stack(sf_out, axis=1)
    new_state_hgrn_bwd = jnp.stack(sb_out, axis=1)
    return (y_prompt, y_sample, new_cache_k, new_cache_v, new_state_hgrn_fwd, new_state_hgrn_bwd)
```

```python
import functools
import math

import jax
import jax.numpy as jnp
from jax import lax
from jax.experimental import pallas as pl
from jax.experimental.pallas import tpu as pltpu

F32 = jnp.float32
BF16 = jnp.bfloat16
I32 = jnp.int32

D_MODEL = 2048
GRID_W = 64
HG_WIDTH = 1024
HG_DK = 128
HG_HEADS = 8
DA_HEADS = 8
DA_DK = 64
DA_DV = 128
HEAD_W = 128
ROPE_THETA = 10000.0
N_GROUPS = 4
EXP_PER_GROUP = 8
N_EXPERTS = 32
TOP_K = 2
D_EXPERT = 512
N_ADA = 6
RMS_EPS = 1e-6
PROJ_COLS = 8192
CB_Q_HG, CB_F_FW, CB_F_BW, CB_I_HG, CB_G_HG, CB_Q_DA, CB_K_DA, CB_V_DA = (8 * i for i in range(8))

LANES = 128
VMEM_LIMIT = 56 * 1024 * 1024

ADA_TN = 1536
IN_TM, IN_TN = 1024, 512
OUT_TM = 256
HG_CHUNK = 64
ATT_TQ = 256
MOE_TM = 256
CMB_TM = 512
NEG_INF = float("-inf")


def _cparams(sem):
    return pltpu.CompilerParams(dimension_semantics=sem, vmem_limit_bytes=VMEM_LIMIT)


def _silu(x):
    return x * jax.nn.sigmoid(x)


def _rms(x):
    return x * lax.rsqrt(jnp.mean(x * x, axis=-1, keepdims=True) + RMS_EPS)


def _ada_kernel(cond_ref, w_ref, b_ref, o_ref):
    s = _silu(cond_ref[...]).astype(BF16)
    o_ref[...] = jnp.dot(s, w_ref[...].astype(BF16), preferred_element_type=F32) + b_ref[...]


def _ada(cond, w, b):
    rows, d = cond.shape
    n = w.shape[1]
    return pl.pallas_call(
        _ada_kernel,
        out_shape=jax.ShapeDtypeStruct((rows, n), F32),
        grid=(n // ADA_TN,),
        in_specs=[pl.BlockSpec((rows, d), lambda j: (0, 0)),
                  pl.BlockSpec((d, ADA_TN), lambda j: (0, j)),
                  pl.BlockSpec((1, ADA_TN), lambda j: (0, j))],
        out_specs=pl.BlockSpec((rows, ADA_TN), lambda j: (0, j)),
        compiler_params=_cparams(("arbitrary",)),
        name="ada",
    )(cond, w, b.reshape(1, n))


def _in_kernel(x_ref, mod_ref, g_ref, w_ref, o_ref, h_ref):
    @pl.when(pl.program_id(1) == 0)
    def _():
        y = _rms(x_ref[...]) * g_ref[...]
        h_ref[...] = (y * (1.0 + mod_ref[0, 1:2, :]) + mod_ref[0, 0:1, :]).astype(BF16)

    o_ref[...] = jnp.dot(h_ref[...], w_ref[...], preferred_element_type=F32)


def _seg_of_tile(i, tm, n_ctx, lat_t):
    r = i * tm
    return jnp.where(r < n_ctx, 0, 1 + (r - n_ctx) // lat_t)


def _in_proj(x_all, mods, g, w_bf16, n_ctx, lat_t):
    n, d = x_all.shape
    cols = w_bf16.shape[1]
    seg = functools.partial(_seg_of_tile, tm=IN_TM, n_ctx=n_ctx, lat_t=lat_t)
    return pl.pallas_call(
        _in_kernel,
        out_shape=jax.ShapeDtypeStruct((n, cols), F32),
        grid=(n // IN_TM, cols // IN_TN),
        in_specs=[pl.BlockSpec((IN_TM, d), lambda i, j: (i, 0)),
                  pl.BlockSpec((1, N_ADA, d), lambda i, j: (seg(i), 0, 0)),
                  pl.BlockSpec((1, d), lambda i, j: (0, 0)),
                  pl.BlockSpec((d, IN_TN), lambda i, j: (0, j))],
        out_specs=pl.BlockSpec((IN_TM, IN_TN), lambda i, j: (i, j)),
        scratch_shapes=[pltpu.VMEM((IN_TM, d), BF16)],
        compiler_params=_cparams(("arbitrary", "arbitrary")),
        name="in_proj",
    )(x_all, mods, g.reshape(1, d), w_bf16)


def _rope(x, cos, sin_signed):
    lane = lax.broadcasted_iota(I32, x.shape, 1)
    first = (lane % 32) < 16
    partner = jnp.where(first, pltpu.roll(x, LANES - 16, 1), pltpu.roll(x, 16, 1))
    return x * cos + partner * sin_signed


def _split_maps(k):
    lane = lax.broadcasted_iota(I32, k.shape, 1)
    m1 = lane < DA_DK
    return jnp.where(m1, k, 0.0).astype(BF16), jnp.where(m1, 0.0, k).astype(BF16)


def _prep_kernel(*refs, use_rope, emit_cache):
    if use_rope:
        q_ref, k_ref, v_ref, cos_ref, sin_ref = refs[:5]
        outs = refs[5:]
    else:
        q_ref, k_ref, v_ref = refs[:3]
        outs = refs[3:]
    q = q_ref[...]
    k = k_ref[...]
    v = v_ref[...]
    if emit_cache:
        outs[4][0, 0, 0] = k
        outs[5][0, 0, 0] = v
    if use_rope:
        cos = cos_ref[...]
        sin = sin_ref[...]
        q = _rope(q, cos, sin)
        k = _rope(k, cos, sin)
    outs[0][0, 0] = (q * (DA_DK ** -0.5)).astype(BF16)
    k1, k2 = _split_maps(k)
    outs[1][0, 0] = k1
    outs[2][0, 0] = k2
    outs[3][0, 0] = v.astype(BF16)


def _prep(proj, row0, batch, t, tr, cos=None, sin=None, emit_cache=False):
    use_rope = cos is not None
    nt = t // tr
    rb0 = row0 // tr
    grid = (batch, nt, DA_HEADS)

    def col(cb):
        return pl.BlockSpec((tr, HEAD_W), lambda b, i, h: (rb0 + b * nt + i, cb + h))

    in_specs = [col(CB_Q_DA), col(CB_K_DA), col(CB_V_DA)]
    args = [proj, proj, proj]
    if use_rope:
        in_specs += [pl.BlockSpec((tr, HEAD_W), lambda b, i, h: (i, 0))] * 2
        args += [cos, sin]
    bht = jax.ShapeDtypeStruct((batch, DA_HEADS, t, HEAD_W), BF16)
    bht_spec = pl.BlockSpec((1, 1, tr, HEAD_W), lambda b, i, h: (b, h, i, 0))
    out_shape = [bht] * 4
    out_specs = [bht_spec] * 4
    if emit_cache:
        cache = jax.ShapeDtypeStruct((batch, 1, DA_HEADS, t, HEAD_W), F32)
        cache_spec = pl.BlockSpec((1, 1, 1, tr, HEAD_W), lambda b, i, h: (b, 0, h, i, 0))
        out_shape += [cache] * 2
        out_specs += [cache_spec] * 2
    return pl.pallas_call(
        functools.partial(_prep_kernel, use_rope=use_rope, emit_cache=emit_cache),
        out_shape=out_shape,
        grid=grid,
        in_specs=in_specs,
        out_specs=out_specs,
        compiler_params=_cparams(("arbitrary",) * 3),
        name="prep_lat" if use_rope else "prep_ctx",
    )(*args)


def _lower_bound(lb_ref):
    p = lb_ref[...]
    e = jnp.exp(p - jnp.max(p, axis=0, keepdims=True))
    return e[0:1, :] / jnp.sum(e, axis=0, keepdims=True)


def _hgrn_chunk(q, f_logit, v, lb, st_ref, rev):
    c = q.shape[0]
    levels = c.bit_length() - 1
    fg = lb + (1.0 - lb) * jax.nn.sigmoid(f_logit)
    g = jnp.log(fg)
    k = 1.0 - fg
    row = lax.broadcasted_iota(I32, (c, LANES), 0)
    ti = lax.broadcasted_iota(I32, (c, c), 0)
    si = lax.broadcasted_iota(I32, (c, c), 1)
    causal = (si >= ti) if rev else (si <= ti)

    tot = g
    pre = g
    suf = jnp.zeros_like(g)
    scores = jnp.zeros((c, c), F32)
    nt = (((1,), (1,)), ((), ()))
    for j in range(levels):
        sh = 1 << j
        bit = ((row >> j) & 1) == 1
        late = jnp.logical_not(bit) if rev else bit
        e = jnp.exp(jnp.where(late, pre, suf))
        qe = q * e
        if j == 0:
            qt = jnp.concatenate([qe, q], axis=1)
            kt = jnp.concatenate([jnp.where(late, 0.0, k), jnp.where(late, k, 0.0)], axis=1)
            mask = ((ti >> 1) == (si >> 1)) & causal
        else:
            qt = jnp.where(late, qe, 0.0)
            kt = jnp.where(late, 0.0, k * e)
            mask = (ti >> (j + 1)) == (si >> (j + 1))
        part = lax.dot_general(qt.astype(BF16), kt.astype(BF16), nt, preferred_element_type=F32)
        scores = scores + jnp.where(mask, part, 0.0)
        up = pltpu.roll(tot, sh, 0)
        dn = pltpu.roll(tot, c - sh, 0)
        from_early, from_late = (dn, up) if rev else (up, dn)
        pre = pre + jnp.where(late, from_early, 0.0)
        suf = suf + jnp.where(late, 0.0, from_late)
        tot = tot + jnp.where(late, from_early, from_late)

    st = st_ref[...]
    q_dec = (q * jnp.exp(pre)).astype(BF16)
    o = lax.dot_general(q_dec, st.astype(BF16), nt, preferred_element_type=F32)
    o = o + jnp.dot(scores.astype(BF16), v.astype(BF16), preferred_element_type=F32)
    k_dec = (k * jnp.exp(suf)).astype(BF16)
    st_ref[...] = jnp.exp(tot[0:1, :]) * st + jnp.dot(v.T.astype(BF16), k_dec, preferred_element_type=F32)
    return o


def _hgrn_kernel(*refs, t, has_init, emit_state):
    q_ref, ff_ref, fb_ref, i_ref, g_ref, lbf_ref, lbb_ref, ng_ref = refs[:8]
    pos = 8
    if has_init:
        s0f_ref, s0b_ref = refs[pos:pos + 2]
        pos += 2
    o_ref = refs[pos]
    pos += 1
    if emit_state:
        sf_ref, sb_ref = refs[pos:pos + 2]
        pos += 2
    of_scr, ob_scr, stf, stb = refs[pos:pos + 4]

    c = HG_CHUNK
    n = t // c
    if has_init:
        stf[...] = s0f_ref[0, 0, 0].T
        stb[...] = s0b_ref[0, 0, 0].T
    else:
        stf[...] = jnp.zeros_like(stf)
        stb[...] = jnp.zeros_like(stb)
    lbf = _lower_bound(lbf_ref)
    lbb = _lower_bound(lbb_ref)

    def body(i, carry):
        sl = pl.ds(pl.multiple_of(i * c, c), c)
        of_scr[sl, :] = _hgrn_chunk(_silu(q_ref[sl, :]), ff_ref[sl, :], i_ref[sl, :], lbf, stf, False)
        sl = pl.ds(pl.multiple_of((n - 1 - i) * c, c), c)
        ob_scr[sl, :] = _hgrn_chunk(_silu(q_ref[sl, :]), fb_ref[sl, :], i_ref[sl, :], lbb, stb, True)
        return carry

    lax.fori_loop(0, n, body, 0)

    fin = min(t, 256)

    def finish(i, carry):
        sl = pl.ds(pl.multiple_of(i * fin, fin), fin)
        o = _rms(of_scr[sl, :] + ob_scr[sl, :]) * ng_ref[...]
        o_ref[sl, :] = (o * _silu(g_ref[sl, :])).astype(BF16)
        return carry

    lax.fori_loop(0, t // fin, finish, 0)
    if emit_state:
        sf_ref[0, 0, 0] = stf[...].T
        sb_ref[0, 0, 0] = stb[...].T


def _hgrn(proj, row0, batch, t, lb_fwd, lb_bwd, norm_g, s0f=None, s0b=None, emit_state=False):
    has_init = s0f is not None
    rb0 = row0 // t

    def col(cb):
        return pl.BlockSpec((t, HEAD_W), lambda b, h: (rb0 + b, cb + h))

    lb_spec = pl.BlockSpec((lb_fwd.shape[0], HEAD_W), lambda b, h: (0, h))
    in_specs = [col(CB_Q_HG), col(CB_F_FW), col(CB_F_BW), col(CB_I_HG), col(CB_G_HG),
                lb_spec, lb_spec, pl.BlockSpec((1, HEAD_W), lambda b, h: (0, 0))]
    args = [proj] * 5 + [lb_fwd, lb_bwd, norm_g.reshape(1, HEAD_W)]
    st_spec = pl.BlockSpec((1, 1, 1, HG_DK, HEAD_W), lambda b, h: (b, 0, h, 0, 0))
    if has_init:
        in_specs += [st_spec, st_spec]
        args += [s0f, s0b]
    out_shape = [jax.ShapeDtypeStruct((batch * t, HG_WIDTH), BF16)]
    out_specs = [pl.BlockSpec((t, HEAD_W), lambda b, h: (b, h))]
    if emit_state:
        st = jax.ShapeDtypeStruct((batch, 1, HG_HEADS, HG_DK, HEAD_W), F32)
        out_shape += [st, st]
        out_specs += [st_spec, st_spec]
    return pl.pallas_call(
        functools.partial(_hgrn_kernel, t=t, has_init=has_init, emit_state=emit_state),
        out_shape=out_shape,
        grid=(batch, HG_HEADS),
        in_specs=in_specs,
        out_specs=out_specs,
        scratch_shapes=[pltpu.VMEM((t, HEAD_W), F32), pltpu.VMEM((t, HEAD_W), F32),
                        pltpu.VMEM((HEAD_W, HG_DK), F32), pltpu.VMEM((HEAD_W, HG_DK), F32)],
        compiler_params=_cparams(("arbitrary", "arbitrary")),
        name="hgrn_lat" if has_init else "hgrn_ctx",
    )(*args)


def _attn_kernel(*refs, key_tiles, lam_init):
    q_ref = refs[0]
    n_groups = len(key_tiles)
    kv_refs = refs[1:1 + 3 * n_groups]
    lq1_ref, lk1_ref, lq2_ref, lk2_ref, ng_ref, o_ref = refs[1 + 3 * n_groups:7 + 3 * n_groups]
    m1, l1, a1, m2, l2, a2 = refs[7 + 3 * n_groups:]

    q = q_ref[0, 0]
    for r in (m1, m2):
        r[...] = jnp.full_like(r, NEG_INF)
    for r in (l1, a1, l2, a2):
        r[...] = jnp.zeros_like(r)
    nt = (((1,), (1,)), ((), ()))

    def one_map(k, v, m_ref, l_ref, a_ref):
        s = lax.dot_general(q, k, nt, preferred_element_type=F32)
        m_old = m_ref[...]
        m_new = jnp.maximum(m_old, jnp.max(s, axis=-1, keepdims=True))
        p = jnp.exp(s - m_new)
        alpha = jnp.exp(m_old - m_new)
        l_ref[...] = alpha * l_ref[...] + jnp.sum(p, axis=-1, keepdims=True)
        a_ref[...] = alpha * a_ref[...] + jnp.dot(p.astype(BF16), v, preferred_element_type=F32)
        m_ref[...] = m_new

    for gi, (n_tiles, tk) in enumerate(key_tiles):
        k1_ref, k2_ref, v_ref = kv_refs[3 * gi:3 * gi + 3]

        def body(i, carry, k1_ref=k1_ref, k2_ref=k2_ref, v_ref=v_ref, tk=tk):
            sl = pl.ds(pl.multiple_of(i * tk, tk), tk)
            v = v_ref[0, 0, sl, :]
            one_map(k1_ref[0, 0, sl, :], v, m1, l1, a1)
            one_map(k2_ref[0, 0, sl, :], v, m2, l2, a2)
            return carry

        lax.fori_loop(0, n_tiles, body, 0)

    lam = (jnp.exp(jnp.sum(lq1_ref[...] * lk1_ref[...], axis=-1, keepdims=True))
           - jnp.exp(jnp.sum(lq2_ref[...] * lk2_ref[...], axis=-1, keepdims=True)) + lam_init)
    o = a1[...] / l1[...] - lam * (a2[...] / l2[...])
    o_ref[...] = (_rms(o) * ng_ref[...] * (1.0 - lam_init)).astype(BF16)


def _attn(q, key_groups, lam_params, norm_g, lam_init):
    batch, heads, t, _ = q.shape
    nq = t // ATT_TQ
    in_specs = [pl.BlockSpec((1, 1, ATT_TQ, HEAD_W), lambda b, h, i: (b, h, i, 0))]
    args = [q]
    key_tiles = []
    for k1, k2, v, tk in key_groups:
        tkk = k1.shape[2]
        spec = pl.BlockSpec((1, 1, tkk, HEAD_W), lambda b, h, i: (b, h, 0, 0))
        in_specs += [spec] * 3
        args += [k1, k2, v]
        key_tiles.append((tkk // tk, tk))
    small = pl.BlockSpec((1, DA_DK), lambda b, h, i: (0, 0))
    in_specs += [small] * 4 + [pl.BlockSpec((1, HEAD_W), lambda b, h, i: (0, 0))]
    args += [p.reshape(1, DA_DK) for p in lam_params] + [norm_g.reshape(1, HEAD_W)]
    col = pltpu.VMEM((ATT_TQ, 1), F32)
    acc = pltpu.VMEM((ATT_TQ, HEAD_W), F32)
    return pl.pallas_call(
        functools.partial(_attn_kernel, key_tiles=tuple(key_tiles), lam_init=lam_init),
        out_shape=jax.ShapeDtypeStruct((batch * t, heads * HEAD_W), BF16),
        grid=(batch, heads, nq),
        in_specs=in_specs,
        out_specs=pl.BlockSpec((ATT_TQ, HEAD_W), lambda b, h, i: (b * nq + i, h)),
        scratch_shapes=[col, col, acc, col, col, acc],
        compiler_params=_cparams(("arbitrary",) * 3),
        name="attn_lat" if len(key_groups) > 1 else "attn_ctx",
    )(*args)


def _out_kernel(mhg_ref, mda_ref, w_ref, x_ref, mod_ref, g_ref, rw_ref, rb_ref,
                x1_ref, h2_ref, ids_ref, wts_ref):
    mix = jnp.dot(mhg_ref[...], w_ref[0:HG_WIDTH, :], preferred_element_type=F32)
    mix = mix + jnp.dot(mda_ref[...], w_ref[HG_WIDTH:, :], preferred_element_type=F32)
    x1 = x_ref[...] + mod_ref[0, 2:3, :] * mix
    x1_ref[...] = x1
    h2 = (_rms(x1) * g_ref[...]) * (1.0 + mod_ref[0, 4:5, :]) + mod_ref[0, 3:4, :]
    h2_ref[...] = h2

    logit = jnp.dot(h2, rw_ref[...], preferred_element_type=F32, precision=lax.Precision.HIGHEST) + rb_ref[...]
    lane = lax.broadcasted_iota(I32, logit.shape, 1)

    def first_max(x):
        m = jnp.max(x, axis=-1, keepdims=True)
        return m, jnp.min(jnp.where(x == m, lane, LANES), axis=-1, keepdims=True)

    gmask = lane < N_GROUPS
    gmax, gsel = first_max(jnp.where(gmask, logit, NEG_INF))
    p_grp = 1.0 / jnp.sum(jnp.where(gmask, jnp.exp(logit - gmax), 0.0), axis=-1, keepdims=True)
    lo = N_GROUPS + EXP_PER_GROUP * gsel
    le = jnp.where((lane >= lo) & (lane < lo + EXP_PER_GROUP), logit, NEG_INF)
    v1, i1 = first_max(le)
    v2, i2 = first_max(jnp.where(lane == i1, NEG_INF, le))
    e = jnp.exp(v2 - v1)
    w1 = p_grp / (1.0 + e)
    w2 = p_grp * e / (1.0 + e)
    ids_ref[...] = jnp.where(lane == 0, i1 - N_GROUPS, jnp.where(lane == 1, i2 - N_GROUPS, 0))
    wts_ref[...] = jnp.where(lane == 0, w1, jnp.where(lane == 1, w2, 0.0))


def _out_proj(mhg, mda, w_bf16, x_all, mods, g, rw, rb, n_ctx, lat_t):
    n, d = x_all.shape
    seg = functools.partial(_seg_of_tile, tm=OUT_TM, n_ctx=n_ctx, lat_t=lat_t)
    row = lambda i: (i, 0)
    const = lambda i: (0, 0)
    return pl.pallas_call(
        _out_kernel,
        out_shape=[jax.ShapeDtypeStruct((n, d), F32), jax.ShapeDtypeStruct((n, d), F32),
                   jax.ShapeDtypeStruct((n, LANES), I32), jax.ShapeDtypeStruct((n, LANES), F32)],
        grid=(n // OUT_TM,),
        in_specs=[pl.BlockSpec((OUT_TM, HG_WIDTH), row), pl.BlockSpec((OUT_TM, HG_WIDTH), row),
                  pl.BlockSpec((d, d), const), pl.BlockSpec((OUT_TM, d), row),
                  pl.BlockSpec((1, N_ADA, d), lambda i: (seg(i), 0, 0)), pl.BlockSpec((1, d), const),
                  pl.BlockSpec((d, LANES), const), pl.BlockSpec((1, LANES), const)],
        out_specs=[pl.BlockSpec((OUT_TM, d), row), pl.BlockSpec((OUT_TM, d), row),
                   pl.BlockSpec((OUT_TM, LANES), row), pl.BlockSpec((OUT_TM, LANES), row)],
        compiler_params=_cparams(("arbitrary",)),
        name="out_proj",
    )(mhg, mda, w_bf16, x_all, mods, g.reshape(1, d), rw, rb)


def _gmm_kernel(src_ref, texp_ref, ntile_ref, h_hbm, w1_ref, w3_ref, w2_ref, y_hbm, xbuf, ybuf, sems):
    i = pl.program_id(0)
    base = i * MOE_TM

    def gather_copy(r):
        tok = jnp.maximum(src_ref[base + r], 0) >> 1
        return pltpu.make_async_copy(h_hbm.at[pl.ds(tok, 1)], xbuf.at[pl.ds(r, 1)], sems.at[0])

    def scatter_copy(r):
        return pltpu.make_async_copy(ybuf.at[pl.ds(r, 1)], y_hbm.at[pl.ds(src_ref[base + r], 1)], sems.at[1])

    def for_rows(fn):
        def body(r, carry):
            fn(r)
            return carry
        lax.fori_loop(0, MOE_TM, body, 0)

    def when_real(r, fn):
        @pl.when(src_ref[base + r] >= 0)
        def _():
            fn()

    @pl.when(i < ntile_ref[0])
    def _():
        for_rows(lambda r: gather_copy(r).start())
        for_rows(lambda r: gather_copy(r).wait())
        x = xbuf[...].astype(BF16)
        a = jnp.dot(x, w1_ref[0].astype(BF16), preferred_element_type=F32)
        b = jnp.dot(x, w3_ref[0].astype(BF16), preferred_element_type=F32)
        hid = (_silu(a) * b).astype(BF16)
        ybuf[...] = jnp.dot(hid, w2_ref[0].astype(BF16), preferred_element_type=F32)
        for_rows(lambda r: when_real(r, lambda: scatter_copy(r).start()))
        for_rows(lambda r: when_real(r, lambda: scatter_copy(r).wait()))


def _gmm(src, tile_expert, n_tiles, h2, w1, w3, w2, n_pairs):
    d = h2.shape[1]
    max_tiles = tile_expert.shape[0]
    wspec = lambda shape: pl.BlockSpec((1,) + shape, lambda i, src, te, nt: (te[i], 0, 0))
    return pl.pallas_call(
        _gmm_kernel,
        out_shape=jax.ShapeDtypeStruct((n_pairs, d), F32),
        grid_spec=pltpu.PrefetchScalarGridSpec(
            num_scalar_prefetch=3,
            grid=(max_tiles,),
            in_specs=[pl.BlockSpec(memory_space=pl.ANY),
                      wspec((d, D_EXPERT)), wspec((d, D_EXPERT)), wspec((D_EXPERT, d))],
            out_specs=pl.BlockSpec(memory_space=pl.ANY),
            scratch_shapes=[pltpu.VMEM((MOE_TM, d), F32), pltpu.VMEM((MOE_TM, d), F32),
                            pltpu.SemaphoreType.DMA((2,))],
        ),
        compiler_params=_cparams(("arbitrary",)),
        name="gmm",
    )(src, tile_expert, n_tiles, h2, w1, w3, w2)


def _route_tables(ids, n_tok):
    n_pairs = n_tok * TOP_K
    max_tiles = n_pairs // MOE_TM + N_EXPERTS
    e_flat = ids[:, :TOP_K].reshape(n_pairs)
    onehot = (e_flat[:, None] == jnp.arange(N_EXPERTS, dtype=I32)[None, :]).astype(I32)
    csum = jnp.cumsum(onehot, axis=0)
    counts = csum[-1]
    rank = jnp.sum(csum * onehot, axis=1) - 1
    tiles_per = (counts + MOE_TM - 1) // MOE_TM
    tile_end = jnp.cumsum(tiles_per)
    row_start = (tile_end - tiles_per) * MOE_TM
    pos = row_start[e_flat] + rank
    src = jnp.full((max_tiles * MOE_TM,), -1, I32).at[pos].set(jnp.arange(n_pairs, dtype=I32))
    owner = jnp.sum((tile_end[None, :] <= jnp.arange(max_tiles, dtype=I32)[:, None]).astype(I32), axis=1)
    tile_expert = jnp.minimum(owner, N_EXPERTS - 1)
    return src, tile_expert, tile_end[-1:].astype(I32)


def _combine_kernel(x1_ref, y_ref, wts_ref, mod_ref, g_ref, o_ref):
    d = x1_ref.shape[1]
    w = wts_ref[...]
    moe = w[:, 0:1] * y_ref[:, 0:d] + w[:, 1:2] * y_ref[:, d:]
    x2 = x1_ref[...] + mod_ref[0, 5:6, :] * moe
    o_ref[...] = _rms(x2) * g_ref[...]


def _combine(x1, y_pairs, wts, mods, g, row0, rows, seg_fn):
    d = x1.shape[1]
    rb0 = row0 // CMB_TM
    row = lambda i: (rb0 + i, 0)
    return pl.pallas_call(
        _combine_kernel,
        out_shape=jax.ShapeDtypeStruct((rows, d), F32),
        grid=(rows // CMB_TM,),
        in_specs=[pl.BlockSpec((CMB_TM, d), row), pl.BlockSpec((CMB_TM, TOP_K * d), row),
                  pl.BlockSpec((CMB_TM, LANES), row),
                  pl.BlockSpec((1, N_ADA, d), lambda i: (seg_fn(rb0 + i), 0, 0)),
                  pl.BlockSpec((1, d), lambda i: (0, 0))],
        out_specs=pl.BlockSpec((CMB_TM, d), lambda i: (i, 0)),
        compiler_params=_cparams(("arbitrary",)),
        name="combine",
    )(x1, y_pairs, wts, mods, g.reshape(1, d))


def _rope_tables(n_lat):
    half = DA_DK // 2
    nf = half // 2
    t = jnp.arange(n_lat)
    row = (t // GRID_W).astype(F32)
    col = (t % GRID_W).astype(F32)
    inv = ROPE_THETA ** (-jnp.arange(nf, dtype=F32) / nf)
    lane = jnp.arange(HEAD_W)
    freq = inv[lane % nf]
    pos = jnp.where(((lane // half) % 2 == 0)[None, :], row[:, None], col[:, None])
    ang = pos * freq[None, :]
    sign = jnp.where((lane % half) < nf, -1.0, 1.0).astype(F32)
    return jnp.cos(ang), jnp.sin(ang) * sign[None, :]


def kernel(x_prompt, x_sample, cache_k, cache_v, state_hgrn_fwd, state_hgrn_bwd, c, c_ctx, w_ada, b_ada,
           norm1_g, norm2_g, norm_final_g, w_in, hg_lb_fwd, hg_lb_bwd, hg_norm_g, da_lambda_q1, da_lambda_k1,
           da_lambda_q2, da_lambda_k2, da_norm_g, w_out, router_g_w, router_g_b, router_e_w, router_e_b,
           exp_w1, exp_w3, exp_w2):
    l = 0
    batch, seq, d = x_prompt.shape
    dec_batch, dec_seq, _ = x_sample.shape
    n_ctx = batch * seq
    n_lat = dec_batch * dec_seq
    n_tok = n_ctx + n_lat
    lam_init = 0.8 - 0.6 * math.exp(-0.3 * l)

    x_all = jnp.concatenate([x_prompt.reshape(n_ctx, d), x_sample.reshape(n_lat, d)], axis=0)
    cond = jnp.zeros((8, d), F32).at[0].set(c_ctx).at[1:1 + dec_batch].set(c)
    mods = _ada(cond, w_ada[l], b_ada[l]).reshape(8, N_ADA, d)

    proj = _in_proj(x_all, mods, norm1_g[l], w_in[l].astype(BF16), n_ctx, dec_seq)

    qc, k1c, k2c, vc, new_k, new_v = _prep(proj, 0, batch, seq, seq, emit_cache=True)
    lam_params = (da_lambda_q1[l], da_lambda_k1[l], da_lambda_q2[l], da_lambda_k2[l])
    da_ctx = _attn(qc, [(k1c, k2c, vc, seq)], lam_params, da_norm_g[l], lam_init)
    hg_ctx, new_sf, new_sb = _hgrn(proj, 0, batch, seq, hg_lb_fwd, hg_lb_bwd, hg_norm_g[l], emit_state=True)

    cos, sin = _rope_tables(dec_seq)
    ql, k1l, k2l, vl = _prep(proj, n_ctx, dec_batch, dec_seq, 512, cos=cos, sin=sin)
    ck = cache_k[:, l]
    lane = jnp.arange(HEAD_W)
    ck1 = jnp.where(lane < DA_DK, ck, 0.0).astype(BF16)
    ck2 = jnp.where(lane < DA_DK, 0.0, ck).astype(BF16)
    cv = cache_v[:, l].astype(BF16)
    da_lat = _attn(ql, [(k1l, k2l, vl, 512), (ck1, ck2, cv, ck.shape[2])], lam_params, da_norm_g[l], lam_init)
    (hg_lat,) = _hgrn(proj, n_ctx, dec_batch, dec_seq, hg_lb_fwd, hg_lb_bwd, hg_norm_g[l],
                   s0f=state_hgrn_fwd[:, l:l + 1], s0b=state_hgrn_bwd[:, l:l + 1])

    mhg = jnp.concatenate([hg_ctx, hg_lat], axis=0)
    mda = jnp.concatenate([da_ctx, da_lat], axis=0)
    rw = jnp.zeros((d, LANES), F32).at[:, :N_GROUPS].set(router_g_w[l]).at[:, N_GROUPS:N_GROUPS + N_EXPERTS].set(
        router_e_w[l])
    rb = jnp.zeros((1, LANES), F32).at[0, :N_GROUPS].set(router_g_b[l]).at[0, N_GROUPS:N_GROUPS + N_EXPERTS].set(
        router_e_b[l])
    x1, h2, ids, wts = _out_proj(mhg, mda, w_out[l].astype(BF16), x_all, mods, norm2_g[l], rw, rb, n_ctx, dec_seq)

    src, tile_expert, n_tiles = _route_tables(ids, n_tok)
    y_pairs = _gmm(src, tile_expert, n_tiles, h2, exp_w1[l], exp_w3[l], exp_w2[l], n_tok * TOP_K)
    y_pairs = y_pairs.reshape(n_tok, TOP_K * d)

    seg = functools.partial(_seg_of_tile, tm=CMB_TM, n_ctx=n_ctx, lat_t=dec_seq)
    y_ctx = _combine(x1, y_pairs, wts, mods, norm_final_g, 0, n_ctx, seg)
    y_lat = _combine(x1, y_pairs, wts, mods, norm_final_g, n_ctx, n_lat, seg)
    return (y_ctx.reshape(batch, seq, d), y_lat.reshape(dec_batch, dec_seq, d), new_k, new_v, new_sf, new_sb)
```

```python
import functools
import math

import numpy as np
import jax
import jax.numpy as jnp
from jax import lax
from jax.experimental import pallas as pl
from jax.experimental.pallas import tpu as pltpu

F32 = jnp.float32
BF16 = jnp.bfloat16
I32 = jnp.int32

GRID_W = 64
HG_WIDTH = 1024
HG_DK = 128
HG_HEADS = 8
DA_HEADS = 8
DA_DK = 64
HEAD_W = 128
ROPE_THETA = 10000.0
N_GROUPS = 4
EXP_PER_GROUP = 8
N_EXPERTS = 32
TOP_K = 2
D_EXPERT = 512
N_ADA = 6
RMS_EPS = 1e-6
CB_Q_HG, CB_F_FW, CB_F_BW, CB_I_HG, CB_G_HG, CB_Q_DA, CB_K_DA, CB_V_DA = (8 * i for i in range(8))

LANES = 128
VMEM_LIMIT = 56 * 1024 * 1024

ADA_TN = 1536
IN_TN = 512
OUT_TM = 256
HG_CHUNK = 64
ATT_TQ = 256
ATT_TK = 512
MOE_TM = 256
CMB_TM = 512
NEG_INF = float("-inf")


def _cparams(sem):
    return pltpu.CompilerParams(dimension_semantics=sem, vmem_limit_bytes=VMEM_LIMIT)


def _silu(x):
    return x * jax.nn.sigmoid(x)


def _rms(x):
    return x * lax.rsqrt(jnp.mean(x * x, axis=-1, keepdims=True) + RMS_EPS)


def _ada_kernel(cond_ref, w_ref, b_ref, o_ref):
    s = _silu(cond_ref[...]).astype(BF16)
    o_ref[...] = jnp.dot(s, w_ref[...].astype(BF16), preferred_element_type=F32) + b_ref[...]


def _ada(cond, w, b):
    rows, d = cond.shape
    n = w.shape[1]
    return pl.pallas_call(
        _ada_kernel,
        out_shape=jax.ShapeDtypeStruct((rows, n), F32),
        grid=(n // ADA_TN,),
        in_specs=[pl.BlockSpec((rows, d), lambda j: (0, 0)),
                  pl.BlockSpec((d, ADA_TN), lambda j: (0, j)),
                  pl.BlockSpec((1, ADA_TN), lambda j: (0, j))],
        out_specs=pl.BlockSpec((rows, ADA_TN), lambda j: (0, j)),
        compiler_params=_cparams(("arbitrary",)),
        name="ada",
    )(cond, w, b.reshape(1, n))


def _in_kernel(x_ref, mod_ref, g_ref, w_ref, o_ref, h_ref):
    @pl.when(pl.program_id(1) == 0)
    def _():
        y = _rms(x_ref[...]) * g_ref[...]
        h_ref[...] = (y * (1.0 + mod_ref[0, 1:2, :]) + mod_ref[0, 0:1, :]).astype(BF16)

    o_ref[...] = jnp.dot(h_ref[...], w_ref[...], preferred_element_type=F32)


def _in_proj(x, mods, mod_row0, g, w_bf16, tm, name):
    batch, t, d = x.shape
    cols = w_bf16.shape[1]
    nt = t // tm
    return pl.pallas_call(
        _in_kernel,
        out_shape=jax.ShapeDtypeStruct((batch * t, cols), F32),
        grid=(batch * nt, cols // IN_TN),
        in_specs=[pl.BlockSpec((tm, d), lambda i, j: (i, 0)),
                  pl.BlockSpec((1, N_ADA, d), lambda i, j: (mod_row0 + i // nt, 0, 0)),
                  pl.BlockSpec((1, d), lambda i, j: (0, 0)),
                  pl.BlockSpec((d, IN_TN), lambda i, j: (0, j))],
        out_specs=pl.BlockSpec((tm, IN_TN), lambda i, j: (i, j)),
        scratch_shapes=[pltpu.VMEM((tm, d), BF16)],
        compiler_params=_cparams(("arbitrary", "arbitrary")),
        name=name,
    )(x.reshape(batch * t, d), mods, g.reshape(1, d), w_bf16)


def _rope(x, cos, sin_signed):
    lane = lax.broadcasted_iota(I32, x.shape, 1)
    first = (lane % 32) < 16
    partner = jnp.where(first, pltpu.roll(x, LANES - 16, 1), pltpu.roll(x, 16, 1))
    return x * cos + partner * sin_signed


def _split_maps(k):
    lane = lax.broadcasted_iota(I32, k.shape, 1)
    m1 = lane < DA_DK
    return jnp.where(m1, k, 0.0).astype(BF16), jnp.where(m1, 0.0, k).astype(BF16)


def _prep_kernel(*refs, use_rope, emit_cache):
    if use_rope:
        q_ref, k_ref, v_ref, cos_ref, sin_ref = refs[:5]
        outs = refs[5:]
    else:
        q_ref, k_ref, v_ref = refs[:3]
        outs = refs[3:]
    q = q_ref[...]
    k = k_ref[...]
    v = v_ref[...]
    if emit_cache:
        outs[4][0, 0, 0] = k
        outs[5][0, 0, 0] = v
    if use_rope:
        cos = cos_ref[...]
        sin = sin_ref[...]
        q = _rope(q, cos, sin)
        k = _rope(k, cos, sin)
    outs[0][0, 0] = (q * (DA_DK ** -0.5)).astype(BF16)
    k1, k2 = _split_maps(k)
    outs[1][0, 0] = k1
    outs[2][0, 0] = k2
    outs[3][0, 0] = v.astype(BF16)


def _prep(proj, batch, t, tr, cos=None, sin=None, emit_cache=False):
    use_rope = cos is not None
    nt = t // tr
    grid = (batch, nt, DA_HEADS)

    def col(cb):
        return pl.BlockSpec((tr, HEAD_W), lambda b, i, h: (b * nt + i, cb + h))

    in_specs = [col(CB_Q_DA), col(CB_K_DA), col(CB_V_DA)]
    args = [proj, proj, proj]
    if use_rope:
        in_specs += [pl.BlockSpec((tr, HEAD_W), lambda b, i, h: (i, 0))] * 2
        args += [cos, sin]
    bht = jax.ShapeDtypeStruct((batch, DA_HEADS, t, HEAD_W), BF16)
    bht_spec = pl.BlockSpec((1, 1, tr, HEAD_W), lambda b, i, h: (b, h, i, 0))
    out_shape = [bht] * 4
    out_specs = [bht_spec] * 4
    if emit_cache:
        cache = jax.ShapeDtypeStruct((batch, 1, DA_HEADS, t, HEAD_W), F32)
        cache_spec = pl.BlockSpec((1, 1, 1, tr, HEAD_W), lambda b, i, h: (b, 0, h, i, 0))
        out_shape += [cache] * 2
        out_specs += [cache_spec] * 2
    return pl.pallas_call(
        functools.partial(_prep_kernel, use_rope=use_rope, emit_cache=emit_cache),
        out_shape=out_shape,
        grid=grid,
        in_specs=in_specs,
        out_specs=out_specs,
        compiler_params=_cparams(("arbitrary",) * 3),
        name="prep_lat" if use_rope else "prep_ctx",
    )(*args)


def _lower_bound(lb_ref):
    p = lb_ref[...]
    e = jnp.exp(p - jnp.max(p, axis=0, keepdims=True))
    return e[0:1, :] / jnp.sum(e, axis=0, keepdims=True)


def _hgrn_chunk(q, f_logit, v, lb, st_ref, rev):
    c = q.shape[0]
    levels = c.bit_length() - 1
    fg = lb + (1.0 - lb) * jax.nn.sigmoid(f_logit)
    g = jnp.log(fg)
    k = 1.0 - fg
    row = lax.broadcasted_iota(I32, (c, LANES), 0)
    ti = lax.broadcasted_iota(I32, (c, c), 0)
    si = lax.broadcasted_iota(I32, (c, c), 1)
    causal = (si >= ti) if rev else (si <= ti)

    tot = g
    pre = g
    suf = jnp.zeros_like(g)
    scores = jnp.zeros((c, c), F32)
    nt = (((1,), (1,)), ((), ()))
    for j in range(levels):
        sh = 1 << j
        bit = ((row >> j) & 1) == 1
        late = jnp.logical_not(bit) if rev else bit
        e = jnp.exp(jnp.where(late, pre, suf))
        qe = q * e
        if j == 0:
            qt = jnp.concatenate([qe, q], axis=1)
            kt = jnp.concatenate([jnp.where(late, 0.0, k), jnp.where(late, k, 0.0)], axis=1)
            mask = ((ti >> 1) == (si >> 1)) & causal
        else:
            qt = jnp.where(late, qe, 0.0)
            kt = jnp.where(late, 0.0, k * e)
            mask = (ti >> (j + 1)) == (si >> (j + 1))
        part = lax.dot_general(qt.astype(BF16), kt.astype(BF16), nt, preferred_element_type=F32)
        scores = scores + jnp.where(mask, part, 0.0)
        up = pltpu.roll(tot, sh, 0)
        dn = pltpu.roll(tot, c - sh, 0)
        from_early, from_late = (dn, up) if rev else (up, dn)
        pre = pre + jnp.where(late, from_early, 0.0)
        suf = suf + jnp.where(late, 0.0, from_late)
        tot = tot + jnp.where(late, from_early, from_late)

    st = st_ref[...]
    q_dec = (q * jnp.exp(pre)).astype(BF16)
    o = lax.dot_general(q_dec, st.astype(BF16), nt, preferred_element_type=F32)
    o = o + jnp.dot(scores.astype(BF16), v.astype(BF16), preferred_element_type=F32)
    k_dec = (k * jnp.exp(suf)).astype(BF16)
    st_ref[...] = jnp.exp(tot[0:1, :]) * st + jnp.dot(v.T.astype(BF16), k_dec, preferred_element_type=F32)
    return o


def _hgrn_kernel(*refs, t, has_init, emit_state):
    q_ref, ff_ref, fb_ref, i_ref, g_ref, lbf_ref, lbb_ref, ng_ref = refs[:8]
    pos = 8
    if has_init:
        s0f_ref, s0b_ref = refs[pos:pos + 2]
        pos += 2
    o_ref = refs[pos]
    pos += 1
    if emit_state:
        sf_ref, sb_ref = refs[pos:pos + 2]
        pos += 2
    of_scr, ob_scr, stf, stb = refs[pos:pos + 4]

    c = HG_CHUNK
    n = t // c
    if has_init:
        stf[...] = s0f_ref[0, 0, 0].T
        stb[...] = s0b_ref[0, 0, 0].T
    else:
        stf[...] = jnp.zeros_like(stf)
        stb[...] = jnp.zeros_like(stb)
    lbf = _lower_bound(lbf_ref)
    lbb = _lower_bound(lbb_ref)

    def body(i, carry):
        sl = pl.ds(pl.multiple_of(i * c, c), c)
        of_scr[sl, :] = _hgrn_chunk(_silu(q_ref[sl, :]), ff_ref[sl, :], i_ref[sl, :], lbf, stf, False)
        sl = pl.ds(pl.multiple_of((n - 1 - i) * c, c), c)
        ob_scr[sl, :] = _hgrn_chunk(_silu(q_ref[sl, :]), fb_ref[sl, :], i_ref[sl, :], lbb, stb, True)
        return carry

    lax.fori_loop(0, n, body, 0)

    fin = min(t, 256)

    def finish(i, carry):
        sl = pl.ds(pl.multiple_of(i * fin, fin), fin)
        o = _rms(of_scr[sl, :] + ob_scr[sl, :]) * ng_ref[...]
        o_ref[sl, :] = (o * _silu(g_ref[sl, :])).astype(BF16)
        return carry

    lax.fori_loop(0, t // fin, finish, 0)
    if emit_state:
        sf_ref[0, 0, 0] = stf[...].T
        sb_ref[0, 0, 0] = stb[...].T


def _hgrn(proj, batch, t, lb_fwd, lb_bwd, norm_g, s0f=None, s0b=None, emit_state=False):
    has_init = s0f is not None

    def col(cb):
        return pl.BlockSpec((t, HEAD_W), lambda b, h: (b, cb + h))

    lb_spec = pl.BlockSpec((lb_fwd.shape[0], HEAD_W), lambda b, h: (0, h))
    in_specs = [col(CB_Q_HG), col(CB_F_FW), col(CB_F_BW), col(CB_I_HG), col(CB_G_HG),
                lb_spec, lb_spec, pl.BlockSpec((1, HEAD_W), lambda b, h: (0, 0))]
    args = [proj] * 5 + [lb_fwd, lb_bwd, norm_g.reshape(1, HEAD_W)]
    st_spec = pl.BlockSpec((1, 1, 1, HG_DK, HEAD_W), lambda b, h: (b, 0, h, 0, 0))
    if has_init:
        in_specs += [st_spec, st_spec]
        args += [s0f, s0b]
    out_shape = [jax.ShapeDtypeStruct((batch * t, HG_WIDTH), BF16)]
    out_specs = [pl.BlockSpec((t, HEAD_W), lambda b, h: (b, h))]
    if emit_state:
        st = jax.ShapeDtypeStruct((batch, 1, HG_HEADS, HG_DK, HEAD_W), F32)
        out_shape += [st, st]
        out_specs += [st_spec, st_spec]
    return pl.pallas_call(
        functools.partial(_hgrn_kernel, t=t, has_init=has_init, emit_state=emit_state),
        out_shape=out_shape,
        grid=(batch, HG_HEADS),
        in_specs=in_specs,
        out_specs=out_specs,
        scratch_shapes=[pltpu.VMEM((t, HEAD_W), F32), pltpu.VMEM((t, HEAD_W), F32),
                        pltpu.VMEM((HEAD_W, HG_DK), F32), pltpu.VMEM((HEAD_W, HG_DK), F32)],
        compiler_params=_cparams(("arbitrary", "arbitrary")),
        name="hgrn_lat" if has_init else "hgrn_ctx",
    )(*args)


def _attn_kernel(*refs, key_tiles, lam_init):
    n_groups = len(key_tiles)
    q_ref = refs[0]
    kv_refs = refs[1:1 + 3 * n_groups]
    pos = 1 + 3 * n_groups
    lq1_ref, lk1_ref, lq2_ref, lk2_ref, ng_ref, o_ref = refs[pos:pos + 6]
    pos += 6
    s_scrs = refs[pos:pos + n_groups]
    m_scr, l_scr, a_scr = refs[pos + n_groups:]

    q = q_ref[0, 0]
    nt = (((1,), (1,)), ((), ()))
    m_scr[...] = jnp.full_like(m_scr, NEG_INF)
    l_scr[...] = jnp.zeros_like(l_scr)
    a_scr[...] = jnp.zeros_like(a_scr)

    def lane_fold(x, op):
        out = x[:, 0:LANES]
        for j in range(1, x.shape[1] // LANES):
            out = op(out, x[:, j * LANES:(j + 1) * LANES])
        return out

    for gi, (n_tiles, tk) in enumerate(key_tiles):
        k_refs = kv_refs[3 * gi:3 * gi + 2]
        s_scr = s_scrs[gi]

        def scores(i, carry, k_refs=k_refs, s_scr=s_scr, tk=tk):
            sl = pl.ds(pl.multiple_of(i * tk, tk), tk)
            for mp in range(2):
                s = lax.dot_general(q, k_refs[mp][0, 0, sl, :], nt, preferred_element_type=F32)
                s_scr[mp, i] = s
                m_scr[mp] = jnp.maximum(m_scr[mp], lane_fold(s, jnp.maximum))
            return carry

        lax.fori_loop(0, n_tiles, scores, 0)

    for mp in range(2):
        m_scr[mp] = jnp.broadcast_to(jnp.max(m_scr[mp], axis=-1, keepdims=True), m_scr.shape[1:])

    for gi, (n_tiles, tk) in enumerate(key_tiles):
        v_ref = kv_refs[3 * gi + 2]
        s_scr = s_scrs[gi]

        def values(i, carry, v_ref=v_ref, s_scr=s_scr, tk=tk):
            sl = pl.ds(pl.multiple_of(i * tk, tk), tk)
            v = v_ref[0, 0, sl, :]
            for mp in range(2):
                m = m_scr[mp]
                s = s_scr[mp, i]
                es = [jnp.exp(s[:, j * LANES:(j + 1) * LANES] - m) for j in range(tk // LANES)]
                part = es[0]
                for e in es[1:]:
                    part = part + e
                l_scr[mp] = l_scr[mp] + part
                p = jnp.concatenate(es, axis=1).astype(BF16)
                a_scr[mp] = a_scr[mp] + jnp.dot(p, v, preferred_element_type=F32)
            return carry

        lax.fori_loop(0, n_tiles, values, 0)

    lam = (jnp.exp(jnp.sum(lq1_ref[...] * lk1_ref[...], axis=-1, keepdims=True))
           - jnp.exp(jnp.sum(lq2_ref[...] * lk2_ref[...], axis=-1, keepdims=True)) + lam_init)
    l1 = jnp.sum(l_scr[0], axis=-1, keepdims=True)
    l2 = jnp.sum(l_scr[1], axis=-1, keepdims=True)
    o = a_scr[0] / l1 - lam * (a_scr[1] / l2)
    o_ref[...] = (_rms(o) * ng_ref[...] * (1.0 - lam_init)).astype(BF16)


def _attn(q, key_groups, lam_params, norm_g, lam_init):
    batch, heads, t, _ = q.shape
    nq = t // ATT_TQ
    in_specs = [pl.BlockSpec((1, 1, ATT_TQ, HEAD_W), lambda b, h, i: (b, h, i, 0))]
    args = [q]
    key_tiles = []
    scratch = []
    for k1, k2, v, tk in key_groups:
        tkk = k1.shape[2]
        spec = pl.BlockSpec((1, 1, tkk, HEAD_W), lambda b, h, i: (b, h, 0, 0))
        in_specs += [spec] * 3
        args += [k1, k2, v]
        key_tiles.append((tkk // tk, tk))
        scratch.append(pltpu.VMEM((2, tkk // tk, ATT_TQ, tk), F32))
    small = pl.BlockSpec((1, DA_DK), lambda b, h, i: (0, 0))
    in_specs += [small] * 4 + [pl.BlockSpec((1, HEAD_W), lambda b, h, i: (0, 0))]
    args += [p.reshape(1, DA_DK) for p in lam_params] + [norm_g.reshape(1, HEAD_W)]
    scratch += [pltpu.VMEM((2, ATT_TQ, LANES), F32)] * 3
    return pl.pallas_call(
        functools.partial(_attn_kernel, key_tiles=tuple(key_tiles), lam_init=lam_init),
        out_shape=jax.ShapeDtypeStruct((batch * t, heads * HEAD_W), BF16),
        grid=(batch, heads, nq),
        in_specs=in_specs,
        out_specs=pl.BlockSpec((ATT_TQ, HEAD_W), lambda b, h, i: (b * nq + i, h)),
        scratch_shapes=scratch,
        compiler_params=_cparams(("arbitrary",) * 3),
        name="attn_lat" if len(key_groups) > 1 else "attn_ctx",
    )(*args)


def _out_kernel(mhg_c, mda_c, x_c, mhg_l, mda_l, x_l, w_ref, mod_ref, g_ref, rw_ref, rb_ref,
                x1_ref, h2_ref, ids_ref, wts_ref, *, ctx_tiles):
    def body(mhg_ref, mda_ref, x_ref):
        mix = jnp.dot(mhg_ref[...], w_ref[0:HG_WIDTH, :], preferred_element_type=F32)
        mix = mix + jnp.dot(mda_ref[...], w_ref[HG_WIDTH:, :], preferred_element_type=F32)
        x1 = x_ref[...] + mod_ref[0, 2:3, :] * mix
        x1_ref[...] = x1
        h2 = (_rms(x1) * g_ref[...]) * (1.0 + mod_ref[0, 4:5, :]) + mod_ref[0, 3:4, :]
        h2_ref[...] = h2

        logit = jnp.dot(h2, rw_ref[...], preferred_element_type=F32, precision=lax.Precision.HIGHEST)
        logit = logit + rb_ref[...]
        lane = lax.broadcasted_iota(I32, logit.shape, 1)

        def first_max(x):
            m = jnp.max(x, axis=-1, keepdims=True)
            return m, jnp.min(jnp.where(x == m, lane, LANES), axis=-1, keepdims=True)

        gmask = lane < N_GROUPS
        gmax, gsel = first_max(jnp.where(gmask, logit, NEG_INF))
        p_grp = 1.0 / jnp.sum(jnp.where(gmask, jnp.exp(logit - gmax), 0.0), axis=-1, keepdims=True)
        lo = N_GROUPS + EXP_PER_GROUP * gsel
        le = jnp.where((lane >= lo) & (lane < lo + EXP_PER_GROUP), logit, NEG_INF)
        v1, i1 = first_max(le)
        v2, i2 = first_max(jnp.where(lane == i1, NEG_INF, le))
        e = jnp.exp(v2 - v1)
        w1 = p_grp / (1.0 + e)
        w2 = p_grp * e / (1.0 + e)
        ids_ref[...] = jnp.where(lane == 0, i1 - N_GROUPS, jnp.where(lane == 1, i2 - N_GROUPS, 0))
        wts_ref[...] = jnp.where(lane == 0, w1, jnp.where(lane == 1, w2, 0.0))

    i = pl.program_id(0)
    pl.when(i < ctx_tiles)(lambda: body(mhg_c, mda_c, x_c))
    pl.when(i >= ctx_tiles)(lambda: body(mhg_l, mda_l, x_l))


def _out_proj(mhg_c, mda_c, x_c, mhg_l, mda_l, x_l, w_bf16, mods, g, rw, rb, lat_t):
    n_ctx, d = x_c.shape
    n_lat = x_l.shape[0]
    n = n_ctx + n_lat
    ctx_tiles = n_ctx // OUT_TM
    lat_tiles = lat_t // OUT_TM
    row = lambda i: (i, 0)
    const = lambda i: (0, 0)
    crow = lambda i: (jnp.minimum(i, ctx_tiles - 1), 0)
    lrow = lambda i: (jnp.maximum(i - ctx_tiles, 0), 0)
    seg = lambda i: (jnp.where(i < ctx_tiles, 0, 1 + (i - ctx_tiles) // lat_tiles), 0, 0)
    half = lambda m: pl.BlockSpec((OUT_TM, HG_WIDTH), m)
    full = lambda m: pl.BlockSpec((OUT_TM, d), m)
    return pl.pallas_call(
        functools.partial(_out_kernel, ctx_tiles=ctx_tiles),
        out_shape=[jax.ShapeDtypeStruct((n, d), F32), jax.ShapeDtypeStruct((n, d), F32),
                   jax.ShapeDtypeStruct((n, LANES), I32), jax.ShapeDtypeStruct((n, LANES), F32)],
        grid=(n // OUT_TM,),
        in_specs=[half(crow), half(crow), full(crow), half(lrow), half(lrow), full(lrow),
                  pl.BlockSpec((d, d), const), pl.BlockSpec((1, N_ADA, d), seg), pl.BlockSpec((1, d), const),
                  pl.BlockSpec((d, LANES), const), pl.BlockSpec((1, LANES), const)],
        out_specs=[full(row), full(row), pl.BlockSpec((OUT_TM, LANES), row), pl.BlockSpec((OUT_TM, LANES), row)],
        compiler_params=_cparams(("arbitrary",)),
        name="out_proj",
    )(mhg_c, mda_c, x_c, mhg_l, mda_l, x_l, w_bf16, mods, g.reshape(1, d), rw, rb)


def _gmm_kernel(src_ref, texp_ref, ntile_ref, h_hbm, w1_ref, w3_ref, w2_ref, y_hbm,
                x0, x1, y0, y1, sems, *, n_tok):
    i = pl.program_id(0)
    n_tiles = ntile_ref[0]
    xs = (x0, x1)
    ys = (y0, y1)

    def gather_copy(tile, r, slot):
        p = jnp.maximum(src_ref[(tile + 1) * MOE_TM + r], 0)
        tok = jnp.where(p >= n_tok, p - n_tok, p)
        return pltpu.make_async_copy(h_hbm.at[pl.ds(tok, 1)], xs[slot].at[pl.ds(r, 1)], sems.at[slot])

    def scatter_copy(tile, r, slot):
        p = src_ref[(tile + 1) * MOE_TM + r]
        dst = jnp.where(p >= 0, p, TOP_K * n_tok + r)
        return pltpu.make_async_copy(ys[slot].at[pl.ds(r, 1)], y_hbm.at[pl.ds(dst, 1)], sems.at[2 + slot])

    @pl.when(i == 0)
    def _():
        y1[...] = jnp.zeros_like(y1)
        for r in range(MOE_TM):
            gather_copy(0, r, 0).start()

    def step(cur):
        nxt = 1 - cur
        for r in range(MOE_TM):
            gather_copy(i, r, cur).wait()
        for r in range(MOE_TM):
            gather_copy(i + 1, r, nxt).start()
        for r in range(MOE_TM):
            scatter_copy(i - 1, r, nxt).start()
        x = xs[cur][...].astype(BF16)
        a = jnp.dot(x, w1_ref[0].astype(BF16), preferred_element_type=F32)
        b = jnp.dot(x, w3_ref[0].astype(BF16), preferred_element_type=F32)
        hid = (_silu(a) * b).astype(BF16)
        ys[cur][...] = jnp.dot(hid, w2_ref[0].astype(BF16), preferred_element_type=F32)
        for r in range(MOE_TM):
            scatter_copy(i - 1, r, nxt).wait()

        @pl.when(i == n_tiles)
        def _():
            for r in range(MOE_TM):
                gather_copy(i + 1, r, nxt).wait()

    live = i <= n_tiles
    pl.when(live & (i % 2 == 0))(lambda: step(0))
    pl.when(live & (i % 2 == 1))(lambda: step(1))


def _gmm(src, tile_expert, n_tiles, h2, w1, w3, w2, n_tok):
    d = h2.shape[1]
    steps = tile_expert.shape[0]
    wspec = lambda shape: pl.BlockSpec((1,) + shape, lambda i, src, te, nt: (te[i], 0, 0))
    return pl.pallas_call(
        functools.partial(_gmm_kernel, n_tok=n_tok),
        out_shape=jax.ShapeDtypeStruct((TOP_K * n_tok + MOE_TM, d), F32),
        grid_spec=pltpu.PrefetchScalarGridSpec(
            num_scalar_prefetch=3,
            grid=(steps,),
            in_specs=[pl.BlockSpec(memory_space=pl.ANY),
                      wspec((d, D_EXPERT)), wspec((d, D_EXPERT)), wspec((D_EXPERT, d))],
            out_specs=pl.BlockSpec(memory_space=pl.ANY),
            scratch_shapes=[pltpu.VMEM((MOE_TM, d), F32)] * 4 + [pltpu.SemaphoreType.DMA((4,))],
        ),
        compiler_params=_cparams(("arbitrary",)),
        name="gmm",
    )(src, tile_expert, n_tiles, h2, w1, w3, w2)


def _route_tables(ids, n_tok):
    n_pairs = n_tok * TOP_K
    max_tiles = n_pairs // MOE_TM + N_EXPERTS
    e_flat = ids[:, :TOP_K].T.reshape(n_pairs)
    onehot = (e_flat[:, None] == jnp.arange(N_EXPERTS, dtype=I32)[None, :]).astype(I32)
    csum = jnp.cumsum(onehot, axis=0)
    counts = csum[-1]
    rank = jnp.sum(csum * onehot, axis=1) - 1
    tiles_per = (counts + MOE_TM - 1) // MOE_TM
    tile_end = jnp.cumsum(tiles_per)
    row_start = (tile_end - tiles_per + 1) * MOE_TM
    pos = row_start[e_flat] + rank
    src = jnp.full(((max_tiles + 3) * MOE_TM,), -1, I32).at[pos].set(jnp.arange(n_pairs, dtype=I32))
    steps = jnp.arange(max_tiles + 1, dtype=I32)
    owner = jnp.sum((tile_end[None, :] <= steps[:, None]).astype(I32), axis=1)
    tile_expert = jnp.minimum(owner, N_EXPERTS - 1)
    return src, tile_expert, tile_end[-1:].astype(I32)


def _combine_kernel(x1_ref, y0_ref, y1_ref, wts_ref, mod_ref, g_ref, o_ref):
    w = wts_ref[...]
    moe = w[:, 0:1] * y0_ref[...] + w[:, 1:2] * y1_ref[...]
    x2 = x1_ref[...] + mod_ref[0, 5:6, :] * moe
    o_ref[...] = _rms(x2) * g_ref[...]


def _combine(x1, y_pairs, wts, mods, mod_row0, g, row0, batch, t, n_tok):
    d = x1.shape[1]
    nt = t // CMB_TM
    rb0 = row0 // CMB_TM
    slot1 = n_tok // CMB_TM
    return pl.pallas_call(
        _combine_kernel,
        out_shape=jax.ShapeDtypeStruct((batch * t, d), F32),
        grid=(batch * nt,),
        in_specs=[pl.BlockSpec((CMB_TM, d), lambda i: (rb0 + i, 0)),
                  pl.BlockSpec((CMB_TM, d), lambda i: (rb0 + i, 0)),
                  pl.BlockSpec((CMB_TM, d), lambda i: (slot1 + rb0 + i, 0)),
                  pl.BlockSpec((CMB_TM, LANES), lambda i: (rb0 + i, 0)),
                  pl.BlockSpec((1, N_ADA, d), lambda i: (mod_row0 + i // nt, 0, 0)),
                  pl.BlockSpec((1, d), lambda i: (0, 0))],
        out_specs=pl.BlockSpec((CMB_TM, d), lambda i: (i, 0)),
        compiler_params=_cparams(("arbitrary",)),
        name="combine",
    )(x1, y_pairs, y_pairs, wts, mods, g.reshape(1, d))


def _rope_tables(n_lat):
    half = DA_DK // 2
    nf = half // 2
    t = jnp.arange(n_lat)
    row = (t // GRID_W).astype(F32)
    col = (t % GRID_W).astype(F32)
    inv = ROPE_THETA ** (-jnp.arange(nf, dtype=F32) / nf)
    lane = np.arange(HEAD_W)
    freq = inv[lane % nf]
    pos = jnp.where(((lane // half) % 2 == 0)[None, :], row[:, None], col[:, None])
    ang = pos * freq[None, :]
    sign = np.where((lane % half) < nf, -1.0, 1.0).astype(np.float32)
    return jnp.cos(ang), jnp.sin(ang) * sign[None, :]


def kernel(x_prompt, x_sample, cache_k, cache_v, state_hgrn_fwd, state_hgrn_bwd, c, c_ctx, w_ada, b_ada,
           norm1_g, norm2_g, norm_final_g, w_in, hg_lb_fwd, hg_lb_bwd, hg_norm_g, da_lambda_q1, da_lambda_k1,
           da_lambda_q2, da_lambda_k2, da_norm_g, w_out, router_g_w, router_g_b, router_e_w, router_e_b,
           exp_w1, exp_w3, exp_w2):
    l = 0
    batch, seq, d = x_prompt.shape
    dec_batch, dec_seq, _ = x_sample.shape
    n_ctx = batch * seq
    n_lat = dec_batch * dec_seq
    n_tok = n_ctx + n_lat
    lam_init = 0.8 - 0.6 * math.exp(-0.3 * l)

    cond = jnp.zeros((8, d), F32).at[0].set(c_ctx).at[1:1 + dec_batch].set(c)
    mods = _ada(cond, w_ada[l], b_ada[l]).reshape(8, N_ADA, d)

    w_in_bf16 = w_in[l].astype(BF16)
    proj_c = _in_proj(x_prompt.reshape(1, n_ctx, d), mods, 0, norm1_g[l], w_in_bf16, 1024, "in_proj_ctx")
    proj_l = _in_proj(x_sample, mods, 1, norm1_g[l], w_in_bf16, 1024, "in_proj_lat")
    lam_params = (da_lambda_q1[l], da_lambda_k1[l], da_lambda_q2[l], da_lambda_k2[l])

    qc, k1c, k2c, vc, new_k, new_v = _prep(proj_c, batch, seq, seq, emit_cache=True)
    da_ctx = _attn(qc, [(k1c, k2c, vc, seq)], lam_params, da_norm_g[l], lam_init)
    hg_ctx, new_sf, new_sb = _hgrn(proj_c, batch, seq, hg_lb_fwd, hg_lb_bwd, hg_norm_g[l], emit_state=True)

    cos, sin = _rope_tables(dec_seq)
    ql, k1l, k2l, vl = _prep(proj_l, dec_batch, dec_seq, 512, cos=cos, sin=sin)
    ck = cache_k[:, l]
    lane = jnp.arange(HEAD_W)
    ck1 = jnp.where(lane < DA_DK, ck, 0.0).astype(BF16)
    ck2 = jnp.where(lane < DA_DK, 0.0, ck).astype(BF16)
    cv = cache_v[:, l].astype(BF16)
    da_lat = _attn(ql, [(k1l, k2l, vl, ATT_TK), (ck1, ck2, cv, ck.shape[2])], lam_params, da_norm_g[l], lam_init)
    (hg_lat,) = _hgrn(proj_l, dec_batch, dec_seq, hg_lb_fwd, hg_lb_bwd, hg_norm_g[l],
                      s0f=state_hgrn_fwd[:, l:l + 1], s0b=state_hgrn_bwd[:, l:l + 1])

    rw = jnp.zeros((d, LANES), F32).at[:, :N_GROUPS].set(router_g_w[l]).at[:, N_GROUPS:N_GROUPS + N_EXPERTS].set(
        router_e_w[l])
    rb = jnp.zeros((1, LANES), F32).at[0, :N_GROUPS].set(router_g_b[l]).at[0, N_GROUPS:N_GROUPS + N_EXPERTS].set(
        router_e_b[l])
    x1, h2, ids, wts = _out_proj(hg_ctx, da_ctx, x_prompt.reshape(n_ctx, d), hg_lat, da_lat,
                                 x_sample.reshape(n_lat, d), w_out[l].astype(BF16), mods, norm2_g[l], rw, rb, dec_seq)

    src, tile_expert, n_tiles = _route_tables(ids, n_tok)
    y_pairs = _gmm(src, tile_expert, n_tiles, h2, exp_w1[l], exp_w3[l], exp_w2[l], n_tok)

    y_ctx = _combine(x1, y_pairs, wts, mods, 0, norm_final_g, 0, 1, n_ctx, n_tok)
    y_lat = _combine(x1, y_pairs, wts, mods, 1, norm_final_g, n_ctx, dec_batch, dec_seq, n_tok)
    return (y_ctx.reshape(batch, seq, d), y_lat.reshape(dec_batch, dec_seq, d), new_k, new_v, new_sf, new_sb)
```

```python
import functools
import math

import numpy as np
import jax
import jax.numpy as jnp
from jax import lax
from jax.experimental import pallas as pl
from jax.experimental.pallas import tpu as pltpu

F32 = jnp.float32
BF16 = jnp.bfloat16
I32 = jnp.int32

GRID_W = 64
HG_WIDTH = 1024
HG_DK = 128
HG_HEADS = 8
DA_HEADS = 8
DA_DK = 64
HEAD_W = 128
ROPE_THETA = 10000.0
N_GROUPS = 4
EXP_PER_GROUP = 8
N_EXPERTS = 32
TOP_K = 2
D_EXPERT = 512
N_ADA = 6
RMS_EPS = 1e-6
CB_Q_HG, CB_F_FW, CB_F_BW, CB_I_HG, CB_G_HG, CB_Q_DA, CB_K_DA, CB_V_DA = (8 * i for i in range(8))

LANES = 128
VMEM_LIMIT = 56 * 1024 * 1024

ADA_TN = 1536
IN_TN = 512
OUT_TM = 256
HG_CHUNK = 64
HG_UNROLL = 2
ATT_TQ = 256
ATT_TK = 512
MOE_TM = 256
CMB_TM = 512
NEG_INF = float("-inf")


def _cparams(sem):
    return pltpu.CompilerParams(dimension_semantics=sem, vmem_limit_bytes=VMEM_LIMIT)


def _silu(x):
    return x * jax.nn.sigmoid(x)


def _rms(x):
    return x * lax.rsqrt(jnp.mean(x * x, axis=-1, keepdims=True) + RMS_EPS)


def _ada_kernel(cond_ref, w_ref, b_ref, o_ref):
    s = _silu(cond_ref[...]).astype(BF16)
    o_ref[...] = jnp.dot(s, w_ref[...].astype(BF16), preferred_element_type=F32) + b_ref[...]


def _ada(cond, w, b):
    rows, d = cond.shape
    n = w.shape[1]
    return pl.pallas_call(
        _ada_kernel,
        out_shape=jax.ShapeDtypeStruct((rows, n), F32),
        grid=(n // ADA_TN,),
        in_specs=[pl.BlockSpec((rows, d), lambda j: (0, 0)),
                  pl.BlockSpec((d, ADA_TN), lambda j: (0, j)),
                  pl.BlockSpec((1, ADA_TN), lambda j: (0, j))],
        out_specs=pl.BlockSpec((rows, ADA_TN), lambda j: (0, j)),
        compiler_params=_cparams(("arbitrary",)),
        name="ada",
    )(cond, w, b.reshape(1, n))


def _in_kernel(x_ref, mod_ref, g_ref, w_ref, o_ref, h_ref):
    @pl.when(pl.program_id(1) == 0)
    def _():
        y = _rms(x_ref[...]) * g_ref[...]
        h_ref[...] = (y * (1.0 + mod_ref[0, 1:2, :]) + mod_ref[0, 0:1, :]).astype(BF16)

    o_ref[...] = jnp.dot(h_ref[...], w_ref[...], preferred_element_type=F32)


def _in_proj(x, mods, mod_row0, g, w_bf16, tm, name):
    batch, t, d = x.shape
    cols = w_bf16.shape[1]
    nt = t // tm
    return pl.pallas_call(
        _in_kernel,
        out_shape=jax.ShapeDtypeStruct((batch * t, cols), F32),
        grid=(batch * nt, cols // IN_TN),
        in_specs=[pl.BlockSpec((tm, d), lambda i, j: (i, 0)),
                  pl.BlockSpec((1, N_ADA, d), lambda i, j: (mod_row0 + i // nt, 0, 0)),
                  pl.BlockSpec((1, d), lambda i, j: (0, 0)),
                  pl.BlockSpec((d, IN_TN), lambda i, j: (0, j))],
        out_specs=pl.BlockSpec((tm, IN_TN), lambda i, j: (i, j)),
        scratch_shapes=[pltpu.VMEM((tm, d), BF16)],
        compiler_params=_cparams(("arbitrary", "arbitrary")),
        name=name,
    )(x.reshape(batch * t, d), mods, g.reshape(1, d), w_bf16)


def _rope(x, cos, sin_signed):
    lane = lax.broadcasted_iota(I32, x.shape, 1)
    first = (lane % 32) < 16
    partner = jnp.where(first, pltpu.roll(x, LANES - 16, 1), pltpu.roll(x, 16, 1))
    return x * cos + partner * sin_signed


def _split_maps(k):
    lane = lax.broadcasted_iota(I32, k.shape, 1)
    m1 = lane < DA_DK
    return jnp.where(m1, k, 0.0).astype(BF16), jnp.where(m1, 0.0, k).astype(BF16)


def _lower_bound(lb_ref):
    p = lb_ref[...]
    e = jnp.exp(p - jnp.max(p, axis=0, keepdims=True))
    return e[0:1, :] / jnp.sum(e, axis=0, keepdims=True)


def _hgrn_chunk(q, f_logit, v, lb, st_ref, rev):
    c = q.shape[0]
    levels = c.bit_length() - 1
    fg = lb + (1.0 - lb) * jax.nn.sigmoid(f_logit)
    g = jnp.log(fg)
    k = 1.0 - fg
    row = lax.broadcasted_iota(I32, (c, LANES), 0)
    ti = lax.broadcasted_iota(I32, (c, c), 0)
    si = lax.broadcasted_iota(I32, (c, c), 1)
    causal = (si >= ti) if rev else (si <= ti)

    tot = g
    pre = g
    suf = jnp.zeros_like(g)
    scores = jnp.zeros((c, c), F32)
    nt = (((1,), (1,)), ((), ()))
    for j in range(levels):
        sh = 1 << j
        bit = ((row >> j) & 1) == 1
        late = jnp.logical_not(bit) if rev else bit
        e = jnp.exp(jnp.where(late, pre, suf))
        qe = q * e
        if j == 0:
            qt = jnp.concatenate([qe, q], axis=1)
            kt = jnp.concatenate([jnp.where(late, 0.0, k), jnp.where(late, k, 0.0)], axis=1)
            mask = ((ti >> 1) == (si >> 1)) & causal
        else:
            qt = jnp.where(late, qe, 0.0)
            kt = jnp.where(late, 0.0, k * e)
            mask = (ti >> (j + 1)) == (si >> (j + 1))
        part = lax.dot_general(qt.astype(BF16), kt.astype(BF16), nt, preferred_element_type=F32)
        scores = scores + jnp.where(mask, part, 0.0)
        up = pltpu.roll(tot, sh, 0)
        dn = pltpu.roll(tot, c - sh, 0)
        from_early, from_late = (dn, up) if rev else (up, dn)
        pre = pre + jnp.where(late, from_early, 0.0)
        suf = suf + jnp.where(late, 0.0, from_late)
        tot = tot + jnp.where(late, from_early, from_late)

    st = st_ref[...]
    q_dec = (q * jnp.exp(pre)).astype(BF16)
    o = lax.dot_general(q_dec, st.astype(BF16), nt, preferred_element_type=F32)
    o = o + jnp.dot(scores.astype(BF16), v.astype(BF16), preferred_element_type=F32)
    k_dec = (k * jnp.exp(suf)).astype(BF16)
    st_ref[...] = jnp.exp(tot[0:1, :]) * st + jnp.dot(v.T.astype(BF16), k_dec, preferred_element_type=F32)
    return o


def _hgrn_kernel(*refs, t, has_init, emit_state):
    q_ref, ff_ref, fb_ref, i_ref, g_ref, lbf_ref, lbb_ref, ng_ref = refs[:8]
    pos = 8
    if has_init:
        s0f_ref, s0b_ref = refs[pos:pos + 2]
        pos += 2
    o_ref = refs[pos]
    pos += 1
    if emit_state:
        sf_ref, sb_ref = refs[pos:pos + 2]
        pos += 2
    of_scr, ob_scr, stf, stb = refs[pos:pos + 4]

    c = HG_CHUNK
    n = t // c
    if has_init:
        stf[...] = s0f_ref[0, 0, 0].T
        stb[...] = s0b_ref[0, 0, 0].T
    else:
        stf[...] = jnp.zeros_like(stf)
        stb[...] = jnp.zeros_like(stb)
    lbf = _lower_bound(lbf_ref)
    lbb = _lower_bound(lbb_ref)

    def one(i):
        sl = pl.ds(pl.multiple_of(i * c, c), c)
        of_scr[sl, :] = _hgrn_chunk(_silu(q_ref[sl, :]), ff_ref[sl, :], i_ref[sl, :], lbf, stf, False)
        sl = pl.ds(pl.multiple_of((n - 1 - i) * c, c), c)
        ob_scr[sl, :] = _hgrn_chunk(_silu(q_ref[sl, :]), fb_ref[sl, :], i_ref[sl, :], lbb, stb, True)

    def body(i, carry):
        for u in range(HG_UNROLL):
            one(i * HG_UNROLL + u)
        return carry

    lax.fori_loop(0, n // HG_UNROLL, body, 0)

    fin = min(t, 256)

    def finish(i, carry):
        sl = pl.ds(pl.multiple_of(i * fin, fin), fin)
        o = _rms(of_scr[sl, :] + ob_scr[sl, :]) * ng_ref[...]
        o_ref[sl, :] = (o * _silu(g_ref[sl, :])).astype(BF16)
        return carry

    lax.fori_loop(0, t // fin, finish, 0)
    if emit_state:
        sf_ref[0, 0, 0] = stf[...].T
        sb_ref[0, 0, 0] = stb[...].T


def _hgrn(proj, batch, t, lb_fwd, lb_bwd, norm_g, s0f=None, s0b=None, emit_state=False):
    has_init = s0f is not None

    def col(cb):
        return pl.BlockSpec((t, HEAD_W), lambda b, h: (b, cb + h))

    lb_spec = pl.BlockSpec((lb_fwd.shape[0], HEAD_W), lambda b, h: (0, h))
    in_specs = [col(CB_Q_HG), col(CB_F_FW), col(CB_F_BW), col(CB_I_HG), col(CB_G_HG),
                lb_spec, lb_spec, pl.BlockSpec((1, HEAD_W), lambda b, h: (0, 0))]
    args = [proj] * 5 + [lb_fwd, lb_bwd, norm_g.reshape(1, HEAD_W)]
    st_spec = pl.BlockSpec((1, 1, 1, HG_DK, HEAD_W), lambda b, h: (b, 0, h, 0, 0))
    if has_init:
        in_specs += [st_spec, st_spec]
        args += [s0f, s0b]
    out_shape = [jax.ShapeDtypeStruct((batch * t, HG_WIDTH), BF16)]
    out_specs = [pl.BlockSpec((t, HEAD_W), lambda b, h: (b, h))]
    if emit_state:
        st = jax.ShapeDtypeStruct((batch, 1, HG_HEADS, HG_DK, HEAD_W), F32)
        out_shape += [st, st]
        out_specs += [st_spec, st_spec]
    return pl.pallas_call(
        functools.partial(_hgrn_kernel, t=t, has_init=has_init, emit_state=emit_state),
        out_shape=out_shape,
        grid=(batch, HG_HEADS),
        in_specs=in_specs,
        out_specs=out_specs,
        scratch_shapes=[pltpu.VMEM((t, HEAD_W), F32), pltpu.VMEM((t, HEAD_W), F32),
                        pltpu.VMEM((HEAD_W, HG_DK), F32), pltpu.VMEM((HEAD_W, HG_DK), F32)],
        compiler_params=_cparams(("arbitrary", "arbitrary")),
        name="hgrn_lat" if has_init else "hgrn_ctx",
    )(*args)


def _attn_kernel(*refs, t, n_cache, use_rope, emit_cache, lam_init):
    it = iter(refs)
    q_ref, k_ref, v_ref = next(it), next(it), next(it)
    if use_rope:
        cosq_ref, sinq_ref, cos_ref, sin_ref = next(it), next(it), next(it), next(it)
    if n_cache:
        ck_ref, cv_ref = next(it), next(it)
    lq1_ref, lk1_ref, lq2_ref, lk2_ref, ng_ref, o_ref = (next(it) for _ in range(6))
    if emit_cache:
        nk_ref, nv_ref = next(it), next(it)
    k1_scr, k2_scr, v_scr, s_scr, p_scr = (next(it) for _ in range(5))
    k_scrs = (k1_scr, k2_scr)
    tq = q_ref.shape[0]

    @pl.when(pl.program_id(2) == 0)
    def _():
        rows = min(t, ATT_TK)

        def stage(i, carry):
            sl = pl.ds(pl.multiple_of(i * rows, rows), rows)
            k = k_ref[sl, :]
            v = v_ref[sl, :]
            if emit_cache:
                nk_ref[0, 0, 0, sl, :] = k
                nv_ref[0, 0, 0, sl, :] = v
            if use_rope:
                k = _rope(k, cos_ref[sl, :], sin_ref[sl, :])
            k1_scr[sl, :], k2_scr[sl, :] = _split_maps(k)
            v_scr[sl, :] = v.astype(BF16)
            return carry

        lax.fori_loop(0, t // rows, stage, 0)
        if n_cache:
            k1_scr[t:t + n_cache, :], k2_scr[t:t + n_cache, :] = _split_maps(ck_ref[0, 0, 0])
            v_scr[t:t + n_cache, :] = cv_ref[0, 0, 0].astype(BF16)

    q = q_ref[...]
    if use_rope:
        q = _rope(q, cosq_ref[...], sinq_ref[...])
    q = (q * (DA_DK ** -0.5)).astype(BF16)
    nt = (((1,), (1,)), ((), ()))
    n_keys = t + n_cache
    tiles = [(st, min(ATT_TK, n_keys - st)) for st in range(0, n_keys, ATT_TK)]

    row_max = []
    for mp in range(2):
        mx = jnp.full((tq, LANES), NEG_INF, F32)
        for st, sz in tiles:
            s = lax.dot_general(q, k_scrs[mp][st:st + sz, :], nt, preferred_element_type=F32)
            s_scr[mp, :, st:st + sz] = s
            for j in range(sz // LANES):
                mx = jnp.maximum(mx, s[:, j * LANES:(j + 1) * LANES])
        row_max.append(jnp.broadcast_to(jnp.max(mx, axis=-1, keepdims=True), (tq, LANES)))

    acc = []
    row_sum = []
    for mp in range(2):
        part = jnp.zeros((tq, LANES), F32)
        for st in range(0, n_keys, LANES):
            e = jnp.exp(s_scr[mp, :, st:st + LANES] - row_max[mp])
            part = part + e
            p_scr[mp, :, st:st + LANES] = e.astype(BF16)
        row_sum.append(jnp.sum(part, axis=-1, keepdims=True))
        acc.append(jnp.dot(p_scr[mp], v_scr[...], preferred_element_type=F32))

    lam = (jnp.exp(jnp.sum(lq1_ref[...] * lk1_ref[...], axis=-1, keepdims=True))
           - jnp.exp(jnp.sum(lq2_ref[...] * lk2_ref[...], axis=-1, keepdims=True)) + lam_init)
    o = acc[0] / row_sum[0] - lam * (acc[1] / row_sum[1])
    o_ref[...] = (_rms(o) * ng_ref[...] * (1.0 - lam_init)).astype(BF16)


def _attn(proj, batch, t, lam_params, norm_g, lam_init, cos=None, sin=None, cache_k=None, cache_v=None,
          emit_cache=False):
    use_rope = cos is not None
    n_cache = 0 if cache_k is None else cache_k.shape[3]
    nq = t // ATT_TQ
    n_keys = t + n_cache

    def col(cb, rows, row_map):
        return pl.BlockSpec((rows, HEAD_W), lambda b, h, i: (row_map(b, i), cb + h))

    in_specs = [col(CB_Q_DA, ATT_TQ, lambda b, i: b * nq + i),
                col(CB_K_DA, t, lambda b, i: b), col(CB_V_DA, t, lambda b, i: b)]
    args = [proj, proj, proj]
    if use_rope:
        in_specs += [pl.BlockSpec((ATT_TQ, HEAD_W), lambda b, h, i: (i, 0))] * 2
        in_specs += [pl.BlockSpec((t, HEAD_W), lambda b, h, i: (0, 0))] * 2
        args += [cos, sin, cos, sin]
    cache_spec = lambda n: pl.BlockSpec((1, 1, 1, n, HEAD_W), lambda b, h, i: (b, 0, h, 0, 0))
    if n_cache:
        in_specs += [cache_spec(n_cache)] * 2
        args += [cache_k, cache_v]
    small = pl.BlockSpec((1, DA_DK), lambda b, h, i: (0, 0))
    in_specs += [small] * 4 + [pl.BlockSpec((1, HEAD_W), lambda b, h, i: (0, 0))]
    args += [p.reshape(1, DA_DK) for p in lam_params] + [norm_g.reshape(1, HEAD_W)]
    out_shape = [jax.ShapeDtypeStruct((batch * t, DA_HEADS * HEAD_W), BF16)]
    out_specs = [pl.BlockSpec((ATT_TQ, HEAD_W), lambda b, h, i: (b * nq + i, h))]
    if emit_cache:
        out_shape += [jax.ShapeDtypeStruct((batch, 1, DA_HEADS, t, HEAD_W), F32)] * 2
        out_specs += [cache_spec(t)] * 2
    scratch = [pltpu.VMEM((n_keys, HEAD_W), BF16)] * 3
    scratch += [pltpu.VMEM((2, ATT_TQ, n_keys), F32), pltpu.VMEM((2, ATT_TQ, n_keys), BF16)]
    return pl.pallas_call(
        functools.partial(_attn_kernel, t=t, n_cache=n_cache, use_rope=use_rope, emit_cache=emit_cache,
                          lam_init=lam_init),
        out_shape=out_shape,
        grid=(batch, DA_HEADS, nq),
        in_specs=in_specs,
        out_specs=out_specs,
        scratch_shapes=scratch,
        compiler_params=_cparams(("arbitrary",) * 3),
        name="attn_lat" if use_rope else "attn_ctx",
    )(*args)


def _out_kernel(mhg_c, mda_c, x_c, mhg_l, mda_l, x_l, w_ref, mod_ref, g_ref, rw_ref, rb_ref,
                x1_ref, h2_ref, ids_ref, wts_ref, *, ctx_tiles):
    def body(mhg_ref, mda_ref, x_ref):
        mix = jnp.dot(mhg_ref[...], w_ref[0:HG_WIDTH, :], preferred_element_type=F32)
        mix = mix + jnp.dot(mda_ref[...], w_ref[HG_WIDTH:, :], preferred_element_type=F32)
        x1 = x_ref[...] + mod_ref[0, 2:3, :] * mix
        x1_ref[...] = x1
        h2 = (_rms(x1) * g_ref[...]) * (1.0 + mod_ref[0, 4:5, :]) + mod_ref[0, 3:4, :]
        h2_ref[...] = h2

        h_hi = h2.astype(BF16)
        h_lo = (h2 - h_hi.astype(F32)).astype(BF16)
        logit = (jnp.dot(h_hi, rw_ref[0], preferred_element_type=F32)
                 + jnp.dot(h_lo, rw_ref[0], preferred_element_type=F32)
                 + jnp.dot(h_hi, rw_ref[1], preferred_element_type=F32)) + rb_ref[...]
        lane = lax.broadcasted_iota(I32, logit.shape, 1)

        def first_max(x):
            m = jnp.max(x, axis=-1, keepdims=True)
            return m, jnp.min(jnp.where(x == m, lane, LANES), axis=-1, keepdims=True)

        gmask = lane < N_GROUPS
        gmax, gsel = first_max(jnp.where(gmask, logit, NEG_INF))
        p_grp = 1.0 / jnp.sum(jnp.where(gmask, jnp.exp(logit - gmax), 0.0), axis=-1, keepdims=True)
        lo = N_GROUPS + EXP_PER_GROUP * gsel
        le = jnp.where((lane >= lo) & (lane < lo + EXP_PER_GROUP), logit, NEG_INF)
        v1, i1 = first_max(le)
        v2, i2 = first_max(jnp.where(lane == i1, NEG_INF, le))
        e = jnp.exp(v2 - v1)
        w1 = p_grp / (1.0 + e)
        w2 = p_grp * e / (1.0 + e)
        ids_ref[...] = jnp.where(lane == 0, i1 - N_GROUPS, jnp.where(lane == 1, i2 - N_GROUPS, 0))
        wts_ref[...] = jnp.where(lane == 0, w1, jnp.where(lane == 1, w2, 0.0))

    i = pl.program_id(0)
    pl.when(i < ctx_tiles)(lambda: body(mhg_c, mda_c, x_c))
    pl.when(i >= ctx_tiles)(lambda: body(mhg_l, mda_l, x_l))


def _out_proj(mhg_c, mda_c, x_c, mhg_l, mda_l, x_l, w_bf16, mods, g, rw, rb, lat_t):
    n_ctx, d = x_c.shape
    n_lat = x_l.shape[0]
    n = n_ctx + n_lat
    ctx_tiles = n_ctx // OUT_TM
    lat_tiles = lat_t // OUT_TM
    row = lambda i: (i, 0)
    const = lambda i: (0, 0)
    crow = lambda i: (jnp.minimum(i, ctx_tiles - 1), 0)
    lrow = lambda i: (jnp.maximum(i - ctx_tiles, 0), 0)
    seg = lambda i: (jnp.where(i < ctx_tiles, 0, 1 + (i - ctx_tiles) // lat_tiles), 0, 0)
    half = lambda m: pl.BlockSpec((OUT_TM, HG_WIDTH), m)
    full = lambda m: pl.BlockSpec((OUT_TM, d), m)
    return pl.pallas_call(
        functools.partial(_out_kernel, ctx_tiles=ctx_tiles),
        out_shape=[jax.ShapeDtypeStruct((n, d), F32), jax.ShapeDtypeStruct((n, d), F32),
                   jax.ShapeDtypeStruct((n, LANES), I32), jax.ShapeDtypeStruct((n, LANES), F32)],
        grid=(n // OUT_TM,),
        in_specs=[half(crow), half(crow), full(crow), half(lrow), half(lrow), full(lrow),
                  pl.BlockSpec((d, d), const), pl.BlockSpec((1, N_ADA, d), seg), pl.BlockSpec((1, d), const),
                  pl.BlockSpec((2, d, LANES), lambda i: (0, 0, 0)), pl.BlockSpec((1, LANES), const)],
        out_specs=[full(row), full(row), pl.BlockSpec((OUT_TM, LANES), row), pl.BlockSpec((OUT_TM, LANES), row)],
        compiler_params=_cparams(("arbitrary",)),
        name="out_proj",
    )(mhg_c, mda_c, x_c, mhg_l, mda_l, x_l, w_bf16, mods, g.reshape(1, d), rw, rb)


def _gmm_kernel(src_ref, texp_ref, ntile_ref, h_hbm, w1_ref, w3_ref, w2_ref, y_hbm,
                x0, x1, y0, y1, sems, *, n_tok):
    i = pl.program_id(0)
    n_tiles = ntile_ref[0]
    xs = (x0, x1)
    ys = (y0, y1)

    def gather_copy(tile, r, slot):
        p = jnp.maximum(src_ref[(tile + 1) * MOE_TM + r], 0)
        tok = jnp.where(p >= n_tok, p - n_tok, p)
        return pltpu.make_async_copy(h_hbm.at[pl.ds(tok, 1)], xs[slot].at[pl.ds(r, 1)], sems.at[slot])

    def scatter_copy(tile, r, slot):
        p = src_ref[(tile + 1) * MOE_TM + r]
        dst = jnp.where(p >= 0, p, TOP_K * n_tok + r)
        return pltpu.make_async_copy(ys[slot].at[pl.ds(r, 1)], y_hbm.at[pl.ds(dst, 1)], sems.at[2 + slot])

    @pl.when(i == 0)
    def _():
        y1[...] = jnp.zeros_like(y1)
        for r in range(MOE_TM):
            gather_copy(0, r, 0).start()

    def step(cur):
        nxt = 1 - cur
        for r in range(MOE_TM):
            gather_copy(i, r, cur).wait()
        for r in range(MOE_TM):
            gather_copy(i + 1, r, nxt).start()
        for r in range(MOE_TM):
            scatter_copy(i - 1, r, nxt).start()
        x = xs[cur][...].astype(BF16)
        a = jnp.dot(x, w1_ref[0].astype(BF16), preferred_element_type=F32)
        b = jnp.dot(x, w3_ref[0].astype(BF16), preferred_element_type=F32)
        hid = (_silu(a) * b).astype(BF16)
        ys[cur][...] = jnp.dot(hid, w2_ref[0].astype(BF16), preferred_element_type=F32)
        for r in range(MOE_TM):
            scatter_copy(i - 1, r, nxt).wait()

        @pl.when(i == n_tiles)
        def _():
            for r in range(MOE_TM):
                gather_copy(i + 1, r, nxt).wait()

    live = i <= n_tiles
    pl.when(live & (i % 2 == 0))(lambda: step(0))
    pl.when(live & (i % 2 == 1))(lambda: step(1))


def _gmm(src, tile_expert, n_tiles, h2, w1, w3, w2, n_tok):
    d = h2.shape[1]
    steps = tile_expert.shape[0]
    wspec = lambda shape: pl.BlockSpec((1,) + shape, lambda i, src, te, nt: (te[i], 0, 0))
    return pl.pallas_call(
        functools.partial(_gmm_kernel, n_tok=n_tok),
        out_shape=jax.ShapeDtypeStruct((TOP_K * n_tok + MOE_TM, d), F32),
        grid_spec=pltpu.PrefetchScalarGridSpec(
            num_scalar_prefetch=3,
            grid=(steps,),
            in_specs=[pl.BlockSpec(memory_space=pl.ANY),
                      wspec((d, D_EXPERT)), wspec((d, D_EXPERT)), wspec((D_EXPERT, d))],
            out_specs=pl.BlockSpec(memory_space=pl.ANY),
            scratch_shapes=[pltpu.VMEM((MOE_TM, d), F32)] * 4 + [pltpu.SemaphoreType.DMA((4,))],
        ),
        compiler_params=_cparams(("arbitrary",)),
        name="gmm",
    )(src, tile_expert, n_tiles, h2, w1, w3, w2)


def _route_tables(ids, n_tok):
    n_pairs = n_tok * TOP_K
    max_tiles = n_pairs // MOE_TM + N_EXPERTS
    e_flat = ids[:, :TOP_K].T.reshape(n_pairs)
    onehot = (e_flat[:, None] == jnp.arange(N_EXPERTS, dtype=I32)[None, :]).astype(I32)
    csum = jnp.cumsum(onehot, axis=0)
    counts = csum[-1]
    rank = jnp.sum(csum * onehot, axis=1) - 1
    tiles_per = (counts + MOE_TM - 1) // MOE_TM
    tile_end = jnp.cumsum(tiles_per)
    row_start = (tile_end - tiles_per + 1) * MOE_TM
    pos = row_start[e_flat] + rank
    src = jnp.full(((max_tiles + 3) * MOE_TM,), -1, I32).at[pos].set(jnp.arange(n_pairs, dtype=I32))
    steps = jnp.arange(max_tiles + 1, dtype=I32)
    owner = jnp.sum((tile_end[None, :] <= steps[:, None]).astype(I32), axis=1)
    tile_expert = jnp.minimum(owner, N_EXPERTS - 1)
    return src, tile_expert, tile_end[-1:].astype(I32)


def _combine_kernel(x1_ref, y0_ref, y1_ref, wts_ref, mod_ref, g_ref, o_ref):
    w = wts_ref[...]
    moe = w[:, 0:1] * y0_ref[...] + w[:, 1:2] * y1_ref[...]
    x2 = x1_ref[...] + mod_ref[0, 5:6, :] * moe
    o_ref[...] = _rms(x2) * g_ref[...]


def _combine(x1, y_pairs, wts, mods, mod_row0, g, row0, batch, t, n_tok):
    d = x1.shape[1]
    nt = t // CMB_TM
    rb0 = row0 // CMB_TM
    slot1 = n_tok // CMB_TM
    return pl.pallas_call(
        _combine_kernel,
        out_shape=jax.ShapeDtypeStruct((batch * t, d), F32),
        grid=(batch * nt,),
        in_specs=[pl.BlockSpec((CMB_TM, d), lambda i: (rb0 + i, 0)),
                  pl.BlockSpec((CMB_TM, d), lambda i: (rb0 + i, 0)),
                  pl.BlockSpec((CMB_TM, d), lambda i: (slot1 + rb0 + i, 0)),
                  pl.BlockSpec((CMB_TM, LANES), lambda i: (rb0 + i, 0)),
                  pl.BlockSpec((1, N_ADA, d), lambda i: (mod_row0 + i // nt, 0, 0)),
                  pl.BlockSpec((1, d), lambda i: (0, 0))],
        out_specs=pl.BlockSpec((CMB_TM, d), lambda i: (i, 0)),
        compiler_params=_cparams(("arbitrary",)),
        name="combine",
    )(x1, y_pairs, y_pairs, wts, mods, g.reshape(1, d))


def _rope_tables(n_lat):
    half = DA_DK // 2
    nf = half // 2
    t = jnp.arange(n_lat)
    row = (t // GRID_W).astype(F32)
    col = (t % GRID_W).astype(F32)
    inv = ROPE_THETA ** (-jnp.arange(nf, dtype=F32) / nf)
    lane = np.arange(HEAD_W)
    freq = inv[lane % nf]
    pos = jnp.where(((lane // half) % 2 == 0)[None, :], row[:, None], col[:, None])
    ang = pos * freq[None, :]
    sign = np.where((lane % half) < nf, -1.0, 1.0).astype(np.float32)
    return jnp.cos(ang), jnp.sin(ang) * sign[None, :]


def kernel(x_prompt, x_sample, cache_k, cache_v, state_hgrn_fwd, state_hgrn_bwd, c, c_ctx, w_ada, b_ada,
           norm1_g, norm2_g, norm_final_g, w_in, hg_lb_fwd, hg_lb_bwd, hg_norm_g, da_lambda_q1, da_lambda_k1,
           da_lambda_q2, da_lambda_k2, da_norm_g, w_out, router_g_w, router_g_b, router_e_w, router_e_b,
           exp_w1, exp_w3, exp_w2):
    l = 0
    batch, seq, d = x_prompt.shape
    dec_batch, dec_seq, _ = x_sample.shape
    n_ctx = batch * seq
    n_lat = dec_batch * dec_seq
    n_tok = n_ctx + n_lat
    lam_init = 0.8 - 0.6 * math.exp(-0.3 * l)

    cond = jnp.zeros((8, d), F32).at[0].set(c_ctx).at[1:1 + dec_batch].set(c)
    mods = _ada(cond, w_ada[l], b_ada[l]).reshape(8, N_ADA, d)

    w_in_bf16 = w_in[l].astype(BF16)
    proj_c = _in_proj(x_prompt.reshape(1, n_ctx, d), mods, 0, norm1_g[l], w_in_bf16, 1024, "in_proj_ctx")
    proj_l = _in_proj(x_sample, mods, 1, norm1_g[l], w_in_bf16, 1024, "in_proj_lat")
    lam_params = (da_lambda_q1[l], da_lambda_k1[l], da_lambda_q2[l], da_lambda_k2[l])

    da_ctx, new_k, new_v = _attn(proj_c, batch, seq, lam_params, da_norm_g[l], lam_init, emit_cache=True)
    hg_ctx, new_sf, new_sb = _hgrn(proj_c, batch, seq, hg_lb_fwd, hg_lb_bwd, hg_norm_g[l], emit_state=True)

    cos, sin = _rope_tables(dec_seq)
    (da_lat,) = _attn(proj_l, dec_batch, dec_seq, lam_params, da_norm_g[l], lam_init, cos=cos, sin=sin,
                      cache_k=cache_k[:, l:l + 1], cache_v=cache_v[:, l:l + 1])
    (hg_lat,) = _hgrn(proj_l, dec_batch, dec_seq, hg_lb_fwd, hg_lb_bwd, hg_norm_g[l],
                      s0f=state_hgrn_fwd[:, l:l + 1], s0b=state_hgrn_bwd[:, l:l + 1])

    rw = jnp.zeros((d, LANES), F32).at[:, :N_GROUPS].set(router_g_w[l]).at[:, N_GROUPS:N_GROUPS + N_EXPERTS].set(
        router_e_w[l])
    rw_hi = rw.astype(BF16)
    rw = jnp.stack([rw_hi, (rw - rw_hi.astype(F32)).astype(BF16)])
    rb = jnp.zeros((1, LANES), F32).at[0, :N_GROUPS].set(router_g_b[l]).at[0, N_GROUPS:N_GROUPS + N_EXPERTS].set(
        router_e_b[l])
    x1, h2, ids, wts = _out_proj(hg_ctx, da_ctx, x_prompt.reshape(n_ctx, d), hg_lat, da_lat,
                                 x_sample.reshape(n_lat, d), w_out[l].astype(BF16), mods, norm2_g[l], rw, rb, dec_seq)

    src, tile_expert, n_tiles = _route_tables(ids, n_tok)
    y_pairs = _gmm(src, tile_expert, n_tiles, h2, exp_w1[l], exp_w3[l], exp_w2[l], n_tok)

    y_ctx = _combine(x1, y_pairs, wts, mods, 0, norm_final_g, 0, 1, n_ctx, n_tok)
    y_lat = _combine(x1, y_pairs, wts, mods, 1, norm_final_g, n_ctx, dec_batch, dec_seq, n_tok)
    return (y_ctx.reshape(batch, seq, d), y_lat.reshape(dec_batch, dec_seq, d), new_k, new_v, new_sf, new_sb)
```

```python
import functools
import math

import numpy as np
import jax
import jax.numpy as jnp
from jax import lax
from jax.experimental import pallas as pl
from jax.experimental.pallas import tpu as pltpu

F32 = jnp.float32
BF16 = jnp.bfloat16
I32 = jnp.int32

GRID_W = 64
HG_WIDTH = 1024
HG_DK = 128
HG_HEADS = 8
DA_HEADS = 8
DA_DK = 64
HEAD_W = 128
ROPE_THETA = 10000.0
N_GROUPS = 4
EXP_PER_GROUP = 8
N_EXPERTS = 32
TOP_K = 2
D_EXPERT = 512
N_ADA = 6
RMS_EPS = 1e-6
CB_Q_HG, CB_F_FW, CB_F_BW, CB_I_HG, CB_G_HG, CB_Q_DA, CB_K_DA, CB_V_DA = (8 * i for i in range(8))

LANES = 128
SUBLANES = 8
VMEM_LIMIT = 56 * 1024 * 1024

ADA_TN = 1536
IN_TN = 512
OUT_TM = 256
HG_CHUNK = 64
HG_GROUP = 4
ATT_TQ = 256
ATT_TK = 512
MOE_TM = 256
CMB_TM = 512
NEG_INF = float("-inf")


def _cparams(sem):
    return pltpu.CompilerParams(dimension_semantics=sem, vmem_limit_bytes=VMEM_LIMIT)


def _silu(x):
    return x * jax.nn.sigmoid(x)


def _rms(x):
    return x * lax.rsqrt(jnp.mean(x * x, axis=-1, keepdims=True) + RMS_EPS)


def _ada_kernel(cond_ref, w_ref, b_ref, o_ref):
    s = _silu(cond_ref[...]).astype(BF16)
    o_ref[...] = jnp.dot(s, w_ref[...].astype(BF16), preferred_element_type=F32) + b_ref[...]


def _ada(cond, w, b):
    rows, d = cond.shape
    n = w.shape[1]
    return pl.pallas_call(
        _ada_kernel,
        out_shape=jax.ShapeDtypeStruct((rows, n), F32),
        grid=(n // ADA_TN,),
        in_specs=[pl.BlockSpec((rows, d), lambda j: (0, 0)),
                  pl.BlockSpec((d, ADA_TN), lambda j: (0, j)),
                  pl.BlockSpec((1, ADA_TN), lambda j: (0, j))],
        out_specs=pl.BlockSpec((rows, ADA_TN), lambda j: (0, j)),
        compiler_params=_cparams(("arbitrary",)),
        name="ada",
    )(cond, w, b.reshape(1, n))


def _in_kernel(x_ref, mod_ref, g_ref, w_ref, o_ref, h_ref):
    @pl.when(pl.program_id(1) == 0)
    def _():
        y = _rms(x_ref[...]) * g_ref[...]
        h_ref[...] = (y * (1.0 + mod_ref[0, 1:2, :]) + mod_ref[0, 0:1, :]).astype(BF16)

    o_ref[...] = jnp.dot(h_ref[...], w_ref[...], preferred_element_type=F32)


def _in_proj(x, mods, mod_row0, g, w_bf16, tm, name):
    batch, t, d = x.shape
    cols = w_bf16.shape[1]
    nt = t // tm
    return pl.pallas_call(
        _in_kernel,
        out_shape=jax.ShapeDtypeStruct((batch * t, cols), F32),
        grid=(batch * nt, cols // IN_TN),
        in_specs=[pl.BlockSpec((tm, d), lambda i, j: (i, 0)),
                  pl.BlockSpec((1, N_ADA, d), lambda i, j: (mod_row0 + i // nt, 0, 0)),
                  pl.BlockSpec((1, d), lambda i, j: (0, 0)),
                  pl.BlockSpec((d, IN_TN), lambda i, j: (0, j))],
        out_specs=pl.BlockSpec((tm, IN_TN), lambda i, j: (i, j)),
        scratch_shapes=[pltpu.VMEM((tm, d), BF16)],
        compiler_params=_cparams(("arbitrary", "arbitrary")),
        name=name,
    )(x.reshape(batch * t, d), mods, g.reshape(1, d), w_bf16)


def _rope(x, cos, sin_signed):
    lane = lax.broadcasted_iota(I32, x.shape, 1)
    first = (lane % 32) < 16
    partner = jnp.where(first, pltpu.roll(x, LANES - 16, 1), pltpu.roll(x, 16, 1))
    return x * cos + partner * sin_signed


def _split_maps(k):
    lane = lax.broadcasted_iota(I32, k.shape, 1)
    m1 = lane < DA_DK
    return jnp.where(m1, k, 0.0).astype(BF16), jnp.where(m1, 0.0, k).astype(BF16)


def _lower_bound(lb_ref):
    p = lb_ref[...]
    e = jnp.exp(p - jnp.max(p, axis=0, keepdims=True))
    return e[0:1, :] / jnp.sum(e, axis=0, keepdims=True)


def _hgrn_pair_masks(mask_ref, c):
    ti = lax.broadcasted_iota(I32, (c, c), 0)
    si = lax.broadcasted_iota(I32, (c, c), 1)
    for d in range(2):
        mask_ref[d, 0] = (ti == si).astype(F32)
        for j in range(c.bit_length() - 1):
            same = (ti >> (j + 1)) == (si >> (j + 1))
            t_bit = ((ti >> j) & 1) == 1
            s_bit = ((si >> j) & 1) == 1
            pair = (s_bit & jnp.logical_not(t_bit)) if d else (t_bit & jnp.logical_not(s_bit))
            mask_ref[d, 1 + j] = (same & pair).astype(F32)


def _hgrn_chunk(q, f_logit, v, lb, mask_ref, rev):
    c = q.shape[0]
    n_piece = c // SUBLANES
    levels = c.bit_length() - 1
    low = SUBLANES.bit_length() - 1
    d = 1 if rev else 0
    nt = (((1,), (1,)), ((), ()))
    pieces = lambda x: [x[SUBLANES * i:SUBLANES * (i + 1)] for i in range(n_piece)]
    whole = lambda xs: jnp.concatenate(xs, axis=0)

    fg = lb + (1.0 - lb) * jax.nn.sigmoid(f_logit)
    k = 1.0 - fg
    q_bf = q.astype(BF16)
    k_bf = k.astype(BF16)
    qs, ks, fgs = pieces(q), pieces(k), pieces(fg)
    tot = pieces(jnp.log(fg))
    pre = list(tot)
    suf = [jnp.zeros((SUBLANES, LANES), F32)] * n_piece
    sub = lax.broadcasted_iota(I32, (SUBLANES, LANES), 0)

    def pair_scores(qt, kt, idx):
        return lax.dot_general(qt, kt, nt, preferred_element_type=F32) * mask_ref[d, idx]

    scores = pair_scores(q_bf, k_bf, 0)
    yield
    for j in range(levels):
        if j < low:
            sh = 1 << j
            bit = ((sub >> j) & 1) == 1
            late = jnp.logical_not(bit) if rev else bit
            if j == 0:
                qt = whole([qs[i] * jnp.where(late, fgs[i], 1.0) for i in range(n_piece)]).astype(BF16)
                kt = k_bf
            else:
                es = [jnp.exp(jnp.where(late, pre[i], suf[i])) for i in range(n_piece)]
                qt = whole([qs[i] * es[i] for i in range(n_piece)]).astype(BF16)
                kt = whole([ks[i] * es[i] for i in range(n_piece)]).astype(BF16)
            for i in range(n_piece):
                up = pltpu.roll(tot[i], sh, 0)
                dn = pltpu.roll(tot[i], SUBLANES - sh, 0)
                sib = jnp.where(late, dn, up) if rev else jnp.where(late, up, dn)
                pre[i] = pre[i] + jnp.where(late, sib, 0.0)
                suf[i] = suf[i] + jnp.where(late, 0.0, sib)
                tot[i] = tot[i] + sib
        else:
            half = 1 << (j - low)
            upper = [(i // half) % 2 == 1 for i in range(n_piece)]
            late = [(not u) if rev else u for u in upper]
            es = [jnp.exp(pre[i] if late[i] else suf[i]) for i in range(n_piece)]
            qt = whole([qs[i] * es[i] for i in range(n_piece)]).astype(BF16)
            kt = whole([ks[i] * es[i] for i in range(n_piece)]).astype(BF16)
            sib = [tot[i - half] if upper[i] else tot[i + half] for i in range(n_piece)]
            pre = [pre[i] + sib[i] if late[i] else pre[i] for i in range(n_piece)]
            suf = [suf[i] if late[i] else suf[i] + sib[i] for i in range(n_piece)]
            tot = [tot[i] + sib[i] for i in range(n_piece)]
        scores = scores + pair_scores(qt, kt, 1 + j)
        yield

    q_dec = whole([qs[i] * jnp.exp(pre[i]) for i in range(n_piece)]).astype(BF16)
    k_dec = whole([ks[i] * jnp.exp(suf[i]) for i in range(n_piece)]).astype(BF16)
    return dict(q_dec=q_dec, k_dec=k_dec, v_t=v.T.astype(BF16), v=v.astype(BF16),
                scores=scores.astype(BF16), decay=jnp.exp(tot[0][0:1, :]))


def _hgrn_state_step(p, st_ref):
    st = st_ref[...]
    o = lax.dot_general(p["q_dec"], st.astype(BF16), (((1,), (1,)), ((), ())), preferred_element_type=F32)
    st_ref[...] = p["decay"] * st + jnp.dot(p["v_t"], p["k_dec"], preferred_element_type=F32)
    return o


def _lockstep(gens):
    results = [None] * len(gens)
    live = list(range(len(gens)))
    while live:
        for idx in list(live):
            try:
                next(gens[idx])
            except StopIteration as stop:
                results[idx] = stop.value
                live.remove(idx)
    return results


def _hgrn_kernel(*refs, t, has_init, emit_state):
    q_ref, ff_ref, fb_ref, i_ref, g_ref, lbf_ref, lbb_ref, ng_ref = refs[:8]
    pos = 8
    if has_init:
        s0f_ref, s0b_ref = refs[pos:pos + 2]
        pos += 2
    o_ref = refs[pos]
    pos += 1
    if emit_state:
        sf_ref, sb_ref = refs[pos:pos + 2]
        pos += 2
    of_scr, ob_scr, stf, stb, mask_scr = refs[pos:pos + 5]

    c = HG_CHUNK
    n = t // c
    _hgrn_pair_masks(mask_scr, c)
    if has_init:
        stf[...] = s0f_ref[0, 0, 0].T
        stb[...] = s0b_ref[0, 0, 0].T
    else:
        stf[...] = jnp.zeros_like(stf)
        stb[...] = jnp.zeros_like(stb)
    lbf = _lower_bound(lbf_ref)
    lbb = _lower_bound(lbb_ref)

    def body(i, carry):
        work = []
        for u in range(HG_GROUP):
            ci = i * HG_GROUP + u
            work.append((pl.ds(pl.multiple_of(ci * c, c), c), ff_ref, lbf, stf, of_scr, False))
            work.append((pl.ds(pl.multiple_of((n - 1 - ci) * c, c), c), fb_ref, lbb, stb, ob_scr, True))
        parts = _lockstep([_hgrn_chunk(_silu(q_ref[sl, :]), f_ref[sl, :], i_ref[sl, :], lb, mask_scr, rev)
                           for sl, f_ref, lb, _, _, rev in work])
        outs = [_hgrn_state_step(p, w[3]) for p, w in zip(parts, work)]
        for p, w, o in zip(parts, work, outs):
            w[4][w[0], :] = o + jnp.dot(p["scores"], p["v"], preferred_element_type=F32)
        return carry

    lax.fori_loop(0, n // HG_GROUP, body, 0)

    fin = min(t, 256)

    def finish(i, carry):
        sl = pl.ds(pl.multiple_of(i * fin, fin), fin)
        o = _rms(of_scr[sl, :] + ob_scr[sl, :]) * ng_ref[...]
        o_ref[sl, :] = (o * _silu(g_ref[sl, :])).astype(BF16)
        return carry

    lax.fori_loop(0, t // fin, finish, 0)
    if emit_state:
        sf_ref[0, 0, 0] = stf[...].T
        sb_ref[0, 0, 0] = stb[...].T


def _hgrn(proj, batch, t, lb_fwd, lb_bwd, norm_g, s0f=None, s0b=None, emit_state=False):
    has_init = s0f is not None

    def col(cb):
        return pl.BlockSpec((t, HEAD_W), lambda b, h: (b, cb + h))

    lb_spec = pl.BlockSpec((lb_fwd.shape[0], HEAD_W), lambda b, h: (0, h))
    in_specs = [col(CB_Q_HG), col(CB_F_FW), col(CB_F_BW), col(CB_I_HG), col(CB_G_HG),
                lb_spec, lb_spec, pl.BlockSpec((1, HEAD_W), lambda b, h: (0, 0))]
    args = [proj] * 5 + [lb_fwd, lb_bwd, norm_g.reshape(1, HEAD_W)]
    st_spec = pl.BlockSpec((1, 1, 1, HG_DK, HEAD_W), lambda b, h: (b, 0, h, 0, 0))
    if has_init:
        in_specs += [st_spec, st_spec]
        args += [s0f, s0b]
    out_shape = [jax.ShapeDtypeStruct((batch * t, HG_WIDTH), BF16)]
    out_specs = [pl.BlockSpec((t, HEAD_W), lambda b, h: (b, h))]
    if emit_state:
        st = jax.ShapeDtypeStruct((batch, 1, HG_HEADS, HG_DK, HEAD_W), F32)
        out_shape += [st, st]
        out_specs += [st_spec, st_spec]
    return pl.pallas_call(
        functools.partial(_hgrn_kernel, t=t, has_init=has_init, emit_state=emit_state),
        out_shape=out_shape,
        grid=(batch, HG_HEADS),
        in_specs=in_specs,
        out_specs=out_specs,
        scratch_shapes=[pltpu.VMEM((t, HEAD_W), F32), pltpu.VMEM((t, HEAD_W), F32),
                        pltpu.VMEM((HEAD_W, HG_DK), F32), pltpu.VMEM((HEAD_W, HG_DK), F32),
                        pltpu.VMEM((2, HG_CHUNK.bit_length(), HG_CHUNK, HG_CHUNK), F32)],
        compiler_params=_cparams(("arbitrary", "arbitrary")),
        name="hgrn_lat" if has_init else "hgrn_ctx",
    )(*args)


def _attn_kernel(*refs, t, n_cache, use_rope, emit_cache, lam_init):
    it = iter(refs)
    q_ref, k_ref, v_ref = next(it), next(it), next(it)
    if use_rope:
        cosq_ref, sinq_ref, cos_ref, sin_ref = next(it), next(it), next(it), next(it)
    if n_cache:
        ck_ref, cv_ref = next(it), next(it)
    lq1_ref, lk1_ref, lq2_ref, lk2_ref, ng_ref, o_ref = (next(it) for _ in range(6))
    if emit_cache:
        nk_ref, nv_ref = next(it), next(it)
    k1_scr, k2_scr, v_scr, s_scr, p_scr = (next(it) for _ in range(5))
    k_scrs = (k1_scr, k2_scr)
    tq = q_ref.shape[0]

    @pl.when(pl.program_id(2) == 0)
    def _():
        rows = min(t, ATT_TK)

        def stage(i, carry):
            sl = pl.ds(pl.multiple_of(i * rows, rows), rows)
            k = k_ref[sl, :]
            v = v_ref[sl, :]
            if emit_cache:
                nk_ref[0, 0, 0, sl, :] = k
                nv_ref[0, 0, 0, sl, :] = v
            if use_rope:
                k = _rope(k, cos_ref[sl, :], sin_ref[sl, :])
            k1_scr[sl, :], k2_scr[sl, :] = _split_maps(k)
            v_scr[sl, :] = v.astype(BF16)
            return carry

        lax.fori_loop(0, t // rows, stage, 0)
        if n_cache:
            k1_scr[t:t + n_cache, :], k2_scr[t:t + n_cache, :] = _split_maps(ck_ref[0, 0, 0])
            v_scr[t:t + n_cache, :] = cv_ref[0, 0, 0].astype(BF16)

    q = q_ref[...]
    if use_rope:
        q = _rope(q, cosq_ref[...], sinq_ref[...])
    q = (q * (DA_DK ** -0.5)).astype(BF16)
    nt = (((1,), (1,)), ((), ()))
    n_keys = t + n_cache
    tiles = [(st, min(ATT_TK, n_keys - st)) for st in range(0, n_keys, ATT_TK)]

    def scores_tile(mp, st, sz, mx):
        s = lax.dot_general(q, k_scrs[mp][st:st + sz, :], nt, preferred_element_type=F32)
        s_scr[mp, :, st:st + sz] = s
        for j in range(sz // LANES):
            mx = jnp.maximum(mx, s[:, j * LANES:(j + 1) * LANES])
        return mx

    def exp_tile(mp, st, sz, m, part):
        for lo in range(st, st + sz, LANES):
            e = jnp.exp(s_scr[mp, :, lo:lo + LANES] - m)
            part = part + e
            p_scr[mp, :, lo:lo + LANES] = e.astype(BF16)
        return part

    def value_tile(mp, st, sz, acc):
        return acc + jnp.dot(p_scr[mp, :, st:st + sz], v_scr[st:st + sz, :], preferred_element_type=F32)

    def row_stat(x, op):
        return jnp.broadcast_to(op(x, axis=-1, keepdims=True), (tq, LANES))

    neg = jnp.full((tq, LANES), NEG_INF, F32)
    zero = jnp.zeros((tq, LANES), F32)
    mx = neg
    for st, sz in tiles:
        mx = scores_tile(0, st, sz, mx)
    m0 = row_stat(mx, jnp.max)
    mx, part0 = neg, zero
    for st, sz in tiles:
        mx = scores_tile(1, st, sz, mx)
        part0 = exp_tile(0, st, sz, m0, part0)
    m1 = row_stat(mx, jnp.max)
    acc0, part1 = zero, zero
    for st, sz in tiles:
        acc0 = value_tile(0, st, sz, acc0)
        part1 = exp_tile(1, st, sz, m1, part1)
    acc1 = jnp.dot(p_scr[1], v_scr[...], preferred_element_type=F32)

    lam = (jnp.exp(jnp.sum(lq1_ref[...] * lk1_ref[...], axis=-1, keepdims=True))
           - jnp.exp(jnp.sum(lq2_ref[...] * lk2_ref[...], axis=-1, keepdims=True)) + lam_init)
    o = acc0 / row_stat(part0, jnp.sum) - lam * (acc1 / row_stat(part1, jnp.sum))
    o_ref[...] = (_rms(o) * ng_ref[...] * (1.0 - lam_init)).astype(BF16)


def _attn(proj, batch, t, lam_params, norm_g, lam_init, cos=None, sin=None, cache_k=None, cache_v=None,
          emit_cache=False):
    use_rope = cos is not None
    n_cache = 0 if cache_k is None else cache_k.shape[3]
    nq = t // ATT_TQ
    n_keys = t + n_cache

    def col(cb, rows, row_map):
        return pl.BlockSpec((rows, HEAD_W), lambda b, h, i: (row_map(b, i), cb + h))

    in_specs = [col(CB_Q_DA, ATT_TQ, lambda b, i: b * nq + i),
                col(CB_K_DA, t, lambda b, i: b), col(CB_V_DA, t, lambda b, i: b)]
    args = [proj, proj, proj]
    if use_rope:
        in_specs += [pl.BlockSpec((ATT_TQ, HEAD_W), lambda b, h, i: (i, 0))] * 2
        in_specs += [pl.BlockSpec((t, HEAD_W), lambda b, h, i: (0, 0))] * 2
        args += [cos, sin, cos, sin]
    cache_spec = lambda n: pl.BlockSpec((1, 1, 1, n, HEAD_W), lambda b, h, i: (b, 0, h, 0, 0))
    if n_cache:
        in_specs += [cache_spec(n_cache)] * 2
        args += [cache_k, cache_v]
    small = pl.BlockSpec((1, DA_DK), lambda b, h, i: (0, 0))
    in_specs += [small] * 4 + [pl.BlockSpec((1, HEAD_W), lambda b, h, i: (0, 0))]
    args += [p.reshape(1, DA_DK) for p in lam_params] + [norm_g.reshape(1, HEAD_W)]
    out_shape = [jax.ShapeDtypeStruct((batch * t, DA_HEADS * HEAD_W), BF16)]
    out_specs = [pl.BlockSpec((ATT_TQ, HEAD_W), lambda b, h, i: (b * nq + i, h))]
    if emit_cache:
        out_shape += [jax.ShapeDtypeStruct((batch, 1, DA_HEADS, t, HEAD_W), F32)] * 2
        out_specs += [cache_spec(t)] * 2
    scratch = [pltpu.VMEM((n_keys, HEAD_W), BF16)] * 3
    scratch += [pltpu.VMEM((2, ATT_TQ, n_keys), F32), pltpu.VMEM((2, ATT_TQ, n_keys), BF16)]
    return pl.pallas_call(
        functools.partial(_attn_kernel, t=t, n_cache=n_cache, use_rope=use_rope, emit_cache=emit_cache,
                          lam_init=lam_init),
        out_shape=out_shape,
        grid=(batch, DA_HEADS, nq),
        in_specs=in_specs,
        out_specs=out_specs,
        scratch_shapes=scratch,
        compiler_params=_cparams(("arbitrary",) * 3),
        name="attn_lat" if use_rope else "attn_ctx",
    )(*args)


def _out_kernel(mhg_c, mda_c, x_c, mhg_l, mda_l, x_l, w_ref, mod_ref, g_ref, rw_ref, rb_ref,
                x1_ref, h2_ref, ids_ref, wts_ref, *, ctx_tiles):
    def body(mhg_ref, mda_ref, x_ref):
        mix = jnp.dot(mhg_ref[...], w_ref[0:HG_WIDTH, :], preferred_element_type=F32)
        mix = mix + jnp.dot(mda_ref[...], w_ref[HG_WIDTH:, :], preferred_element_type=F32)
        x1 = x_ref[...] + mod_ref[0, 2:3, :] * mix
        x1_ref[...] = x1
        h2 = (_rms(x1) * g_ref[...]) * (1.0 + mod_ref[0, 4:5, :]) + mod_ref[0, 3:4, :]
        h2_ref[...] = h2

        h_hi = h2.astype(BF16)
        h_lo = (h2 - h_hi.astype(F32)).astype(BF16)
        logit = (jnp.dot(h_hi, rw_ref[0], preferred_element_type=F32)
                 + jnp.dot(h_lo, rw_ref[0], preferred_element_type=F32)
                 + jnp.dot(h_hi, rw_ref[1], preferred_element_type=F32)) + rb_ref[...]
        lane = lax.broadcasted_iota(I32, logit.shape, 1)

        def first_max(x):
            m = jnp.max(x, axis=-1, keepdims=True)
            return m, jnp.min(jnp.where(x == m, lane, LANES), axis=-1, keepdims=True)

        gmask = lane < N_GROUPS
        gmax, gsel = first_max(jnp.where(gmask, logit, NEG_INF))
        p_grp = 1.0 / jnp.sum(jnp.where(gmask, jnp.exp(logit - gmax), 0.0), axis=-1, keepdims=True)
        lo = N_GROUPS + EXP_PER_GROUP * gsel
        le = jnp.where((lane >= lo) & (lane < lo + EXP_PER_GROUP), logit, NEG_INF)
        v1, i1 = first_max(le)
        v2, i2 = first_max(jnp.where(lane == i1, NEG_INF, le))
        e = jnp.exp(v2 - v1)
        w1 = p_grp / (1.0 + e)
        w2 = p_grp * e / (1.0 + e)
        ids_ref[...] = jnp.where(lane == 0, i1 - N_GROUPS, jnp.where(lane == 1, i2 - N_GROUPS, 0))
        wts_ref[...] = jnp.where(lane == 0, w1, jnp.where(lane == 1, w2, 0.0))

    i = pl.program_id(0)
    pl.when(i < ctx_tiles)(lambda: body(mhg_c, mda_c, x_c))
    pl.when(i >= ctx_tiles)(lambda: body(mhg_l, mda_l, x_l))


def _out_proj(mhg_c, mda_c, x_c, mhg_l, mda_l, x_l, w_bf16, mods, g, rw, rb, lat_t):
    n_ctx, d = x_c.shape
    n_lat = x_l.shape[0]
    n = n_ctx + n_lat
    ctx_tiles = n_ctx // OUT_TM
    lat_tiles = lat_t // OUT_TM
    row = lambda i: (i, 0)
    const = lambda i: (0, 0)
    crow = lambda i: (jnp.minimum(i, ctx_tiles - 1), 0)
    lrow = lambda i: (jnp.maximum(i - ctx_tiles, 0), 0)
    seg = lambda i: (jnp.where(i < ctx_tiles, 0, 1 + (i - ctx_tiles) // lat_tiles), 0, 0)
    half = lambda m: pl.BlockSpec((OUT_TM, HG_WIDTH), m)
    full = lambda m: pl.BlockSpec((OUT_TM, d), m)
    return pl.pallas_call(
        functools.partial(_out_kernel, ctx_tiles=ctx_tiles),
        out_shape=[jax.ShapeDtypeStruct((n, d), F32), jax.ShapeDtypeStruct((n, d), F32),
                   jax.ShapeDtypeStruct((n, LANES), I32), jax.ShapeDtypeStruct((n, LANES), F32)],
        grid=(n // OUT_TM,),
        in_specs=[half(crow), half(crow), full(crow), half(lrow), half(lrow), full(lrow),
                  pl.BlockSpec((d, d), const), pl.BlockSpec((1, N_ADA, d), seg), pl.BlockSpec((1, d), const),
                  pl.BlockSpec((2, d, LANES), lambda i: (0, 0, 0)), pl.BlockSpec((1, LANES), const)],
        out_specs=[full(row), full(row), pl.BlockSpec((OUT_TM, LANES), row), pl.BlockSpec((OUT_TM, LANES), row)],
        compiler_params=_cparams(("arbitrary",)),
        name="out_proj",
    )(mhg_c, mda_c, x_c, mhg_l, mda_l, x_l, w_bf16, mods, g.reshape(1, d), rw, rb)


def _gmm_kernel(src_ref, texp_ref, ntile_ref, h_hbm, w1_ref, w3_ref, w2_ref, y_hbm,
                x0, x1, y0, y1, sems, *, n_tok):
    i = pl.program_id(0)
    n_tiles = ntile_ref[0]
    xs = (x0, x1)
    ys = (y0, y1)

    def gather_copy(tile, r, slot):
        p = jnp.maximum(src_ref[(tile + 1) * MOE_TM + r], 0)
        tok = jnp.where(p >= n_tok, p - n_tok, p)
        return pltpu.make_async_copy(h_hbm.at[pl.ds(tok, 1)], xs[slot].at[pl.ds(r, 1)], sems.at[slot])

    def scatter_copy(tile, r, slot):
        p = src_ref[(tile + 1) * MOE_TM + r]
        dst = jnp.where(p >= 0, p, TOP_K * n_tok + r)
        return pltpu.make_async_copy(ys[slot].at[pl.ds(r, 1)], y_hbm.at[pl.ds(dst, 1)], sems.at[2 + slot])

    @pl.when(i == 0)
    def _():
        y1[...] = jnp.zeros_like(y1)
        for r in range(MOE_TM):
            gather_copy(0, r, 0).start()

    def wait_gather(slot):
        pltpu.make_async_copy(h_hbm.at[pl.ds(0, MOE_TM)], xs[slot], sems.at[slot]).wait()

    def wait_scatter(slot):
        pltpu.make_async_copy(ys[slot], y_hbm.at[pl.ds(0, MOE_TM)], sems.at[2 + slot]).wait()

    def step(cur):
        nxt = 1 - cur
        wait_gather(cur)
        for r in range(MOE_TM):
            gather_copy(i + 1, r, nxt).start()
        for r in range(MOE_TM):
            scatter_copy(i - 1, r, nxt).start()
        x = xs[cur][...].astype(BF16)
        a = jnp.dot(x, w1_ref[0].astype(BF16), preferred_element_type=F32)
        b = jnp.dot(x, w3_ref[0].astype(BF16), preferred_element_type=F32)
        hid = (_silu(a) * b).astype(BF16)
        ys[cur][...] = jnp.dot(hid, w2_ref[0].astype(BF16), preferred_element_type=F32)
        wait_scatter(nxt)

        @pl.when(i == n_tiles)
        def _():
            wait_gather(nxt)

    live = i <= n_tiles
    pl.when(live & (i % 2 == 0))(lambda: step(0))
    pl.when(live & (i % 2 == 1))(lambda: step(1))


def _gmm(src, tile_expert, n_tiles, h2, w1, w3, w2, n_tok):
    d = h2.shape[1]
    steps = tile_expert.shape[0]
    wspec = lambda shape: pl.BlockSpec((1,) + shape, lambda i, src, te, nt: (te[i], 0, 0))
    return pl.pallas_call(
        functools.partial(_gmm_kernel, n_tok=n_tok),
        out_shape=jax.ShapeDtypeStruct((TOP_K * n_tok + MOE_TM, d), F32),
        grid_spec=pltpu.PrefetchScalarGridSpec(
            num_scalar_prefetch=3,
            grid=(steps,),
            in_specs=[pl.BlockSpec(memory_space=pl.ANY),
                      wspec((d, D_EXPERT)), wspec((d, D_EXPERT)), wspec((D_EXPERT, d))],
            out_specs=pl.BlockSpec(memory_space=pl.ANY),
            scratch_shapes=[pltpu.VMEM((MOE_TM, d), F32)] * 4 + [pltpu.SemaphoreType.DMA((4,))],
        ),
        compiler_params=_cparams(("arbitrary",)),
        name="gmm",
    )(src, tile_expert, n_tiles, h2, w1, w3, w2)


def _route_tables(ids, n_tok):
    n_pairs = n_tok * TOP_K
    max_tiles = n_pairs // MOE_TM + N_EXPERTS
    e_flat = ids[:, :TOP_K].T.reshape(n_pairs)
    onehot = (e_flat[:, None] == jnp.arange(N_EXPERTS, dtype=I32)[None, :]).astype(I32)
    csum = jnp.cumsum(onehot, axis=0)
    counts = csum[-1]
    rank = jnp.sum(csum * onehot, axis=1) - 1
    tiles_per = (counts + MOE_TM - 1) // MOE_TM
    tile_end = jnp.cumsum(tiles_per)
    row_start = (tile_end - tiles_per + 1) * MOE_TM
    pos = row_start[e_flat] + rank
    src = jnp.full(((max_tiles + 3) * MOE_TM,), -1, I32).at[pos].set(jnp.arange(n_pairs, dtype=I32))
    steps = jnp.arange(max_tiles + 1, dtype=I32)
    owner = jnp.sum((tile_end[None, :] <= steps[:, None]).astype(I32), axis=1)
    tile_expert = jnp.minimum(owner, N_EXPERTS - 1)
    return src, tile_expert, tile_end[-1:].astype(I32)


def _combine_kernel(x1_ref, y0_ref, y1_ref, wts_ref, mod_ref, g_ref, o_ref):
    w = wts_ref[...]
    moe = w[:, 0:1] * y0_ref[...] + w[:, 1:2] * y1_ref[...]
    x2 = x1_ref[...] + mod_ref[0, 5:6, :] * moe
    o_ref[...] = _rms(x2) * g_ref[...]


def _combine(x1, y_pairs, wts, mods, mod_row0, g, row0, batch, t, n_tok):
    d = x1.shape[1]
    nt = t // CMB_TM
    rb0 = row0 // CMB_TM
    slot1 = n_tok // CMB_TM
    return pl.pallas_call(
        _combine_kernel,
        out_shape=jax.ShapeDtypeStruct((batch * t, d), F32),
        grid=(batch * nt,),
        in_specs=[pl.BlockSpec((CMB_TM, d), lambda i: (rb0 + i, 0)),
                  pl.BlockSpec((CMB_TM, d), lambda i: (rb0 + i, 0)),
                  pl.BlockSpec((CMB_TM, d), lambda i: (slot1 + rb0 + i, 0)),
                  pl.BlockSpec((CMB_TM, LANES), lambda i: (rb0 + i, 0)),
                  pl.BlockSpec((1, N_ADA, d), lambda i: (mod_row0 + i // nt, 0, 0)),
                  pl.BlockSpec((1, d), lambda i: (0, 0))],
        out_specs=pl.BlockSpec((CMB_TM, d), lambda i: (i, 0)),
        compiler_params=_cparams(("arbitrary",)),
        name="combine",
    )(x1, y_pairs, y_pairs, wts, mods, g.reshape(1, d))


def _rope_tables(n_lat):
    half = DA_DK // 2
    nf = half // 2
    t = jnp.arange(n_lat)
    row = (t // GRID_W).astype(F32)
    col = (t % GRID_W).astype(F32)
    inv = ROPE_THETA ** (-jnp.arange(nf, dtype=F32) / nf)
    lane = np.arange(HEAD_W)
    freq = inv[lane % nf]
    pos = jnp.where(((lane // half) % 2 == 0)[None, :], row[:, None], col[:, None])
    ang = pos * freq[None, :]
    sign = np.where((lane % half) < nf, -1.0, 1.0).astype(np.float32)
    return jnp.cos(ang), jnp.sin(ang) * sign[None, :]


def kernel(x_prompt, x_sample, cache_k, cache_v, state_hgrn_fwd, state_hgrn_bwd, c, c_ctx, w_ada, b_ada,
           norm1_g, norm2_g, norm_final_g, w_in, hg_lb_fwd, hg_lb_bwd, hg_norm_g, da_lambda_q1, da_lambda_k1,
           da_lambda_q2, da_lambda_k2, da_norm_g, w_out, router_g_w, router_g_b, router_e_w, router_e_b,
           exp_w1, exp_w3, exp_w2):
    l = 0
    batch, seq, d = x_prompt.shape
    dec_batch, dec_seq, _ = x_sample.shape
    n_ctx = batch * seq
    n_lat = dec_batch * dec_seq
    n_tok = n_ctx + n_lat
    lam_init = 0.8 - 0.6 * math.exp(-0.3 * l)

    cond = jnp.zeros((8, d), F32).at[0].set(c_ctx).at[1:1 + dec_batch].set(c)
    mods = _ada(cond, w_ada[l], b_ada[l]).reshape(8, N_ADA, d)

    w_in_bf16 = w_in[l].astype(BF16)
    proj_c = _in_proj(x_prompt.reshape(1, n_ctx, d), mods, 0, norm1_g[l], w_in_bf16, 1024, "in_proj_ctx")
    proj_l = _in_proj(x_sample, mods, 1, norm1_g[l], w_in_bf16, 1024, "in_proj_lat")
    lam_params = (da_lambda_q1[l], da_lambda_k1[l], da_lambda_q2[l], da_lambda_k2[l])

    da_ctx, new_k, new_v = _attn(proj_c, batch, seq, lam_params, da_norm_g[l], lam_init, emit_cache=True)
    hg_ctx, new_sf, new_sb = _hgrn(proj_c, batch, seq, hg_lb_fwd, hg_lb_bwd, hg_norm_g[l], emit_state=True)

    cos, sin = _rope_tables(dec_seq)
    (da_lat,) = _attn(proj_l, dec_batch, dec_seq, lam_params, da_norm_g[l], lam_init, cos=cos, sin=sin,
                      cache_k=cache_k[:, l:l + 1], cache_v=cache_v[:, l:l + 1])
    (hg_lat,) = _hgrn(proj_l, dec_batch, dec_seq, hg_lb_fwd, hg_lb_bwd, hg_norm_g[l],
                      s0f=state_hgrn_fwd[:, l:l + 1], s0b=state_hgrn_bwd[:, l:l + 1])

    rw = jnp.zeros((d, LANES), F32).at[:, :N_GROUPS].set(router_g_w[l]).at[:, N_GROUPS:N_GROUPS + N_EXPERTS].set(
        router_e_w[l])
    rw_hi = rw.astype(BF16)
    rw = jnp.stack([rw_hi, (rw - rw_hi.astype(F32)).astype(BF16)])
    rb = jnp.zeros((1, LANES), F32).at[0, :N_GROUPS].set(router_g_b[l]).at[0, N_GROUPS:N_GROUPS + N_EXPERTS].set(
        router_e_b[l])
    x1, h2, ids, wts = _out_proj(hg_ctx, da_ctx, x_prompt.reshape(n_ctx, d), hg_lat, da_lat,
                                 x_sample.reshape(n_lat, d), w_out[l].astype(BF16), mods, norm2_g[l], rw, rb, dec_seq)

    src, tile_expert, n_tiles = _route_tables(ids, n_tok)
    y_pairs = _gmm(src, tile_expert, n_tiles, h2, exp_w1[l], exp_w3[l], exp_w2[l], n_tok)

    y_ctx = _combine(x1, y_pairs, wts, mods, 0, norm_final_g, 0, 1, n_ctx, n_tok)
    y_lat = _combine(x1, y_pairs, wts, mods, 1, norm_final_g, n_ctx, dec_batch, dec_seq, n_tok)
    return (y_ctx.reshape(batch, seq, d), y_lat.reshape(dec_batch, dec_seq, d), new_k, new_v, new_sf, new_sb)
```

```python
import functools
import math

import numpy as np
import jax
import jax.numpy as jnp
from jax import lax
from jax.experimental import pallas as pl
from jax.experimental.pallas import tpu as pltpu

F32 = jnp.float32
BF16 = jnp.bfloat16
I32 = jnp.int32

GRID_W = 64
HG_WIDTH = 1024
HG_DK = 128
HG_HEADS = 8
DA_HEADS = 8
DA_DK = 64
HEAD_W = 128
ROPE_THETA = 10000.0
N_GROUPS = 4
EXP_PER_GROUP = 8
N_EXPERTS = 32
TOP_K = 2
D_EXPERT = 512
N_ADA = 6
RMS_EPS = 1e-6
CB_Q_HG, CB_F_FW, CB_F_BW, CB_I_HG, CB_G_HG, CB_Q_DA, CB_K_DA, CB_V_DA = (8 * i for i in range(8))

LANES = 128
SUBLANES = 8
VMEM_LIMIT = 56 * 1024 * 1024

ADA_TN = 1536
IN_TN = 512
OUT_TM = 256
HG_CHUNK = 64
HG_GROUP = 4
ATT_TQ = 256
ATT_TK = 512
MOE_TM = 256
CMB_TM = 256
NEG_INF = float("-inf")


def _cparams(sem):
    return pltpu.CompilerParams(dimension_semantics=sem, vmem_limit_bytes=VMEM_LIMIT)


def _silu(x):
    return x * jax.nn.sigmoid(x)


def _rms(x):
    return x * lax.rsqrt(jnp.mean(x * x, axis=-1, keepdims=True) + RMS_EPS)


def _ada_kernel(cond_ref, w_ref, b_ref, o_ref):
    s = _silu(cond_ref[...]).astype(BF16)
    o_ref[...] = jnp.dot(s, w_ref[...].astype(BF16), preferred_element_type=F32) + b_ref[...]


def _ada(cond, w, b):
    rows, d = cond.shape
    n = w.shape[1]
    return pl.pallas_call(
        _ada_kernel,
        out_shape=jax.ShapeDtypeStruct((rows, n), F32),
        grid=(n // ADA_TN,),
        in_specs=[pl.BlockSpec((rows, d), lambda j: (0, 0)),
                  pl.BlockSpec((d, ADA_TN), lambda j: (0, j)),
                  pl.BlockSpec((1, ADA_TN), lambda j: (0, j))],
        out_specs=pl.BlockSpec((rows, ADA_TN), lambda j: (0, j)),
        compiler_params=_cparams(("arbitrary",)),
        name="ada",
    )(cond, w, b.reshape(1, n))


def _in_kernel(x_ref, mod_ref, g_ref, w_ref, o_ref, h_ref):
    @pl.when(pl.program_id(1) == 0)
    def _():
        y = _rms(x_ref[...]) * g_ref[...]
        h_ref[...] = (y * (1.0 + mod_ref[0, 1:2, :]) + mod_ref[0, 0:1, :]).astype(BF16)

    o_ref[...] = jnp.dot(h_ref[...], w_ref[...], preferred_element_type=F32)


def _in_proj(x, mods, mod_row0, g, w_bf16, tm, name):
    batch, t, d = x.shape
    cols = w_bf16.shape[1]
    nt = t // tm
    return pl.pallas_call(
        _in_kernel,
        out_shape=jax.ShapeDtypeStruct((batch * t, cols), F32),
        grid=(batch * nt, cols // IN_TN),
        in_specs=[pl.BlockSpec((tm, d), lambda i, j: (i, 0)),
                  pl.BlockSpec((1, N_ADA, d), lambda i, j: (mod_row0 + i // nt, 0, 0)),
                  pl.BlockSpec((1, d), lambda i, j: (0, 0)),
                  pl.BlockSpec((d, IN_TN), lambda i, j: (0, j))],
        out_specs=pl.BlockSpec((tm, IN_TN), lambda i, j: (i, j)),
        scratch_shapes=[pltpu.VMEM((tm, d), BF16)],
        compiler_params=_cparams(("arbitrary", "arbitrary")),
        name=name,
    )(x.reshape(batch * t, d), mods, g.reshape(1, d), w_bf16)


def _rope(x, cos, sin_signed):
    lane = lax.broadcasted_iota(I32, x.shape, 1)
    first = (lane % 32) < 16
    partner = jnp.where(first, pltpu.roll(x, LANES - 16, 1), pltpu.roll(x, 16, 1))
    return x * cos + partner * sin_signed


def _split_maps(k):
    lane = lax.broadcasted_iota(I32, k.shape, 1)
    m1 = lane < DA_DK
    return jnp.where(m1, k, 0.0).astype(BF16), jnp.where(m1, 0.0, k).astype(BF16)


def _lower_bound(lb_ref):
    p = lb_ref[...]
    e = jnp.exp(p - jnp.max(p, axis=0, keepdims=True))
    return e[0:1, :] / jnp.sum(e, axis=0, keepdims=True)


def _hgrn_pair_masks(mask_ref, c):
    ti = lax.broadcasted_iota(I32, (c, c), 0)
    si = lax.broadcasted_iota(I32, (c, c), 1)
    for d in range(2):
        mask_ref[d, 0] = (ti == si).astype(F32)
        for j in range(c.bit_length() - 1):
            same = (ti >> (j + 1)) == (si >> (j + 1))
            t_bit = ((ti >> j) & 1) == 1
            s_bit = ((si >> j) & 1) == 1
            pair = (s_bit & jnp.logical_not(t_bit)) if d else (t_bit & jnp.logical_not(s_bit))
            mask_ref[d, 1 + j] = (same & pair).astype(F32)


def _hgrn_chunk(q, f_logit, v, lb, mask_ref, rev):
    c = q.shape[0]
    n_piece = c // SUBLANES
    levels = c.bit_length() - 1
    low = SUBLANES.bit_length() - 1
    d = 1 if rev else 0
    nt = (((1,), (1,)), ((), ()))
    pieces = lambda x: [x[SUBLANES * i:SUBLANES * (i + 1)] for i in range(n_piece)]
    whole = lambda xs: jnp.concatenate(xs, axis=0)

    fg = lb + (1.0 - lb) * jax.nn.sigmoid(f_logit)
    k = 1.0 - fg
    q_bf = q.astype(BF16)
    k_bf = k.astype(BF16)
    qs, ks, fgs = pieces(q), pieces(k), pieces(fg)
    tot = pieces(jnp.log(fg))
    pre = list(tot)
    suf = [jnp.zeros((SUBLANES, LANES), F32)] * n_piece
    sub = lax.broadcasted_iota(I32, (SUBLANES, LANES), 0)

    def pair_scores(qt, kt, idx):
        return lax.dot_general(qt, kt, nt, preferred_element_type=F32) * mask_ref[d, idx]

    scores = pair_scores(q_bf, k_bf, 0)
    yield
    for j in range(levels):
        if j < low:
            sh = 1 << j
            bit = ((sub >> j) & 1) == 1
            late = jnp.logical_not(bit) if rev else bit
            if j == 0:
                qt = whole([qs[i] * jnp.where(late, fgs[i], 1.0) for i in range(n_piece)]).astype(BF16)
                kt = k_bf
            else:
                es = [jnp.exp(jnp.where(late, pre[i], suf[i])) for i in range(n_piece)]
                qt = whole([qs[i] * es[i] for i in range(n_piece)]).astype(BF16)
                kt = whole([ks[i] * es[i] for i in range(n_piece)]).astype(BF16)
            for i in range(n_piece):
                up = pltpu.roll(tot[i], sh, 0)
                dn = pltpu.roll(tot[i], SUBLANES - sh, 0)
                sib = jnp.where(late, dn, up) if rev else jnp.where(late, up, dn)
                pre[i] = pre[i] + jnp.where(late, sib, 0.0)
                suf[i] = suf[i] + jnp.where(late, 0.0, sib)
                tot[i] = tot[i] + sib
        else:
            half = 1 << (j - low)
            upper = [(i // half) % 2 == 1 for i in range(n_piece)]
            late = [(not u) if rev else u for u in upper]
            es = [jnp.exp(pre[i] if late[i] else suf[i]) for i in range(n_piece)]
            qt = whole([qs[i] * es[i] for i in range(n_piece)]).astype(BF16)
            kt = whole([ks[i] * es[i] for i in range(n_piece)]).astype(BF16)
            sib = [tot[i - half] if upper[i] else tot[i + half] for i in range(n_piece)]
            pre = [pre[i] + sib[i] if late[i] else pre[i] for i in range(n_piece)]
            suf = [suf[i] if late[i] else suf[i] + sib[i] for i in range(n_piece)]
            tot = [tot[i] + sib[i] for i in range(n_piece)]
        scores = scores + pair_scores(qt, kt, 1 + j)
        yield

    q_dec = whole([qs[i] * jnp.exp(pre[i]) for i in range(n_piece)]).astype(BF16)
    k_dec = whole([ks[i] * jnp.exp(suf[i]) for i in range(n_piece)]).astype(BF16)
    return dict(q_dec=q_dec, k_dec=k_dec, v_t=v.T.astype(BF16), v=v.astype(BF16),
                scores=scores.astype(BF16), decay=jnp.exp(tot[0][0:1, :]))


def _hgrn_state_step(p, st_ref):
    st = st_ref[...]
    o = lax.dot_general(p["q_dec"], st.astype(BF16), (((1,), (1,)), ((), ())), preferred_element_type=F32)
    st_ref[...] = p["decay"] * st + jnp.dot(p["v_t"], p["k_dec"], preferred_element_type=F32)
    return o


def _lockstep(gens):
    results = [None] * len(gens)
    live = list(range(len(gens)))
    while live:
        for idx in list(live):
            try:
                next(gens[idx])
            except StopIteration as stop:
                results[idx] = stop.value
                live.remove(idx)
    return results


def _hgrn_kernel(*refs, t, has_init, emit_state):
    q_ref, ff_ref, fb_ref, i_ref, g_ref, lbf_ref, lbb_ref, ng_ref = refs[:8]
    pos = 8
    if has_init:
        s0f_ref, s0b_ref = refs[pos:pos + 2]
        pos += 2
    o_ref = refs[pos]
    pos += 1
    if emit_state:
        sf_ref, sb_ref = refs[pos:pos + 2]
        pos += 2
    of_scr, ob_scr, stf, stb, mask_scr = refs[pos:pos + 5]

    c = HG_CHUNK
    n = t // c
    _hgrn_pair_masks(mask_scr, c)
    if has_init:
        stf[...] = s0f_ref[0, 0, 0].T
        stb[...] = s0b_ref[0, 0, 0].T
    else:
        stf[...] = jnp.zeros_like(stf)
        stb[...] = jnp.zeros_like(stb)
    lbf = _lower_bound(lbf_ref)
    lbb = _lower_bound(lbb_ref)

    def body(i, carry):
        work = []
        for u in range(HG_GROUP):
            ci = i * HG_GROUP + u
            work.append((pl.ds(pl.multiple_of(ci * c, c), c), ff_ref, lbf, stf, of_scr, False))
            work.append((pl.ds(pl.multiple_of((n - 1 - ci) * c, c), c), fb_ref, lbb, stb, ob_scr, True))
        parts = _lockstep([_hgrn_chunk(_silu(q_ref[sl, :]), f_ref[sl, :], i_ref[sl, :], lb, mask_scr, rev)
                           for sl, f_ref, lb, _, _, rev in work])
        outs = [_hgrn_state_step(p, w[3]) for p, w in zip(parts, work)]
        for p, w, o in zip(parts, work, outs):
            w[4][w[0], :] = o + jnp.dot(p["scores"], p["v"], preferred_element_type=F32)
        return carry

    lax.fori_loop(0, n // HG_GROUP, body, 0)

    fin = min(t, 256)

    def finish(i, carry):
        sl = pl.ds(pl.multiple_of(i * fin, fin), fin)
        o = _rms(of_scr[sl, :] + ob_scr[sl, :]) * ng_ref[...]
        o_ref[sl, :] = (o * _silu(g_ref[sl, :])).astype(BF16)
        return carry

    lax.fori_loop(0, t // fin, finish, 0)
    if emit_state:
        sf_ref[0, 0, 0] = stf[...].T
        sb_ref[0, 0, 0] = stb[...].T


def _hgrn(proj, batch, t, lb_fwd, lb_bwd, norm_g, s0f=None, s0b=None, emit_state=False):
    has_init = s0f is not None

    def col(cb):
        return pl.BlockSpec((t, HEAD_W), lambda b, h: (b, cb + h))

    lb_spec = pl.BlockSpec((lb_fwd.shape[0], HEAD_W), lambda b, h: (0, h))
    in_specs = [col(CB_Q_HG), col(CB_F_FW), col(CB_F_BW), col(CB_I_HG), col(CB_G_HG),
                lb_spec, lb_spec, pl.BlockSpec((1, HEAD_W), lambda b, h: (0, 0))]
    args = [proj] * 5 + [lb_fwd, lb_bwd, norm_g.reshape(1, HEAD_W)]
    st_spec = pl.BlockSpec((1, 1, 1, HG_DK, HEAD_W), lambda b, h: (b, 0, h, 0, 0))
    if has_init:
        in_specs += [st_spec, st_spec]
        args += [s0f, s0b]
    out_shape = [jax.ShapeDtypeStruct((batch * t, HG_WIDTH), BF16)]
    out_specs = [pl.BlockSpec((t, HEAD_W), lambda b, h: (b, h))]
    if emit_state:
        st = jax.ShapeDtypeStruct((batch, 1, HG_HEADS, HG_DK, HEAD_W), F32)
        out_shape += [st, st]
        out_specs += [st_spec, st_spec]
    return pl.pallas_call(
        functools.partial(_hgrn_kernel, t=t, has_init=has_init, emit_state=emit_state),
        out_shape=out_shape,
        grid=(batch, HG_HEADS),
        in_specs=in_specs,
        out_specs=out_specs,
        scratch_shapes=[pltpu.VMEM((t, HEAD_W), F32), pltpu.VMEM((t, HEAD_W), F32),
                        pltpu.VMEM((HEAD_W, HG_DK), F32), pltpu.VMEM((HEAD_W, HG_DK), F32),
                        pltpu.VMEM((2, HG_CHUNK.bit_length(), HG_CHUNK, HG_CHUNK), F32)],
        compiler_params=_cparams(("arbitrary", "arbitrary")),
        name="hgrn_lat" if has_init else "hgrn_ctx",
    )(*args)


def _attn_kernel(*refs, t, n_cache, use_rope, emit_cache, lam_init):
    it = iter(refs)
    q_ref, k_ref, v_ref = next(it), next(it), next(it)
    if use_rope:
        cosq_ref, sinq_ref, cos_ref, sin_ref = next(it), next(it), next(it), next(it)
    if n_cache:
        ck_ref, cv_ref = next(it), next(it)
    lq1_ref, lk1_ref, lq2_ref, lk2_ref, ng_ref, o_ref = (next(it) for _ in range(6))
    if emit_cache:
        nk_ref, nv_ref = next(it), next(it)
    k1_scr, k2_scr, v_scr, s_scr, p_scr = (next(it) for _ in range(5))
    k_scrs = (k1_scr, k2_scr)
    tq = q_ref.shape[0]

    @pl.when(pl.program_id(2) == 0)
    def _():
        rows = min(t, ATT_TK)

        def stage(i, carry):
            sl = pl.ds(pl.multiple_of(i * rows, rows), rows)
            k = k_ref[sl, :]
            v = v_ref[sl, :]
            if emit_cache:
                nk_ref[0, 0, 0, sl, :] = k
                nv_ref[0, 0, 0, sl, :] = v
            if use_rope:
                k = _rope(k, cos_ref[sl, :], sin_ref[sl, :])
            k1_scr[sl, :], k2_scr[sl, :] = _split_maps(k)
            v_scr[sl, :] = v.astype(BF16)
            return carry

        lax.fori_loop(0, t // rows, stage, 0)
        if n_cache:
            k1_scr[t:t + n_cache, :], k2_scr[t:t + n_cache, :] = _split_maps(ck_ref[0, 0, 0])
            v_scr[t:t + n_cache, :] = cv_ref[0, 0, 0].astype(BF16)

    q = q_ref[...]
    if use_rope:
        q = _rope(q, cosq_ref[...], sinq_ref[...])
    q = (q * (DA_DK ** -0.5)).astype(BF16)
    nt = (((1,), (1,)), ((), ()))
    n_keys = t + n_cache
    tiles = [(st, min(ATT_TK, n_keys - st)) for st in range(0, n_keys, ATT_TK)]

    def scores_tile(mp, st, sz, mx):
        s = lax.dot_general(q, k_scrs[mp][st:st + sz, :], nt, preferred_element_type=F32)
        s_scr[mp, :, st:st + sz] = s
        for j in range(sz // LANES):
            mx = jnp.maximum(mx, s[:, j * LANES:(j + 1) * LANES])
        return mx

    def exp_tile(mp, st, sz, m, part):
        for lo in range(st, st + sz, LANES):
            e = jnp.exp(s_scr[mp, :, lo:lo + LANES] - m)
            part = part + e
            p_scr[mp, :, lo:lo + LANES] = e.astype(BF16)
        return part

    def value_tile(mp, st, sz, acc):
        return acc + jnp.dot(p_scr[mp, :, st:st + sz], v_scr[st:st + sz, :], preferred_element_type=F32)

    def row_stat(x, op):
        return jnp.broadcast_to(op(x, axis=-1, keepdims=True), (tq, LANES))

    neg = jnp.full((tq, LANES), NEG_INF, F32)
    zero = jnp.zeros((tq, LANES), F32)
    mx = neg
    for st, sz in tiles:
        mx = scores_tile(0, st, sz, mx)
    m0 = row_stat(mx, jnp.max)
    mx, part0 = neg, zero
    for st, sz in tiles:
        mx = scores_tile(1, st, sz, mx)
        part0 = exp_tile(0, st, sz, m0, part0)
    m1 = row_stat(mx, jnp.max)
    acc0, part1 = zero, zero
    for st, sz in tiles:
        acc0 = value_tile(0, st, sz, acc0)
        part1 = exp_tile(1, st, sz, m1, part1)
    acc1 = jnp.dot(p_scr[1], v_scr[...], preferred_element_type=F32)

    lam = (jnp.exp(jnp.sum(lq1_ref[...] * lk1_ref[...], axis=-1, keepdims=True))
           - jnp.exp(jnp.sum(lq2_ref[...] * lk2_ref[...], axis=-1, keepdims=True)) + lam_init)
    o = acc0 / row_stat(part0, jnp.sum) - lam * (acc1 / row_stat(part1, jnp.sum))
    o_ref[...] = (_rms(o) * ng_ref[...] * (1.0 - lam_init)).astype(BF16)


def _attn(proj, batch, t, lam_params, norm_g, lam_init, cos=None, sin=None, cache_k=None, cache_v=None,
          emit_cache=False):
    use_rope = cos is not None
    n_cache = 0 if cache_k is None else cache_k.shape[3]
    tq = min(ATT_TQ, t)
    nq = t // tq
    n_keys = t + n_cache

    def col(cb, rows, row_map):
        return pl.BlockSpec((rows, HEAD_W), lambda b, h, i: (row_map(b, i), cb + h))

    in_specs = [col(CB_Q_DA, tq, lambda b, i: b * nq + i),
                col(CB_K_DA, t, lambda b, i: b), col(CB_V_DA, t, lambda b, i: b)]
    args = [proj, proj, proj]
    if use_rope:
        in_specs += [pl.BlockSpec((tq, HEAD_W), lambda b, h, i: (i, 0))] * 2
        in_specs += [pl.BlockSpec((t, HEAD_W), lambda b, h, i: (0, 0))] * 2
        args += [cos, sin, cos, sin]
    cache_spec = lambda n: pl.BlockSpec((1, 1, 1, n, HEAD_W), lambda b, h, i: (b, 0, h, 0, 0))
    if n_cache:
        in_specs += [cache_spec(n_cache)] * 2
        args += [cache_k, cache_v]
    small = pl.BlockSpec((1, DA_DK), lambda b, h, i: (0, 0))
    in_specs += [small] * 4 + [pl.BlockSpec((1, HEAD_W), lambda b, h, i: (0, 0))]
    args += [p.reshape(1, DA_DK) for p in lam_params] + [norm_g.reshape(1, HEAD_W)]
    out_shape = [jax.ShapeDtypeStruct((batch * t, DA_HEADS * HEAD_W), BF16)]
    out_specs = [pl.BlockSpec((tq, HEAD_W), lambda b, h, i: (b * nq + i, h))]
    if emit_cache:
        out_shape += [jax.ShapeDtypeStruct((batch, 1, DA_HEADS, t, HEAD_W), F32)] * 2
        out_specs += [cache_spec(t)] * 2
    scratch = [pltpu.VMEM((n_keys, HEAD_W), BF16)] * 3
    scratch += [pltpu.VMEM((2, tq, n_keys), F32), pltpu.VMEM((2, tq, n_keys), BF16)]
    return pl.pallas_call(
        functools.partial(_attn_kernel, t=t, n_cache=n_cache, use_rope=use_rope, emit_cache=emit_cache,
                          lam_init=lam_init),
        out_shape=out_shape,
        grid=(batch, DA_HEADS, nq),
        in_specs=in_specs,
        out_specs=out_specs,
        scratch_shapes=scratch,
        compiler_params=_cparams(("arbitrary",) * 3),
        name="attn_lat" if use_rope else "attn_ctx",
    )(*args)


def _out_kernel(mhg_c, mda_c, x_c, mhg_l, mda_l, x_l, w_ref, mod_ref, g_ref, rw_ref, rb_ref,
                x1_ref, h2_ref, ids_ref, wts_ref, *, ctx_tiles):
    def body(mhg_ref, mda_ref, x_ref):
        mix = jnp.dot(mhg_ref[...], w_ref[0:HG_WIDTH, :], preferred_element_type=F32)
        mix = mix + jnp.dot(mda_ref[...], w_ref[HG_WIDTH:, :], preferred_element_type=F32)
        x1 = x_ref[...] + mod_ref[0, 2:3, :] * mix
        x1_ref[...] = x1
        h2 = (_rms(x1) * g_ref[...]) * (1.0 + mod_ref[0, 4:5, :]) + mod_ref[0, 3:4, :]
        h2_ref[...] = h2

        h_hi = h2.astype(BF16)
        h_lo = (h2 - h_hi.astype(F32)).astype(BF16)
        logit = (jnp.dot(h_hi, rw_ref[0], preferred_element_type=F32)
                 + jnp.dot(h_lo, rw_ref[0], preferred_element_type=F32)
                 + jnp.dot(h_hi, rw_ref[1], preferred_element_type=F32)) + rb_ref[...]
        lane = lax.broadcasted_iota(I32, logit.shape, 1)

        def first_max(x):
            m = jnp.max(x, axis=-1, keepdims=True)
            return m, jnp.min(jnp.where(x == m, lane, LANES), axis=-1, keepdims=True)

        gmask = lane < N_GROUPS
        gmax, gsel = first_max(jnp.where(gmask, logit, NEG_INF))
        p_grp = 1.0 / jnp.sum(jnp.where(gmask, jnp.exp(logit - gmax), 0.0), axis=-1, keepdims=True)
        lo = N_GROUPS + EXP_PER_GROUP * gsel
        le = jnp.where((lane >= lo) & (lane < lo + EXP_PER_GROUP), logit, NEG_INF)
        v1, i1 = first_max(le)
        v2, i2 = first_max(jnp.where(lane == i1, NEG_INF, le))
        e = jnp.exp(v2 - v1)
        w1 = p_grp / (1.0 + e)
        w2 = p_grp * e / (1.0 + e)
        ids_ref[...] = jnp.where(lane == 0, i1 - N_GROUPS, jnp.where(lane == 1, i2 - N_GROUPS, 0))
        wts_ref[...] = jnp.where(lane == 0, w1, jnp.where(lane == 1, w2, 0.0))

    i = pl.program_id(0)
    pl.when(i < ctx_tiles)(lambda: body(mhg_c, mda_c, x_c))
    pl.when(i >= ctx_tiles)(lambda: body(mhg_l, mda_l, x_l))


def _out_proj(mhg_c, mda_c, x_c, mhg_l, mda_l, x_l, w_bf16, mods, g, rw, rb, lat_t):
    n_ctx, d = x_c.shape
    n_lat = x_l.shape[0]
    n = n_ctx + n_lat
    ctx_tiles = n_ctx // OUT_TM
    lat_tiles = lat_t // OUT_TM
    row = lambda i: (i, 0)
    const = lambda i: (0, 0)
    crow = lambda i: (jnp.minimum(i, ctx_tiles - 1), 0)
    lrow = lambda i: (jnp.maximum(i - ctx_tiles, 0), 0)
    seg = lambda i: (jnp.where(i < ctx_tiles, 0, 1 + (i - ctx_tiles) // lat_tiles), 0, 0)
    half = lambda m: pl.BlockSpec((OUT_TM, HG_WIDTH), m)
    full = lambda m: pl.BlockSpec((OUT_TM, d), m)
    return pl.pallas_call(
        functools.partial(_out_kernel, ctx_tiles=ctx_tiles),
        out_shape=[jax.ShapeDtypeStruct((n, d), F32), jax.ShapeDtypeStruct((n, d), F32),
                   jax.ShapeDtypeStruct((n, LANES), I32), jax.ShapeDtypeStruct((n, LANES), F32)],
        grid=(n // OUT_TM,),
        in_specs=[half(crow), half(crow), full(crow), half(lrow), half(lrow), full(lrow),
                  pl.BlockSpec((d, d), const), pl.BlockSpec((1, N_ADA, d), seg), pl.BlockSpec((1, d), const),
                  pl.BlockSpec((2, d, LANES), lambda i: (0, 0, 0)), pl.BlockSpec((1, LANES), const)],
        out_specs=[full(row), full(row), pl.BlockSpec((OUT_TM, LANES), row), pl.BlockSpec((OUT_TM, LANES), row)],
        compiler_params=_cparams(("arbitrary",)),
        name="out_proj",
    )(mhg_c, mda_c, x_c, mhg_l, mda_l, x_l, w_bf16, mods, g.reshape(1, d), rw, rb)


def _gmm_kernel(src_ref, texp_ref, ntile_ref, h_hbm, w1_ref, w3_ref, w2_ref, y_ref, x0, x1, sems, *, n_tok):
    i = pl.program_id(0)
    n_tiles = ntile_ref[0]
    xs = (x0, x1)

    def gather_copy(tile, r, slot):
        p = jnp.maximum(src_ref[tile * MOE_TM + r], 0)
        tok = jnp.where(p >= n_tok, p - n_tok, p)
        return pltpu.make_async_copy(h_hbm.at[pl.ds(tok, 1)], xs[slot].at[pl.ds(r, 1)], sems.at[slot])

    def wait_gather(slot):
        pltpu.make_async_copy(h_hbm.at[pl.ds(0, MOE_TM)], xs[slot], sems.at[slot]).wait()

    @pl.when(i == 0)
    def _():
        for r in range(MOE_TM):
            gather_copy(0, r, 0).start()

    def step(cur):
        nxt = 1 - cur
        wait_gather(cur)
        for r in range(MOE_TM):
            gather_copy(i + 1, r, nxt).start()
        x = xs[cur][...].astype(BF16)
        a = jnp.dot(x, w1_ref[0].astype(BF16), preferred_element_type=F32)
        b = jnp.dot(x, w3_ref[0].astype(BF16), preferred_element_type=F32)
        hid = (_silu(a) * b).astype(BF16)
        y_ref[...] = jnp.dot(hid, w2_ref[0].astype(BF16), preferred_element_type=F32)

        @pl.when(i == n_tiles - 1)
        def _():
            wait_gather(nxt)

    live = i < n_tiles
    pl.when(live & (i % 2 == 0))(lambda: step(0))
    pl.when(live & (i % 2 == 1))(lambda: step(1))

    @pl.when(jnp.logical_not(live))
    def _():
        y_ref[...] = jnp.zeros_like(y_ref)


def _gmm(src, tile_expert, n_tiles, h2, w1, w3, w2, n_tok):
    d = h2.shape[1]
    steps = tile_expert.shape[0]
    wspec = lambda shape: pl.BlockSpec((1,) + shape, lambda i, src, te, nt: (te[i], 0, 0))
    return pl.pallas_call(
        functools.partial(_gmm_kernel, n_tok=n_tok),
        out_shape=jax.ShapeDtypeStruct((steps * MOE_TM, d), F32),
        grid_spec=pltpu.PrefetchScalarGridSpec(
            num_scalar_prefetch=3,
            grid=(steps,),
            in_specs=[pl.BlockSpec(memory_space=pl.ANY),
                      wspec((d, D_EXPERT)), wspec((d, D_EXPERT)), wspec((D_EXPERT, d))],
            out_specs=pl.BlockSpec((MOE_TM, d), lambda i, src, te, nt: (i, 0)),
            scratch_shapes=[pltpu.VMEM((MOE_TM, d), F32)] * 2 + [pltpu.SemaphoreType.DMA((2,))],
        ),
        compiler_params=_cparams(("arbitrary",)),
        name="gmm",
    )(src, tile_expert, n_tiles, h2, w1, w3, w2)


def _route_tables(ids, n_tok):
    n_pairs = n_tok * TOP_K
    max_tiles = n_pairs // MOE_TM + N_EXPERTS
    e_flat = ids[:, :TOP_K].T.reshape(n_pairs)
    onehot = (e_flat[:, None] == jnp.arange(N_EXPERTS, dtype=I32)[None, :]).astype(I32)
    csum = jnp.cumsum(onehot, axis=0)
    counts = csum[-1]
    rank = jnp.sum(csum * onehot, axis=1) - 1
    tiles_per = (counts + MOE_TM - 1) // MOE_TM
    tile_end = jnp.cumsum(tiles_per)
    row_start = (tile_end - tiles_per) * MOE_TM
    pos = row_start[e_flat] + rank
    src = jnp.full(((max_tiles + 1) * MOE_TM,), -1, I32).at[pos].set(jnp.arange(n_pairs, dtype=I32))
    steps = jnp.arange(max_tiles, dtype=I32)
    owner = jnp.sum((tile_end[None, :] <= steps[:, None]).astype(I32), axis=1)
    tile_expert = jnp.minimum(owner, N_EXPERTS - 1)
    return src, pos, tile_expert, tile_end[-1:].astype(I32)


def _combine_kernel(pos_ref, x1_ref, wts_ref, mod_ref, g_ref, y_hbm, o_ref, ya0, yb0, ya1, yb1, sems,
                    *, tok0, n_tok, n_steps):
    i = pl.program_id(0)
    bufs = ((ya0, yb0), (ya1, yb1))

    def row_copy(tile, r, slot, which):
        row = pos_ref[which * n_tok + tok0 + tile * CMB_TM + r]
        return pltpu.make_async_copy(y_hbm.at[pl.ds(row, 1)], bufs[slot][which].at[pl.ds(r, 1)],
                                     sems.at[2 * slot + which])

    def issue(tile, slot):
        for r in range(CMB_TM):
            row_copy(tile, r, slot, 0).start()
            row_copy(tile, r, slot, 1).start()

    def wait(slot):
        for which in range(2):
            pltpu.make_async_copy(y_hbm.at[pl.ds(0, CMB_TM)], bufs[slot][which], sems.at[2 * slot + which]).wait()

    @pl.when(i == 0)
    def _():
        issue(0, 0)

    def step(cur):
        nxt = 1 - cur
        wait(cur)
        issue(jnp.minimum(i + 1, n_steps - 1), nxt)
        w = wts_ref[...]
        moe = w[:, 0:1] * bufs[cur][0][...] + w[:, 1:2] * bufs[cur][1][...]
        x2 = x1_ref[...] + mod_ref[0, 5:6, :] * moe
        o_ref[...] = _rms(x2) * g_ref[...]

        @pl.when(i == n_steps - 1)
        def _():
            wait(nxt)

    pl.when(i % 2 == 0)(lambda: step(0))
    pl.when(i % 2 == 1)(lambda: step(1))


def _combine(pos, x1, y_sorted, wts, mods, mod_row0, g, row0, batch, t, n_tok):
    d = x1.shape[1]
    nt = t // CMB_TM
    rb0 = row0 // CMB_TM
    n_steps = batch * nt
    return pl.pallas_call(
        functools.partial(_combine_kernel, tok0=row0, n_tok=n_tok, n_steps=n_steps),
        out_shape=jax.ShapeDtypeStruct((batch * t, d), F32),
        grid_spec=pltpu.PrefetchScalarGridSpec(
            num_scalar_prefetch=1,
            grid=(n_steps,),
            in_specs=[pl.BlockSpec((CMB_TM, d), lambda i, pos: (rb0 + i, 0)),
                      pl.BlockSpec((CMB_TM, LANES), lambda i, pos: (rb0 + i, 0)),
                      pl.BlockSpec((1, N_ADA, d), lambda i, pos: (mod_row0 + i // nt, 0, 0)),
                      pl.BlockSpec((1, d), lambda i, pos: (0, 0)),
                      pl.BlockSpec(memory_space=pl.ANY)],
            out_specs=pl.BlockSpec((CMB_TM, d), lambda i, pos: (i, 0)),
            scratch_shapes=[pltpu.VMEM((CMB_TM, d), F32)] * 4 + [pltpu.SemaphoreType.DMA((4,))],
        ),
        compiler_params=_cparams(("arbitrary",)),
        name="combine",
    )(pos, x1, wts, mods, g.reshape(1, d), y_sorted)


def _rope_tables(n_lat):
    half = DA_DK // 2
    nf = half // 2
    t = jnp.arange(n_lat)
    row = (t // GRID_W).astype(F32)
    col = (t % GRID_W).astype(F32)
    inv = ROPE_THETA ** (-jnp.arange(nf, dtype=F32) / nf)
    lane = np.arange(HEAD_W)
    freq = inv[lane % nf]
    pos = jnp.where(((lane // half) % 2 == 0)[None, :], row[:, None], col[:, None])
    ang = pos * freq[None, :]
    sign = np.where((lane % half) < nf, -1.0, 1.0).astype(np.float32)
    return jnp.cos(ang), jnp.sin(ang) * sign[None, :]


def kernel(x_prompt, x_sample, cache_k, cache_v, state_hgrn_fwd, state_hgrn_bwd, c, c_ctx, w_ada, b_ada,
           norm1_g, norm2_g, norm_final_g, w_in, hg_lb_fwd, hg_lb_bwd, hg_norm_g, da_lambda_q1, da_lambda_k1,
           da_lambda_q2, da_lambda_k2, da_norm_g, w_out, router_g_w, router_g_b, router_e_w, router_e_b,
           exp_w1, exp_w3, exp_w2):
    l = 0
    batch, seq, d = x_prompt.shape
    dec_batch, dec_seq, _ = x_sample.shape
    n_ctx = batch * seq
    n_lat = dec_batch * dec_seq
    n_tok = n_ctx + n_lat
    lam_init = 0.8 - 0.6 * math.exp(-0.3 * l)

    cond = jnp.zeros((8, d), F32).at[0].set(c_ctx).at[1:1 + dec_batch].set(c)
    mods = _ada(cond, w_ada[l], b_ada[l]).reshape(8, N_ADA, d)

    w_in_bf16 = w_in[l].astype(BF16)
    proj_c = _in_proj(x_prompt.reshape(1, n_ctx, d), mods, 0, norm1_g[l], w_in_bf16, 1024, "in_proj_ctx")
    proj_l = _in_proj(x_sample, mods, 1, norm1_g[l], w_in_bf16, 1024, "in_proj_lat")
    lam_params = (da_lambda_q1[l], da_lambda_k1[l], da_lambda_q2[l], da_lambda_k2[l])

    da_ctx, new_k, new_v = _attn(proj_c, batch, seq, lam_params, da_norm_g[l], lam_init, emit_cache=True)
    hg_ctx, new_sf, new_sb = _hgrn(proj_c, batch, seq, hg_lb_fwd, hg_lb_bwd, hg_norm_g[l], emit_state=True)

    cos, sin = _rope_tables(dec_seq)
    (da_lat,) = _attn(proj_l, dec_batch, dec_seq, lam_params, da_norm_g[l], lam_init, cos=cos, sin=sin,
                      cache_k=cache_k[:, l:l + 1], cache_v=cache_v[:, l:l + 1])
    (hg_lat,) = _hgrn(proj_l, dec_batch, dec_seq, hg_lb_fwd, hg_lb_bwd, hg_norm_g[l],
                      s0f=state_hgrn_fwd[:, l:l + 1], s0b=state_hgrn_bwd[:, l:l + 1])

    rw = jnp.zeros((d, LANES), F32).at[:, :N_GROUPS].set(router_g_w[l]).at[:, N_GROUPS:N_GROUPS + N_EXPERTS].set(
        router_e_w[l])
    rw_hi = rw.astype(BF16)
    rw = jnp.stack([rw_hi, (rw - rw_hi.astype(F32)).astype(BF16)])
    rb = jnp.zeros((1, LANES), F32).at[0, :N_GROUPS].set(router_g_b[l]).at[0, N_GROUPS:N_GROUPS + N_EXPERTS].set(
        router_e_b[l])
    x1, h2, ids, wts = _out_proj(hg_ctx, da_ctx, x_prompt.reshape(n_ctx, d), hg_lat, da_lat,
                                 x_sample.reshape(n_lat, d), w_out[l].astype(BF16), mods, norm2_g[l], rw, rb, dec_seq)

    src, pos, tile_expert, n_tiles = _route_tables(ids, n_tok)
    y_sorted = _gmm(src, tile_expert, n_tiles, h2, exp_w1[l], exp_w3[l], exp_w2[l], n_tok)

    y_ctx = _combine(pos, x1, y_sorted, wts, mods, 0, norm_final_g, 0, 1, n_ctx, n_tok)
    y_lat = _combine(pos, x1, y_sorted, wts, mods, 1, norm_final_g, n_ctx, dec_batch, dec_seq, n_tok)
    return (y_ctx.reshape(batch, seq, d), y_lat.reshape(dec_batch, dec_seq, d), new_k, new_v, new_sf, new_sb)
```

```python
import functools
import math

import numpy as np
import jax
import jax.numpy as jnp
from jax import lax
from jax.experimental import pallas as pl
from jax.experimental.pallas import tpu as pltpu

F32 = jnp.float32
BF16 = jnp.bfloat16
I32 = jnp.int32

GRID_W = 64
HG_WIDTH = 1024
HG_DK = 128
HG_HEADS = 8
DA_HEADS = 8
DA_DK = 64
HEAD_W = 128
ROPE_THETA = 10000.0
N_GROUPS = 4
EXP_PER_GROUP = 8
N_EXPERTS = 32
TOP_K = 2
D_EXPERT = 512
N_ADA = 6
RMS_EPS = 1e-6
CB_Q_HG, CB_F_FW, CB_F_BW, CB_I_HG, CB_G_HG, CB_Q_DA, CB_K_DA, CB_V_DA = (8 * i for i in range(8))

LANES = 128
SUBLANES = 8
VMEM_LIMIT = 56 * 1024 * 1024

ADA_TN = 1536
IN_TN = 512
OUT_TM = 256
HG_CHUNK = 64
HG_GROUP = 4
ATT_TQ = 256
ATT_TK = 512
MOE_TM = 256
CMB_TM = 256
NEG_INF = float("-inf")


def _cparams(sem):
    return pltpu.CompilerParams(dimension_semantics=sem, vmem_limit_bytes=VMEM_LIMIT)


def _silu(x):
    return x * jax.nn.sigmoid(x)


def _rms(x):
    return x * lax.rsqrt(jnp.mean(x * x, axis=-1, keepdims=True) + RMS_EPS)


def _ada_kernel(cond_ref, w_ref, b_ref, o_ref):
    s = _silu(cond_ref[...]).astype(BF16)
    o_ref[...] = jnp.dot(s, w_ref[...].astype(BF16), preferred_element_type=F32) + b_ref[...]


def _ada(cond, w, b):
    rows, d = cond.shape
    n = w.shape[1]
    return pl.pallas_call(
        _ada_kernel,
        out_shape=jax.ShapeDtypeStruct((rows, n), F32),
        grid=(n // ADA_TN,),
        in_specs=[pl.BlockSpec((rows, d), lambda j: (0, 0)),
                  pl.BlockSpec((d, ADA_TN), lambda j: (0, j)),
                  pl.BlockSpec((1, ADA_TN), lambda j: (0, j))],
        out_specs=pl.BlockSpec((rows, ADA_TN), lambda j: (0, j)),
        compiler_params=_cparams(("arbitrary",)),
        name="ada",
    )(cond, w, b.reshape(1, n))


def _in_kernel(x_ref, mod_ref, g_ref, w_ref, o_ref, h_ref):
    @pl.when(pl.program_id(1) == 0)
    def _():
        y = _rms(x_ref[...]) * g_ref[...]
        h_ref[...] = (y * (1.0 + mod_ref[0, 1:2, :]) + mod_ref[0, 0:1, :]).astype(BF16)

    o_ref[...] = jnp.dot(h_ref[...], w_ref[...], preferred_element_type=F32)


def _in_proj(x, mods, mod_row0, g, w_bf16, tm, name):
    batch, t, d = x.shape
    cols = w_bf16.shape[1]
    nt = t // tm
    return pl.pallas_call(
        _in_kernel,
        out_shape=jax.ShapeDtypeStruct((batch * t, cols), F32),
        grid=(batch * nt, cols // IN_TN),
        in_specs=[pl.BlockSpec((tm, d), lambda i, j: (i, 0)),
                  pl.BlockSpec((1, N_ADA, d), lambda i, j: (mod_row0 + i // nt, 0, 0)),
                  pl.BlockSpec((1, d), lambda i, j: (0, 0)),
                  pl.BlockSpec((d, IN_TN), lambda i, j: (0, j))],
        out_specs=pl.BlockSpec((tm, IN_TN), lambda i, j: (i, j)),
        scratch_shapes=[pltpu.VMEM((tm, d), BF16)],
        compiler_params=_cparams(("arbitrary", "arbitrary")),
        name=name,
    )(x.reshape(batch * t, d), mods, g.reshape(1, d), w_bf16)


def _rope(x, cos, sin_signed):
    lane = lax.broadcasted_iota(I32, x.shape, 1)
    first = (lane % 32) < 16
    partner = jnp.where(first, pltpu.roll(x, LANES - 16, 1), pltpu.roll(x, 16, 1))
    return x * cos + partner * sin_signed


def _split_maps(k):
    lane = lax.broadcasted_iota(I32, k.shape, 1)
    m1 = lane < DA_DK
    return jnp.where(m1, k, 0.0).astype(BF16), jnp.where(m1, 0.0, k).astype(BF16)


def _lower_bound(lb_ref):
    p = lb_ref[...]
    e = jnp.exp(p - jnp.max(p, axis=0, keepdims=True))
    return e[0:1, :] / jnp.sum(e, axis=0, keepdims=True)


def _hgrn_pair_masks(mask_ref, c):
    ti = lax.broadcasted_iota(I32, (c, c), 0)
    si = lax.broadcasted_iota(I32, (c, c), 1)
    for d in range(2):
        mask_ref[d, 0] = (ti == si).astype(F32)
        for j in range(c.bit_length() - 1):
            same = (ti >> (j + 1)) == (si >> (j + 1))
            t_bit = ((ti >> j) & 1) == 1
            s_bit = ((si >> j) & 1) == 1
            pair = (s_bit & jnp.logical_not(t_bit)) if d else (t_bit & jnp.logical_not(s_bit))
            mask_ref[d, 1 + j] = (same & pair).astype(F32)


def _hgrn_chunk(q, f_logit, v, lb, mask_ref, rev):
    c = q.shape[0]
    n_piece = c // SUBLANES
    levels = c.bit_length() - 1
    low = SUBLANES.bit_length() - 1
    d = 1 if rev else 0
    nt = (((1,), (1,)), ((), ()))
    pieces = lambda x: [x[SUBLANES * i:SUBLANES * (i + 1)] for i in range(n_piece)]
    whole = lambda xs: jnp.concatenate(xs, axis=0)

    fg = lb + (1.0 - lb) * jax.nn.sigmoid(f_logit)
    k = 1.0 - fg
    q_bf = q.astype(BF16)
    k_bf = k.astype(BF16)
    qs, ks, fgs = pieces(q), pieces(k), pieces(fg)
    tot = pieces(jnp.log(fg))
    pre = list(tot)
    suf = [jnp.zeros((SUBLANES, LANES), F32)] * n_piece
    sub = lax.broadcasted_iota(I32, (SUBLANES, LANES), 0)

    def pair_scores(qt, kt, idx):
        return lax.dot_general(qt, kt, nt, preferred_element_type=F32) * mask_ref[d, idx]

    scores = pair_scores(q_bf, k_bf, 0)
    yield
    for j in range(levels):
        if j < low:
            sh = 1 << j
            bit = ((sub >> j) & 1) == 1
            late = jnp.logical_not(bit) if rev else bit
            if j == 0:
                qt = whole([qs[i] * jnp.where(late, fgs[i], 1.0) for i in range(n_piece)]).astype(BF16)
                kt = k_bf
            else:
                es = [jnp.exp(jnp.where(late, pre[i], suf[i])) for i in range(n_piece)]
                qt = whole([qs[i] * es[i] for i in range(n_piece)]).astype(BF16)
                kt = whole([ks[i] * es[i] for i in range(n_piece)]).astype(BF16)
            for i in range(n_piece):
                up = pltpu.roll(tot[i], sh, 0)
                dn = pltpu.roll(tot[i], SUBLANES - sh, 0)
                sib = jnp.where(late, dn, up) if rev else jnp.where(late, up, dn)
                pre[i] = pre[i] + jnp.where(late, sib, 0.0)
                suf[i] = suf[i] + jnp.where(late, 0.0, sib)
                tot[i] = tot[i] + sib
        else:
            half = 1 << (j - low)
            upper = [(i // half) % 2 == 1 for i in range(n_piece)]
            late = [(not u) if rev else u for u in upper]
            es = [jnp.exp(pre[i] if late[i] else suf[i]) for i in range(n_piece)]
            qt = whole([qs[i] * es[i] for i in range(n_piece)]).astype(BF16)
            kt = whole([ks[i] * es[i] for i in range(n_piece)]).astype(BF16)
            sib = [tot[i - half] if upper[i] else tot[i + half] for i in range(n_piece)]
            pre = [pre[i] + sib[i] if late[i] else pre[i] for i in range(n_piece)]
            suf = [suf[i] if late[i] else suf[i] + sib[i] for i in range(n_piece)]
            tot = [tot[i] + sib[i] for i in range(n_piece)]
        scores = scores + pair_scores(qt, kt, 1 + j)
        yield

    q_dec = whole([qs[i] * jnp.exp(pre[i]) for i in range(n_piece)]).astype(BF16)
    k_dec = whole([ks[i] * jnp.exp(suf[i]) for i in range(n_piece)]).astype(BF16)
    return dict(q_dec=q_dec, k_dec=k_dec, v_t=v.T.astype(BF16), v=v.astype(BF16),
                scores=scores.astype(BF16), decay=jnp.exp(tot[0][0:1, :]))


def _hgrn_state_step(p, st_ref):
    st = st_ref[...]
    o = lax.dot_general(p["q_dec"], st.astype(BF16), (((1,), (1,)), ((), ())), preferred_element_type=F32)
    st_ref[...] = p["decay"] * st + jnp.dot(p["v_t"], p["k_dec"], preferred_element_type=F32)
    return o


def _lockstep(gens):
    results = [None] * len(gens)
    live = list(range(len(gens)))
    while live:
        for idx in list(live):
            try:
                next(gens[idx])
            except StopIteration as stop:
                results[idx] = stop.value
                live.remove(idx)
    return results


def _hgrn_kernel(*refs, t, has_init, emit_state):
    q_ref, ff_ref, fb_ref, i_ref, g_ref, lbf_ref, lbb_ref, ng_ref = refs[:8]
    pos = 8
    if has_init:
        s0f_ref, s0b_ref = refs[pos:pos + 2]
        pos += 2
    o_ref = refs[pos]
    pos += 1
    if emit_state:
        sf_ref, sb_ref = refs[pos:pos + 2]
        pos += 2
    of_scr, ob_scr, stf, stb, mask_scr = refs[pos:pos + 5]

    c = HG_CHUNK
    n = t // c
    _hgrn_pair_masks(mask_scr, c)
    if has_init:
        stf[...] = s0f_ref[0, 0, 0].T
        stb[...] = s0b_ref[0, 0, 0].T
    else:
        stf[...] = jnp.zeros_like(stf)
        stb[...] = jnp.zeros_like(stb)
    lbf = _lower_bound(lbf_ref)
    lbb = _lower_bound(lbb_ref)

    def body(i, carry):
        work = []
        for u in range(HG_GROUP):
            ci = i * HG_GROUP + u
            work.append((pl.ds(pl.multiple_of(ci * c, c), c), ff_ref, lbf, stf, of_scr, False))
            work.append((pl.ds(pl.multiple_of((n - 1 - ci) * c, c), c), fb_ref, lbb, stb, ob_scr, True))
        parts = _lockstep([_hgrn_chunk(_silu(q_ref[sl, :]), f_ref[sl, :], i_ref[sl, :], lb, mask_scr, rev)
                           for sl, f_ref, lb, _, _, rev in work])
        outs = [_hgrn_state_step(p, w[3]) for p, w in zip(parts, work)]
        for p, w, o in zip(parts, work, outs):
            w[4][w[0], :] = o + jnp.dot(p["scores"], p["v"], preferred_element_type=F32)
        return carry

    lax.fori_loop(0, n // HG_GROUP, body, 0)

    fin = min(t, 256)

    def finish(i, carry):
        sl = pl.ds(pl.multiple_of(i * fin, fin), fin)
        o = _rms(of_scr[sl, :] + ob_scr[sl, :]) * ng_ref[...]
        o_ref[sl, :] = (o * _silu(g_ref[sl, :])).astype(BF16)
        return carry

    lax.fori_loop(0, t // fin, finish, 0)
    if emit_state:
        sf_ref[0, 0, 0] = stf[...].T
        sb_ref[0, 0, 0] = stb[...].T


def _hgrn(proj, batch, t, lb_fwd, lb_bwd, norm_g, s0f=None, s0b=None, emit_state=False):
    has_init = s0f is not None

    def col(cb):
        return pl.BlockSpec((t, HEAD_W), lambda b, h: (b, cb + h))

    lb_spec = pl.BlockSpec((lb_fwd.shape[0], HEAD_W), lambda b, h: (0, h))
    in_specs = [col(CB_Q_HG), col(CB_F_FW), col(CB_F_BW), col(CB_I_HG), col(CB_G_HG),
                lb_spec, lb_spec, pl.BlockSpec((1, HEAD_W), lambda b, h: (0, 0))]
    args = [proj] * 5 + [lb_fwd, lb_bwd, norm_g.reshape(1, HEAD_W)]
    st_spec = pl.BlockSpec((1, 1, 1, HG_DK, HEAD_W), lambda b, h: (b, 0, h, 0, 0))
    if has_init:
        in_specs += [st_spec, st_spec]
        args += [s0f, s0b]
    out_shape = [jax.ShapeDtypeStruct((batch * t, HG_WIDTH), BF16)]
    out_specs = [pl.BlockSpec((t, HEAD_W), lambda b, h: (b, h))]
    if emit_state:
        st = jax.ShapeDtypeStruct((batch, 1, HG_HEADS, HG_DK, HEAD_W), F32)
        out_shape += [st, st]
        out_specs += [st_spec, st_spec]
    return pl.pallas_call(
        functools.partial(_hgrn_kernel, t=t, has_init=has_init, emit_state=emit_state),
        out_shape=out_shape,
        grid=(batch, HG_HEADS),
        in_specs=in_specs,
        out_specs=out_specs,
        scratch_shapes=[pltpu.VMEM((t, HEAD_W), F32), pltpu.VMEM((t, HEAD_W), F32),
                        pltpu.VMEM((HEAD_W, HG_DK), F32), pltpu.VMEM((HEAD_W, HG_DK), F32),
                        pltpu.VMEM((2, HG_CHUNK.bit_length(), HG_CHUNK, HG_CHUNK), F32)],
        compiler_params=_cparams(("arbitrary", "arbitrary")),
        name="hgrn_lat" if has_init else "hgrn_ctx",
    )(*args)


def _attn_kernel(*refs, t, n_cache, use_rope, emit_cache, lam_init):
    it = iter(refs)
    q_ref, k_ref, v_ref = next(it), next(it), next(it)
    if use_rope:
        cosq_ref, sinq_ref, cos_ref, sin_ref = next(it), next(it), next(it), next(it)
    if n_cache:
        ck_ref, cv_ref = next(it), next(it)
    lq1_ref, lk1_ref, lq2_ref, lk2_ref, ng_ref, o_ref = (next(it) for _ in range(6))
    if emit_cache:
        nk_ref, nv_ref = next(it), next(it)
    k1_scr, k2_scr, v_scr, s_scr, p_scr = (next(it) for _ in range(5))
    k_scrs = (k1_scr, k2_scr)
    tq = q_ref.shape[0]

    @pl.when(pl.program_id(2) == 0)
    def _():
        rows = min(t, ATT_TK)

        def stage(i, carry):
            sl = pl.ds(pl.multiple_of(i * rows, rows), rows)
            k = k_ref[sl, :]
            v = v_ref[sl, :]
            if emit_cache:
                nk_ref[0, 0, 0, sl, :] = k
                nv_ref[0, 0, 0, sl, :] = v
            if use_rope:
                k = _rope(k, cos_ref[sl, :], sin_ref[sl, :])
            k1_scr[sl, :], k2_scr[sl, :] = _split_maps(k)
            v_scr[sl, :] = v.astype(BF16)
            return carry

        lax.fori_loop(0, t // rows, stage, 0)
        if n_cache:
            k1_scr[t:t + n_cache, :], k2_scr[t:t + n_cache, :] = _split_maps(ck_ref[0, 0, 0])
            v_scr[t:t + n_cache, :] = cv_ref[0, 0, 0].astype(BF16)

    q = q_ref[...]
    if use_rope:
        q = _rope(q, cosq_ref[...], sinq_ref[...])
    q = (q * (DA_DK ** -0.5)).astype(BF16)
    nt = (((1,), (1,)), ((), ()))
    n_keys = t + n_cache
    tiles = [(st, min(ATT_TK, n_keys - st)) for st in range(0, n_keys, ATT_TK)]

    def scores_tile(mp, st, sz, mx):
        s = lax.dot_general(q, k_scrs[mp][st:st + sz, :], nt, preferred_element_type=F32)
        s_scr[mp, :, st:st + sz] = s
        for j in range(sz // LANES):
            mx = jnp.maximum(mx, s[:, j * LANES:(j + 1) * LANES])
        return mx

    def exp_tile(mp, st, sz, m, part):
        for lo in range(st, st + sz, LANES):
            e = jnp.exp(s_scr[mp, :, lo:lo + LANES] - m)
            part = part + e
            p_scr[mp, :, lo:lo + LANES] = e.astype(BF16)
        return part

    def value_tile(mp, st, sz, acc):
        return acc + jnp.dot(p_scr[mp, :, st:st + sz], v_scr[st:st + sz, :], preferred_element_type=F32)

    def row_stat(x, op):
        return jnp.broadcast_to(op(x, axis=-1, keepdims=True), (tq, LANES))

    neg = jnp.full((tq, LANES), NEG_INF, F32)
    zero = jnp.zeros((tq, LANES), F32)
    mx = neg
    for st, sz in tiles:
        mx = scores_tile(0, st, sz, mx)
    m0 = row_stat(mx, jnp.max)
    mx, part0 = neg, zero
    for st, sz in tiles:
        mx = scores_tile(1, st, sz, mx)
        part0 = exp_tile(0, st, sz, m0, part0)
    m1 = row_stat(mx, jnp.max)
    acc0, part1 = zero, zero
    for st, sz in tiles:
        acc0 = value_tile(0, st, sz, acc0)
        part1 = exp_tile(1, st, sz, m1, part1)
    acc1 = jnp.dot(p_scr[1], v_scr[...], preferred_element_type=F32)

    lam = (jnp.exp(jnp.sum(lq1_ref[...] * lk1_ref[...], axis=-1, keepdims=True))
           - jnp.exp(jnp.sum(lq2_ref[...] * lk2_ref[...], axis=-1, keepdims=True)) + lam_init)
    o = acc0 / row_stat(part0, jnp.sum) - lam * (acc1 / row_stat(part1, jnp.sum))
    o_ref[...] = (_rms(o) * ng_ref[...] * (1.0 - lam_init)).astype(BF16)


def _attn(proj, batch, t, lam_params, norm_g, lam_init, cos=None, sin=None, cache_k=None, cache_v=None,
          emit_cache=False):
    use_rope = cos is not None
    n_cache = 0 if cache_k is None else cache_k.shape[3]
    tq = min(ATT_TQ, t)
    nq = t // tq
    n_keys = t + n_cache

    def col(cb, rows, row_map):
        return pl.BlockSpec((rows, HEAD_W), lambda b, h, i: (row_map(b, i), cb + h))

    in_specs = [col(CB_Q_DA, tq, lambda b, i: b * nq + i),
                col(CB_K_DA, t, lambda b, i: b), col(CB_V_DA, t, lambda b, i: b)]
    args = [proj, proj, proj]
    if use_rope:
        in_specs += [pl.BlockSpec((tq, HEAD_W), lambda b, h, i: (i, 0))] * 2
        in_specs += [pl.BlockSpec((t, HEAD_W), lambda b, h, i: (0, 0))] * 2
        args += [cos, sin, cos, sin]
    cache_spec = lambda n: pl.BlockSpec((1, 1, 1, n, HEAD_W), lambda b, h, i: (b, 0, h, 0, 0))
    if n_cache:
        in_specs += [cache_spec(n_cache)] * 2
        args += [cache_k, cache_v]
    small = pl.BlockSpec((1, DA_DK), lambda b, h, i: (0, 0))
    in_specs += [small] * 4 + [pl.BlockSpec((1, HEAD_W), lambda b, h, i: (0, 0))]
    args += [p.reshape(1, DA_DK) for p in lam_params] + [norm_g.reshape(1, HEAD_W)]
    out_shape = [jax.ShapeDtypeStruct((batch * t, DA_HEADS * HEAD_W), BF16)]
    out_specs = [pl.BlockSpec((tq, HEAD_W), lambda b, h, i: (b * nq + i, h))]
    if emit_cache:
        out_shape += [jax.ShapeDtypeStruct((batch, 1, DA_HEADS, t, HEAD_W), F32)] * 2
        out_specs += [cache_spec(t)] * 2
    scratch = [pltpu.VMEM((n_keys, HEAD_W), BF16)] * 3
    scratch += [pltpu.VMEM((2, tq, n_keys), F32), pltpu.VMEM((2, tq, n_keys), BF16)]
    return pl.pallas_call(
        functools.partial(_attn_kernel, t=t, n_cache=n_cache, use_rope=use_rope, emit_cache=emit_cache,
                          lam_init=lam_init),
        out_shape=out_shape,
        grid=(batch, DA_HEADS, nq),
        in_specs=in_specs,
        out_specs=out_specs,
        scratch_shapes=scratch,
        compiler_params=_cparams(("arbitrary",) * 3),
        name="attn_lat" if use_rope else "attn_ctx",
    )(*args)


def _out_kernel(mhg_c, mda_c, x_c, mhg_l, mda_l, x_l, w_ref, mod_ref, g_ref, rw_ref, rb_ref,
                x1_ref, h2_ref, ids_ref, wts_ref, *, ctx_tiles):
    def body(mhg_ref, mda_ref, x_ref):
        mix = jnp.dot(mhg_ref[...], w_ref[0:HG_WIDTH, :], preferred_element_type=F32)
        mix = mix + jnp.dot(mda_ref[...], w_ref[HG_WIDTH:, :], preferred_element_type=F32)
        x1 = x_ref[...] + mod_ref[0, 2:3, :] * mix
        x1_ref[...] = x1
        h2 = (_rms(x1) * g_ref[...]) * (1.0 + mod_ref[0, 4:5, :]) + mod_ref[0, 3:4, :]
        h2_ref[...] = h2

        h_hi = h2.astype(BF16)
        h_lo = (h2 - h_hi.astype(F32)).astype(BF16)
        logit = (jnp.dot(h_hi, rw_ref[0], preferred_element_type=F32)
                 + jnp.dot(h_lo, rw_ref[0], preferred_element_type=F32)
                 + jnp.dot(h_hi, rw_ref[1], preferred_element_type=F32)) + rb_ref[...]
        lane = lax.broadcasted_iota(I32, logit.shape, 1)

        def first_max(x):
            m = jnp.max(x, axis=-1, keepdims=True)
            return m, jnp.min(jnp.where(x == m, lane, LANES), axis=-1, keepdims=True)

        gmask = lane < N_GROUPS
        gmax, gsel = first_max(jnp.where(gmask, logit, NEG_INF))
        p_grp = 1.0 / jnp.sum(jnp.where(gmask, jnp.exp(logit - gmax), 0.0), axis=-1, keepdims=True)
        lo = N_GROUPS + EXP_PER_GROUP * gsel
        le = jnp.where((lane >= lo) & (lane < lo + EXP_PER_GROUP), logit, NEG_INF)
        v1, i1 = first_max(le)
        v2, i2 = first_max(jnp.where(lane == i1, NEG_INF, le))
        e = jnp.exp(v2 - v1)
        w1 = p_grp / (1.0 + e)
        w2 = p_grp * e / (1.0 + e)
        ids_ref[...] = jnp.where(lane == 0, i1 - N_GROUPS, jnp.where(lane == 1, i2 - N_GROUPS, 0))
        wts_ref[...] = jnp.where(lane == 0, w1, jnp.where(lane == 1, w2, 0.0))

    i = pl.program_id(0)
    pl.when(i < ctx_tiles)(lambda: body(mhg_c, mda_c, x_c))
    pl.when(i >= ctx_tiles)(lambda: body(mhg_l, mda_l, x_l))


def _out_proj(mhg_c, mda_c, x_c, mhg_l, mda_l, x_l, w_bf16, mods, g, rw, rb, lat_t):
    n_ctx, d = x_c.shape
    n_lat = x_l.shape[0]
    n = n_ctx + n_lat
    ctx_tiles = n_ctx // OUT_TM
    lat_tiles = lat_t // OUT_TM
    row = lambda i: (i, 0)
    const = lambda i: (0, 0)
    crow = lambda i: (jnp.minimum(i, ctx_tiles - 1), 0)
    lrow = lambda i: (jnp.maximum(i - ctx_tiles, 0), 0)
    seg = lambda i: (jnp.where(i < ctx_tiles, 0, 1 + (i - ctx_tiles) // lat_tiles), 0, 0)
    half = lambda m: pl.BlockSpec((OUT_TM, HG_WIDTH), m)
    full = lambda m: pl.BlockSpec((OUT_TM, d), m)
    return pl.pallas_call(
        functools.partial(_out_kernel, ctx_tiles=ctx_tiles),
        out_shape=[jax.ShapeDtypeStruct((n, d), F32), jax.ShapeDtypeStruct((n, d), F32),
                   jax.ShapeDtypeStruct((n, LANES), I32), jax.ShapeDtypeStruct((n, LANES), F32)],
        grid=(n // OUT_TM,),
        in_specs=[half(crow), half(crow), full(crow), half(lrow), half(lrow), full(lrow),
                  pl.BlockSpec((d, d), const), pl.BlockSpec((1, N_ADA, d), seg), pl.BlockSpec((1, d), const),
                  pl.BlockSpec((2, d, LANES), lambda i: (0, 0, 0)), pl.BlockSpec((1, LANES), const)],
        out_specs=[full(row), full(row), pl.BlockSpec((OUT_TM, LANES), row), pl.BlockSpec((OUT_TM, LANES), row)],
        compiler_params=_cparams(("arbitrary",)),
        name="out_proj",
    )(mhg_c, mda_c, x_c, mhg_l, mda_l, x_l, w_bf16, mods, g.reshape(1, d), rw, rb)


def _dispatch_kernel(src_ref, ntile_ref, h_hbm, o_ref, x0, x1, sems, *, n_tok):
    i = pl.program_id(0)
    n_tiles = ntile_ref[0]
    xs = (x0, x1)

    def gather_copy(tile, r, slot):
        p = jnp.maximum(src_ref[tile * MOE_TM + r], 0)
        tok = jnp.where(p >= n_tok, p - n_tok, p)
        return pltpu.make_async_copy(h_hbm.at[pl.ds(tok, 1)], xs[slot].at[pl.ds(r, 1)], sems.at[slot])

    def wait_gather(slot):
        pltpu.make_async_copy(h_hbm.at[pl.ds(0, MOE_TM)], xs[slot], sems.at[slot]).wait()

    @pl.when(i == 0)
    def _():
        for r in range(MOE_TM):
            gather_copy(0, r, 0).start()

    def step(cur):
        nxt = 1 - cur
        wait_gather(cur)
        for r in range(MOE_TM):
            gather_copy(i + 1, r, nxt).start()
        o_ref[...] = xs[cur][...].astype(BF16)

        @pl.when(i == n_tiles - 1)
        def _():
            wait_gather(nxt)

    live = i < n_tiles
    pl.when(live & (i % 2 == 0))(lambda: step(0))
    pl.when(live & (i % 2 == 1))(lambda: step(1))

    @pl.when(jnp.logical_not(live))
    def _():
        o_ref[...] = jnp.zeros_like(o_ref)


def _dispatch(src, n_tiles, h2, steps, n_tok):
    d = h2.shape[1]
    return pl.pallas_call(
        functools.partial(_dispatch_kernel, n_tok=n_tok),
        out_shape=jax.ShapeDtypeStruct((steps * MOE_TM, d), BF16),
        grid_spec=pltpu.PrefetchScalarGridSpec(
            num_scalar_prefetch=2,
            grid=(steps,),
            in_specs=[pl.BlockSpec(memory_space=pl.ANY)],
            out_specs=pl.BlockSpec((MOE_TM, d), lambda i, src, nt: (i, 0)),
            scratch_shapes=[pltpu.VMEM((MOE_TM, d), F32)] * 2 + [pltpu.SemaphoreType.DMA((2,))],
        ),
        compiler_params=_cparams(("arbitrary",)),
        name="dispatch",
    )(src, n_tiles, h2)


def _gmm_kernel(texp_ref, ntile_ref, x_ref, w1_ref, w3_ref, w2_ref, y_ref):
    live = pl.program_id(0) < ntile_ref[0]

    @pl.when(live)
    def _():
        x = x_ref[...]
        a = jnp.dot(x, w1_ref[0].astype(BF16), preferred_element_type=F32)
        b = jnp.dot(x, w3_ref[0].astype(BF16), preferred_element_type=F32)
        hid = (_silu(a) * b).astype(BF16)
        y_ref[...] = jnp.dot(hid, w2_ref[0].astype(BF16), preferred_element_type=F32)

    @pl.when(jnp.logical_not(live))
    def _():
        y_ref[...] = jnp.zeros_like(y_ref)


def _gmm(tile_expert, n_tiles, x_sorted, w1, w3, w2):
    d = x_sorted.shape[1]
    steps = tile_expert.shape[0]
    wspec = lambda shape: pl.BlockSpec((1,) + shape, lambda i, te, nt: (te[i], 0, 0))
    return pl.pallas_call(
        _gmm_kernel,
        out_shape=jax.ShapeDtypeStruct((steps * MOE_TM, d), F32),
        grid_spec=pltpu.PrefetchScalarGridSpec(
            num_scalar_prefetch=2,
            grid=(steps,),
            in_specs=[pl.BlockSpec((MOE_TM, d), lambda i, te, nt: (i, 0)),
                      wspec((d, D_EXPERT)), wspec((d, D_EXPERT)), wspec((D_EXPERT, d))],
            out_specs=pl.BlockSpec((MOE_TM, d), lambda i, te, nt: (i, 0)),
        ),
        compiler_params=_cparams(("arbitrary",)),
        name="gmm",
    )(tile_expert, n_tiles, x_sorted, w1, w3, w2)


def _route_tables(ids, n_tok):
    n_pairs = n_tok * TOP_K
    max_tiles = n_pairs // MOE_TM + N_EXPERTS
    e_flat = ids[:, :TOP_K].T.reshape(n_pairs)
    onehot = (e_flat[:, None] == jnp.arange(N_EXPERTS, dtype=I32)[None, :]).astype(I32)
    csum = jnp.cumsum(onehot, axis=0)
    counts = csum[-1]
    rank = jnp.sum(csum * onehot, axis=1) - 1
    tiles_per = (counts + MOE_TM - 1) // MOE_TM
    tile_end = jnp.cumsum(tiles_per)
    row_start = (tile_end - tiles_per) * MOE_TM
    pos = row_start[e_flat] + rank
    src = jnp.full(((max_tiles + 1) * MOE_TM,), -1, I32).at[pos].set(jnp.arange(n_pairs, dtype=I32))
    steps = jnp.arange(max_tiles, dtype=I32)
    owner = jnp.sum((tile_end[None, :] <= steps[:, None]).astype(I32), axis=1)
    tile_expert = jnp.minimum(owner, N_EXPERTS - 1)
    return src, pos, tile_expert, tile_end[-1:].astype(I32)


def _combine_kernel(pos_ref, x1_ref, wts_ref, mod_ref, g_ref, y_hbm, o_ref, ya0, yb0, ya1, yb1, sems,
                    *, tok0, n_tok, n_steps):
    i = pl.program_id(0)
    bufs = ((ya0, yb0), (ya1, yb1))

    def row_copy(tile, r, slot, which):
        row = pos_ref[which * n_tok + tok0 + tile * CMB_TM + r]
        return pltpu.make_async_copy(y_hbm.at[pl.ds(row, 1)], bufs[slot][which].at[pl.ds(r, 1)],
                                     sems.at[2 * slot + which])

    def issue(tile, slot):
        for r in range(CMB_TM):
            row_copy(tile, r, slot, 0).start()
            row_copy(tile, r, slot, 1).start()

    def wait(slot):
        for which in range(2):
            pltpu.make_async_copy(y_hbm.at[pl.ds(0, CMB_TM)], bufs[slot][which], sems.at[2 * slot + which]).wait()

    @pl.when(i == 0)
    def _():
        issue(0, 0)

    def step(cur):
        nxt = 1 - cur
        wait(cur)
        issue(jnp.minimum(i + 1, n_steps - 1), nxt)
        w = wts_ref[...]
        moe = w[:, 0:1] * bufs[cur][0][...] + w[:, 1:2] * bufs[cur][1][...]
        x2 = x1_ref[...] + mod_ref[0, 5:6, :] * moe
        o_ref[...] = _rms(x2) * g_ref[...]

        @pl.when(i == n_steps - 1)
        def _():
            wait(nxt)

    pl.when(i % 2 == 0)(lambda: step(0))
    pl.when(i % 2 == 1)(lambda: step(1))


def _combine(pos, x1, y_sorted, wts, mods, mod_row0, g, row0, batch, t, n_tok):
    d = x1.shape[1]
    nt = t // CMB_TM
    rb0 = row0 // CMB_TM
    n_steps = batch * nt
    return pl.pallas_call(
        functools.partial(_combine_kernel, tok0=row0, n_tok=n_tok, n_steps=n_steps),
        out_shape=jax.ShapeDtypeStruct((batch * t, d), F32),
        grid_spec=pltpu.PrefetchScalarGridSpec(
            num_scalar_prefetch=1,
            grid=(n_steps,),
            in_specs=[pl.BlockSpec((CMB_TM, d), lambda i, pos: (rb0 + i, 0)),
                      pl.BlockSpec((CMB_TM, LANES), lambda i, pos: (rb0 + i, 0)),
                      pl.BlockSpec((1, N_ADA, d), lambda i, pos: (mod_row0 + i // nt, 0, 0)),
                      pl.BlockSpec((1, d), lambda i, pos: (0, 0)),
                      pl.BlockSpec(memory_space=pl.ANY)],
            out_specs=pl.BlockSpec((CMB_TM, d), lambda i, pos: (i, 0)),
            scratch_shapes=[pltpu.VMEM((CMB_TM, d), F32)] * 4 + [pltpu.SemaphoreType.DMA((4,))],
        ),
        compiler_params=_cparams(("arbitrary",)),
        name="combine",
    )(pos, x1, wts, mods, g.reshape(1, d), y_sorted)


def _rope_tables(n_lat):
    half = DA_DK // 2
    nf = half // 2
    t = jnp.arange(n_lat)
    row = (t // GRID_W).astype(F32)
    col = (t % GRID_W).astype(F32)
    inv = ROPE_THETA ** (-jnp.arange(nf, dtype=F32) / nf)
    lane = np.arange(HEAD_W)
    freq = inv[lane % nf]
    pos = jnp.where(((lane // half) % 2 == 0)[None, :], row[:, None], col[:, None])
    ang = pos * freq[None, :]
    sign = np.where((lane % half) < nf, -1.0, 1.0).astype(np.float32)
    return jnp.cos(ang), jnp.sin(ang) * sign[None, :]


def kernel(x_prompt, x_sample, cache_k, cache_v, state_hgrn_fwd, state_hgrn_bwd, c, c_ctx, w_ada, b_ada,
           norm1_g, norm2_g, norm_final_g, w_in, hg_lb_fwd, hg_lb_bwd, hg_norm_g, da_lambda_q1, da_lambda_k1,
           da_lambda_q2, da_lambda_k2, da_norm_g, w_out, router_g_w, router_g_b, router_e_w, router_e_b,
           exp_w1, exp_w3, exp_w2):
    l = 0
    batch, seq, d = x_prompt.shape
    dec_batch, dec_seq, _ = x_sample.shape
    n_ctx = batch * seq
    n_lat = dec_batch * dec_seq
    n_tok = n_ctx + n_lat
    lam_init = 0.8 - 0.6 * math.exp(-0.3 * l)

    cond = jnp.zeros((8, d), F32).at[0].set(c_ctx).at[1:1 + dec_batch].set(c)
    mods = _ada(cond, w_ada[l], b_ada[l]).reshape(8, N_ADA, d)

    w_in_bf16 = w_in[l].astype(BF16)
    proj_c = _in_proj(x_prompt.reshape(1, n_ctx, d), mods, 0, norm1_g[l], w_in_bf16, 1024, "in_proj_ctx")
    proj_l = _in_proj(x_sample, mods, 1, norm1_g[l], w_in_bf16, 1024, "in_proj_lat")
    lam_params = (da_lambda_q1[l], da_lambda_k1[l], da_lambda_q2[l], da_lambda_k2[l])

    da_ctx, new_k, new_v = _attn(proj_c, batch, seq, lam_params, da_norm_g[l], lam_init, emit_cache=True)
    hg_ctx, new_sf, new_sb = _hgrn(proj_c, batch, seq, hg_lb_fwd, hg_lb_bwd, hg_norm_g[l], emit_state=True)

    cos, sin = _rope_tables(dec_seq)
    (da_lat,) = _attn(proj_l, dec_batch, dec_seq, lam_params, da_norm_g[l], lam_init, cos=cos, sin=sin,
                      cache_k=cache_k[:, l:l + 1], cache_v=cache_v[:, l:l + 1])
    (hg_lat,) = _hgrn(proj_l, dec_batch, dec_seq, hg_lb_fwd, hg_lb_bwd, hg_norm_g[l],
                      s0f=state_hgrn_fwd[:, l:l + 1], s0b=state_hgrn_bwd[:, l:l + 1])

    rw = jnp.zeros((d, LANES), F32).at[:, :N_GROUPS].set(router_g_w[l]).at[:, N_GROUPS:N_GROUPS + N_EXPERTS].set(
        router_e_w[l])
    rw_hi = rw.astype(BF16)
    rw = jnp.stack([rw_hi, (rw - rw_hi.astype(F32)).astype(BF16)])
    rb = jnp.zeros((1, LANES), F32).at[0, :N_GROUPS].set(router_g_b[l]).at[0, N_GROUPS:N_GROUPS + N_EXPERTS].set(
        router_e_b[l])
    x1, h2, ids, wts = _out_proj(hg_ctx, da_ctx, x_prompt.reshape(n_ctx, d), hg_lat, da_lat,
                                 x_sample.reshape(n_lat, d), w_out[l].astype(BF16), mods, norm2_g[l], rw, rb, dec_seq)

    src, pos, tile_expert, n_tiles = _route_tables(ids, n_tok)
    x_sorted = _dispatch(src, n_tiles, h2, tile_expert.shape[0], n_tok)
    y_sorted = _gmm(tile_expert, n_tiles, x_sorted, exp_w1[l], exp_w3[l], exp_w2[l])

    y_ctx = _combine(pos, x1, y_sorted, wts, mods, 0, norm_final_g, 0, 1, n_ctx, n_tok)
    y_lat = _combine(pos, x1, y_sorted, wts, mods, 1, norm_final_g, n_ctx, dec_batch, dec_seq, n_tok)
    return (y_ctx.reshape(batch, seq, d), y_lat.reshape(dec_batch, dec_seq, d), new_k, new_v, new_sf, new_sb)
```

```python
import functools
import math

import numpy as np
import jax
import jax.numpy as jnp
from jax import lax
from jax.experimental import pallas as pl
from jax.experimental.pallas import tpu as pltpu

F32 = jnp.float32
BF16 = jnp.bfloat16
I32 = jnp.int32

GRID_W = 64
HG_WIDTH = 1024
HG_DK = 128
HG_HEADS = 8
DA_HEADS = 8
DA_DK = 64
HEAD_W = 128
ROPE_THETA = 10000.0
N_GROUPS = 4
EXP_PER_GROUP = 8
N_EXPERTS = 32
TOP_K = 2
D_EXPERT = 512
N_ADA = 6
RMS_EPS = 1e-6
CB_Q_HG, CB_F_FW, CB_F_BW, CB_I_HG, CB_G_HG, CB_Q_DA, CB_K_DA, CB_V_DA = (8 * i for i in range(8))

LANES = 128
SUBLANES = 8
VMEM_LIMIT = 56 * 1024 * 1024

ADA_TN = 1536
IN_TN = 512
OUT_TM = 256
HG_CHUNK = 64
HG_GROUP = 4
ATT_TQ = 256
ATT_TK = 512
MOE_TM = 256
DSP_TM = 1024
CMB_TM = 512
DMA_UNROLL = 8
NEG_INF = float("-inf")


def _cparams(sem):
    return pltpu.CompilerParams(dimension_semantics=sem, vmem_limit_bytes=VMEM_LIMIT)


def _silu(x):
    return x * jax.nn.sigmoid(x)


def _rms(x):
    return x * lax.rsqrt(jnp.mean(x * x, axis=-1, keepdims=True) + RMS_EPS)


def _ada_kernel(cond_ref, w_ref, b_ref, o_ref):
    s = _silu(cond_ref[...]).astype(BF16)
    o_ref[...] = jnp.dot(s, w_ref[...].astype(BF16), preferred_element_type=F32) + b_ref[...]


def _ada(cond, w, b):
    rows, d = cond.shape
    n = w.shape[1]
    return pl.pallas_call(
        _ada_kernel,
        out_shape=jax.ShapeDtypeStruct((rows, n), F32),
        grid=(n // ADA_TN,),
        in_specs=[pl.BlockSpec((rows, d), lambda j: (0, 0)),
                  pl.BlockSpec((d, ADA_TN), lambda j: (0, j)),
                  pl.BlockSpec((1, ADA_TN), lambda j: (0, j))],
        out_specs=pl.BlockSpec((rows, ADA_TN), lambda j: (0, j)),
        compiler_params=_cparams(("arbitrary",)),
        name="ada",
    )(cond, w, b.reshape(1, n))


def _in_kernel(x_ref, mod_ref, g_ref, w_ref, o_ref, h_ref):
    @pl.when(pl.program_id(1) == 0)
    def _():
        y = _rms(x_ref[...]) * g_ref[...]
        h_ref[...] = (y * (1.0 + mod_ref[0, 1:2, :]) + mod_ref[0, 0:1, :]).astype(BF16)

    o_ref[...] = jnp.dot(h_ref[...], w_ref[...], preferred_element_type=F32)


def _in_proj(x, mods, mod_row0, g, w_bf16, tm, name):
    batch, t, d = x.shape
    cols = w_bf16.shape[1]
    nt = t // tm
    return pl.pallas_call(
        _in_kernel,
        out_shape=jax.ShapeDtypeStruct((batch * t, cols), F32),
        grid=(batch * nt, cols // IN_TN),
        in_specs=[pl.BlockSpec((tm, d), lambda i, j: (i, 0)),
                  pl.BlockSpec((1, N_ADA, d), lambda i, j: (mod_row0 + i // nt, 0, 0)),
                  pl.BlockSpec((1, d), lambda i, j: (0, 0)),
                  pl.BlockSpec((d, IN_TN), lambda i, j: (0, j))],
        out_specs=pl.BlockSpec((tm, IN_TN), lambda i, j: (i, j)),
        scratch_shapes=[pltpu.VMEM((tm, d), BF16)],
        compiler_params=_cparams(("arbitrary", "arbitrary")),
        name=name,
    )(x.reshape(batch * t, d), mods, g.reshape(1, d), w_bf16)


def _rope(x, cos, sin_signed):
    lane = lax.broadcasted_iota(I32, x.shape, 1)
    first = (lane % 32) < 16
    partner = jnp.where(first, pltpu.roll(x, LANES - 16, 1), pltpu.roll(x, 16, 1))
    return x * cos + partner * sin_signed


def _split_maps(k):
    lane = lax.broadcasted_iota(I32, k.shape, 1)
    m1 = lane < DA_DK
    return jnp.where(m1, k, 0.0).astype(BF16), jnp.where(m1, 0.0, k).astype(BF16)


def _lower_bound(lb_ref):
    p = lb_ref[...]
    e = jnp.exp(p - jnp.max(p, axis=0, keepdims=True))
    return e[0:1, :] / jnp.sum(e, axis=0, keepdims=True)


def _hgrn_pair_masks(mask_ref, c):
    ti = lax.broadcasted_iota(I32, (c, c), 0)
    si = lax.broadcasted_iota(I32, (c, c), 1)
    for d in range(2):
        mask_ref[d, 0] = (ti == si).astype(F32)
        for j in range(c.bit_length() - 1):
            same = (ti >> (j + 1)) == (si >> (j + 1))
            t_bit = ((ti >> j) & 1) == 1
            s_bit = ((si >> j) & 1) == 1
            pair = (s_bit & jnp.logical_not(t_bit)) if d else (t_bit & jnp.logical_not(s_bit))
            mask_ref[d, 1 + j] = (same & pair).astype(F32)


def _hgrn_chunk(q, f_logit, v, lb, mask_ref, rev):
    c = q.shape[0]
    n_piece = c // SUBLANES
    levels = c.bit_length() - 1
    low = SUBLANES.bit_length() - 1
    d = 1 if rev else 0
    nt = (((1,), (1,)), ((), ()))
    pieces = lambda x: [x[SUBLANES * i:SUBLANES * (i + 1)] for i in range(n_piece)]
    whole = lambda xs: jnp.concatenate(xs, axis=0)

    fg = lb + (1.0 - lb) * jax.nn.sigmoid(f_logit)
    k = 1.0 - fg
    q_bf = q.astype(BF16)
    k_bf = k.astype(BF16)
    qs, ks, fgs = pieces(q), pieces(k), pieces(fg)
    tot = pieces(jnp.log(fg))
    pre = list(tot)
    suf = [jnp.zeros((SUBLANES, LANES), F32)] * n_piece
    sub = lax.broadcasted_iota(I32, (SUBLANES, LANES), 0)

    def pair_scores(qt, kt, idx):
        return lax.dot_general(qt, kt, nt, preferred_element_type=F32) * mask_ref[d, idx]

    scores = pair_scores(q_bf, k_bf, 0)
    yield
    for j in range(levels):
        if j < low:
            sh = 1 << j
            bit = ((sub >> j) & 1) == 1
            late = jnp.logical_not(bit) if rev else bit
            if j == 0:
                qt = whole([qs[i] * jnp.where(late, fgs[i], 1.0) for i in range(n_piece)]).astype(BF16)
                kt = k_bf
            else:
                es = [jnp.exp(jnp.where(late, pre[i], suf[i])) for i in range(n_piece)]
                qt = whole([qs[i] * es[i] for i in range(n_piece)]).astype(BF16)
                kt = whole([ks[i] * es[i] for i in range(n_piece)]).astype(BF16)
            for i in range(n_piece):
                up = pltpu.roll(tot[i], sh, 0)
                dn = pltpu.roll(tot[i], SUBLANES - sh, 0)
                sib = jnp.where(late, dn, up) if rev else jnp.where(late, up, dn)
                pre[i] = pre[i] + jnp.where(late, sib, 0.0)
                suf[i] = suf[i] + jnp.where(late, 0.0, sib)
                tot[i] = tot[i] + sib
        else:
            half = 1 << (j - low)
            upper = [(i // half) % 2 == 1 for i in range(n_piece)]
            late = [(not u) if rev else u for u in upper]
            es = [jnp.exp(pre[i] if late[i] else suf[i]) for i in range(n_piece)]
            qt = whole([qs[i] * es[i] for i in range(n_piece)]).astype(BF16)
            kt = whole([ks[i] * es[i] for i in range(n_piece)]).astype(BF16)
            sib = [tot[i - half] if upper[i] else tot[i + half] for i in range(n_piece)]
            pre = [pre[i] + sib[i] if late[i] else pre[i] for i in range(n_piece)]
            suf = [suf[i] if late[i] else suf[i] + sib[i] for i in range(n_piece)]
            tot = [tot[i] + sib[i] for i in range(n_piece)]
        scores = scores + pair_scores(qt, kt, 1 + j)
        yield

    q_dec = whole([qs[i] * jnp.exp(pre[i]) for i in range(n_piece)]).astype(BF16)
    k_dec = whole([ks[i] * jnp.exp(suf[i]) for i in range(n_piece)]).astype(BF16)
    return dict(q_dec=q_dec, k_dec=k_dec, v_t=v.T.astype(BF16), v=v.astype(BF16),
                scores=scores.astype(BF16), decay=jnp.exp(tot[0][0:1, :]))


def _hgrn_state_step(p, st_ref):
    st = st_ref[...]
    o = lax.dot_general(p["q_dec"], st.astype(BF16), (((1,), (1,)), ((), ())), preferred_element_type=F32)
    st_ref[...] = p["decay"] * st + jnp.dot(p["v_t"], p["k_dec"], preferred_element_type=F32)
    return o


def _lockstep(gens):
    results = [None] * len(gens)
    live = list(range(len(gens)))
    while live:
        for idx in list(live):
            try:
                next(gens[idx])
            except StopIteration as stop:
                results[idx] = stop.value
                live.remove(idx)
    return results


def _hgrn_kernel(*refs, t, has_init, emit_state):
    q_ref, ff_ref, fb_ref, i_ref, g_ref, lbf_ref, lbb_ref, ng_ref = refs[:8]
    pos = 8
    if has_init:
        s0f_ref, s0b_ref = refs[pos:pos + 2]
        pos += 2
    o_ref = refs[pos]
    pos += 1
    if emit_state:
        sf_ref, sb_ref = refs[pos:pos + 2]
        pos += 2
    of_scr, ob_scr, stf, stb, mask_scr = refs[pos:pos + 5]

    c = HG_CHUNK
    n = t // c
    _hgrn_pair_masks(mask_scr, c)
    if has_init:
        stf[...] = s0f_ref[0, 0, 0].T
        stb[...] = s0b_ref[0, 0, 0].T
    else:
        stf[...] = jnp.zeros_like(stf)
        stb[...] = jnp.zeros_like(stb)
    lbf = _lower_bound(lbf_ref)
    lbb = _lower_bound(lbb_ref)

    def body(i, carry):
        work = []
        for u in range(HG_GROUP):
            ci = i * HG_GROUP + u
            work.append((pl.ds(pl.multiple_of(ci * c, c), c), ff_ref, lbf, stf, of_scr, False))
            work.append((pl.ds(pl.multiple_of((n - 1 - ci) * c, c), c), fb_ref, lbb, stb, ob_scr, True))
        parts = _lockstep([_hgrn_chunk(_silu(q_ref[sl, :]), f_ref[sl, :], i_ref[sl, :], lb, mask_scr, rev)
                           for sl, f_ref, lb, _, _, rev in work])
        outs = [_hgrn_state_step(p, w[3]) for p, w in zip(parts, work)]
        for p, w, o in zip(parts, work, outs):
            w[4][w[0], :] = o + jnp.dot(p["scores"], p["v"], preferred_element_type=F32)
        return carry

    lax.fori_loop(0, n // HG_GROUP, body, 0)

    fin = min(t, 256)

    def finish(i, carry):
        sl = pl.ds(pl.multiple_of(i * fin, fin), fin)
        o = _rms(of_scr[sl, :] + ob_scr[sl, :]) * ng_ref[...]
        o_ref[sl, :] = (o * _silu(g_ref[sl, :])).astype(BF16)
        return carry

    lax.fori_loop(0, t // fin, finish, 0)
    if emit_state:
        sf_ref[0, 0, 0] = stf[...].T
        sb_ref[0, 0, 0] = stb[...].T


def _hgrn(proj, batch, t, lb_fwd, lb_bwd, norm_g, s0f=None, s0b=None, emit_state=False):
    has_init = s0f is not None

    def col(cb):
        return pl.BlockSpec((t, HEAD_W), lambda b, h: (b, cb + h))

    lb_spec = pl.BlockSpec((lb_fwd.shape[0], HEAD_W), lambda b, h: (0, h))
    in_specs = [col(CB_Q_HG), col(CB_F_FW), col(CB_F_BW), col(CB_I_HG), col(CB_G_HG),
                lb_spec, lb_spec, pl.BlockSpec((1, HEAD_W), lambda b, h: (0, 0))]
    args = [proj] * 5 + [lb_fwd, lb_bwd, norm_g.reshape(1, HEAD_W)]
    st_spec = pl.BlockSpec((1, 1, 1, HG_DK, HEAD_W), lambda b, h: (b, 0, h, 0, 0))
    if has_init:
        in_specs += [st_spec, st_spec]
        args += [s0f, s0b]
    out_shape = [jax.ShapeDtypeStruct((batch * t, HG_WIDTH), BF16)]
    out_specs = [pl.BlockSpec((t, HEAD_W), lambda b, h: (b, h))]
    if emit_state:
        st = jax.ShapeDtypeStruct((batch, 1, HG_HEADS, HG_DK, HEAD_W), F32)
        out_shape += [st, st]
        out_specs += [st_spec, st_spec]
    return pl.pallas_call(
        functools.partial(_hgrn_kernel, t=t, has_init=has_init, emit_state=emit_state),
        out_shape=out_shape,
        grid=(batch, HG_HEADS),
        in_specs=in_specs,
        out_specs=out_specs,
        scratch_shapes=[pltpu.VMEM((t, HEAD_W), F32), pltpu.VMEM((t, HEAD_W), F32),
                        pltpu.VMEM((HEAD_W, HG_DK), F32), pltpu.VMEM((HEAD_W, HG_DK), F32),
                        pltpu.VMEM((2, HG_CHUNK.bit_length(), HG_CHUNK, HG_CHUNK), F32)],
        compiler_params=_cparams(("arbitrary", "arbitrary")),
        name="hgrn_lat" if has_init else "hgrn_ctx",
    )(*args)


def _attn_kernel(*refs, t, n_cache, use_rope, emit_cache, lam_init):
    it = iter(refs)
    q_ref, k_ref, v_ref = next(it), next(it), next(it)
    if use_rope:
        cosq_ref, sinq_ref, cos_ref, sin_ref = next(it), next(it), next(it), next(it)
    if n_cache:
        ck_ref, cv_ref = next(it), next(it)
    lq1_ref, lk1_ref, lq2_ref, lk2_ref, ng_ref, o_ref = (next(it) for _ in range(6))
    if emit_cache:
        nk_ref, nv_ref = next(it), next(it)
    k1_scr, k2_scr, v_scr, s_scr, p_scr = (next(it) for _ in range(5))
    k_scrs = (k1_scr, k2_scr)
    tq = q_ref.shape[0]

    @pl.when(pl.program_id(2) == 0)
    def _():
        rows = min(t, ATT_TK)

        def stage(i, carry):
            sl = pl.ds(pl.multiple_of(i * rows, rows), rows)
            k = k_ref[sl, :]
            v = v_ref[sl, :]
            if emit_cache:
                nk_ref[0, 0, 0, sl, :] = k
                nv_ref[0, 0, 0, sl, :] = v
            if use_rope:
                k = _rope(k, cos_ref[sl, :], sin_ref[sl, :])
            k1_scr[sl, :], k2_scr[sl, :] = _split_maps(k)
            v_scr[sl, :] = v.astype(BF16)
            return carry

        lax.fori_loop(0, t // rows, stage, 0)
        if n_cache:
            k1_scr[t:t + n_cache, :], k2_scr[t:t + n_cache, :] = _split_maps(ck_ref[0, 0, 0])
            v_scr[t:t + n_cache, :] = cv_ref[0, 0, 0].astype(BF16)

    q = q_ref[...]
    if use_rope:
        q = _rope(q, cosq_ref[...], sinq_ref[...])
    q = (q * (DA_DK ** -0.5)).astype(BF16)
    nt = (((1,), (1,)), ((), ()))
    n_keys = t + n_cache
    tiles = [(st, min(ATT_TK, n_keys - st)) for st in range(0, n_keys, ATT_TK)]

    def scores_tile(mp, st, sz, mx):
        s = lax.dot_general(q, k_scrs[mp][st:st + sz, :], nt, preferred_element_type=F32)
        s_scr[mp, :, st:st + sz] = s
        for j in range(sz // LANES):
            mx = jnp.maximum(mx, s[:, j * LANES:(j + 1) * LANES])
        return mx

    def exp_tile(mp, st, sz, m, part):
        for lo in range(st, st + sz, LANES):
            e = jnp.exp(s_scr[mp, :, lo:lo + LANES] - m)
            part = part + e
            p_scr[mp, :, lo:lo + LANES] = e.astype(BF16)
        return part

    def value_tile(mp, st, sz, acc):
        return acc + jnp.dot(p_scr[mp, :, st:st + sz], v_scr[st:st + sz, :], preferred_element_type=F32)

    def row_stat(x, op):
        return jnp.broadcast_to(op(x, axis=-1, keepdims=True), (tq, LANES))

    neg = jnp.full((tq, LANES), NEG_INF, F32)
    zero = jnp.zeros((tq, LANES), F32)
    mx = neg
    for st, sz in tiles:
        mx = scores_tile(0, st, sz, mx)
    m0 = row_stat(mx, jnp.max)
    mx, part0 = neg, zero
    for st, sz in tiles:
        mx = scores_tile(1, st, sz, mx)
        part0 = exp_tile(0, st, sz, m0, part0)
    m1 = row_stat(mx, jnp.max)
    acc0, part1 = zero, zero
    for st, sz in tiles:
        acc0 = value_tile(0, st, sz, acc0)
        part1 = exp_tile(1, st, sz, m1, part1)
    acc1 = jnp.dot(p_scr[1], v_scr[...], preferred_element_type=F32)

    lam = (jnp.exp(jnp.sum(lq1_ref[...] * lk1_ref[...], axis=-1, keepdims=True))
           - jnp.exp(jnp.sum(lq2_ref[...] * lk2_ref[...], axis=-1, keepdims=True)) + lam_init)
    o = acc0 / row_stat(part0, jnp.sum) - lam * (acc1 / row_stat(part1, jnp.sum))
    o_ref[...] = (_rms(o) * ng_ref[...] * (1.0 - lam_init)).astype(BF16)


def _attn(proj, batch, t, lam_params, norm_g, lam_init, cos=None, sin=None, cache_k=None, cache_v=None,
          emit_cache=False):
    use_rope = cos is not None
    n_cache = 0 if cache_k is None else cache_k.shape[3]
    tq = min(ATT_TQ, t)
    nq = t // tq
    n_keys = t + n_cache

    def col(cb, rows, row_map):
        return pl.BlockSpec((rows, HEAD_W), lambda b, h, i: (row_map(b, i), cb + h))

    in_specs = [col(CB_Q_DA, tq, lambda b, i: b * nq + i),
                col(CB_K_DA, t, lambda b, i: b), col(CB_V_DA, t, lambda b, i: b)]
    args = [proj, proj, proj]
    if use_rope:
        in_specs += [pl.BlockSpec((tq, HEAD_W), lambda b, h, i: (i, 0))] * 2
        in_specs += [pl.BlockSpec((t, HEAD_W), lambda b, h, i: (0, 0))] * 2
        args += [cos, sin, cos, sin]
    cache_spec = lambda n: pl.BlockSpec((1, 1, 1, n, HEAD_W), lambda b, h, i: (b, 0, h, 0, 0))
    if n_cache:
        in_specs += [cache_spec(n_cache)] * 2
        args += [cache_k, cache_v]
    small = pl.BlockSpec((1, DA_DK), lambda b, h, i: (0, 0))
    in_specs += [small] * 4 + [pl.BlockSpec((1, HEAD_W), lambda b, h, i: (0, 0))]
    args += [p.reshape(1, DA_DK) for p in lam_params] + [norm_g.reshape(1, HEAD_W)]
    out_shape = [jax.ShapeDtypeStruct((batch * t, DA_HEADS * HEAD_W), BF16)]
    out_specs = [pl.BlockSpec((tq, HEAD_W), lambda b, h, i: (b * nq + i, h))]
    if emit_cache:
        out_shape += [jax.ShapeDtypeStruct((batch, 1, DA_HEADS, t, HEAD_W), F32)] * 2
        out_specs += [cache_spec(t)] * 2
    scratch = [pltpu.VMEM((n_keys, HEAD_W), BF16)] * 3
    scratch += [pltpu.VMEM((2, tq, n_keys), F32), pltpu.VMEM((2, tq, n_keys), BF16)]
    return pl.pallas_call(
        functools.partial(_attn_kernel, t=t, n_cache=n_cache, use_rope=use_rope, emit_cache=emit_cache,
                          lam_init=lam_init),
        out_shape=out_shape,
        grid=(batch, DA_HEADS, nq),
        in_specs=in_specs,
        out_specs=out_specs,
        scratch_shapes=scratch,
        compiler_params=_cparams(("arbitrary",) * 3),
        name="attn_lat" if use_rope else "attn_ctx",
    )(*args)


def _out_kernel(mhg_c, mda_c, x_c, mhg_l, mda_l, x_l, w_ref, mod_ref, g_ref, rw_ref, rb_ref,
                x1_ref, h2_ref, ids_ref, wts_ref, *, ctx_tiles):
    def body(mhg_ref, mda_ref, x_ref):
        mix = jnp.dot(mhg_ref[...], w_ref[0:HG_WIDTH, :], preferred_element_type=F32)
        mix = mix + jnp.dot(mda_ref[...], w_ref[HG_WIDTH:, :], preferred_element_type=F32)
        x1 = x_ref[...] + mod_ref[0, 2:3, :] * mix
        x1_ref[...] = x1
        h2 = (_rms(x1) * g_ref[...]) * (1.0 + mod_ref[0, 4:5, :]) + mod_ref[0, 3:4, :]
        h2_ref[...] = h2

        h_hi = h2.astype(BF16)
        h_lo = (h2 - h_hi.astype(F32)).astype(BF16)
        logit = (jnp.dot(h_hi, rw_ref[0], preferred_element_type=F32)
                 + jnp.dot(h_lo, rw_ref[0], preferred_element_type=F32)
                 + jnp.dot(h_hi, rw_ref[1], preferred_element_type=F32)) + rb_ref[...]
        lane = lax.broadcasted_iota(I32, logit.shape, 1)

        def first_max(x):
            m = jnp.max(x, axis=-1, keepdims=True)
            return m, jnp.min(jnp.where(x == m, lane, LANES), axis=-1, keepdims=True)

        gmask = lane < N_GROUPS
        gmax, gsel = first_max(jnp.where(gmask, logit, NEG_INF))
        p_grp = 1.0 / jnp.sum(jnp.where(gmask, jnp.exp(logit - gmax), 0.0), axis=-1, keepdims=True)
        lo = N_GROUPS + EXP_PER_GROUP * gsel
        le = jnp.where((lane >= lo) & (lane < lo + EXP_PER_GROUP), logit, NEG_INF)
        v1, i1 = first_max(le)
        v2, i2 = first_max(jnp.where(lane == i1, NEG_INF, le))
        e = jnp.exp(v2 - v1)
        w1 = p_grp / (1.0 + e)
        w2 = p_grp * e / (1.0 + e)
        ids_ref[...] = jnp.where(lane == 0, i1 - N_GROUPS, jnp.where(lane == 1, i2 - N_GROUPS, 0))
        wts_ref[...] = jnp.where(lane == 0, w1, jnp.where(lane == 1, w2, 0.0))

    i = pl.program_id(0)
    pl.when(i < ctx_tiles)(lambda: body(mhg_c, mda_c, x_c))
    pl.when(i >= ctx_tiles)(lambda: body(mhg_l, mda_l, x_l))


def _out_proj(mhg_c, mda_c, x_c, mhg_l, mda_l, x_l, w_bf16, mods, g, rw, rb, lat_t):
    n_ctx, d = x_c.shape
    n_lat = x_l.shape[0]
    n = n_ctx + n_lat
    ctx_tiles = n_ctx // OUT_TM
    lat_tiles = lat_t // OUT_TM
    row = lambda i: (i, 0)
    const = lambda i: (0, 0)
    crow = lambda i: (jnp.minimum(i, ctx_tiles - 1), 0)
    lrow = lambda i: (jnp.maximum(i - ctx_tiles, 0), 0)
    seg = lambda i: (jnp.where(i < ctx_tiles, 0, 1 + (i - ctx_tiles) // lat_tiles), 0, 0)
    half = lambda m: pl.BlockSpec((OUT_TM, HG_WIDTH), m)
    full = lambda m: pl.BlockSpec((OUT_TM, d), m)
    return pl.pallas_call(
        functools.partial(_out_kernel, ctx_tiles=ctx_tiles),
        out_shape=[jax.ShapeDtypeStruct((n, d), F32), jax.ShapeDtypeStruct((n, d), F32),
                   jax.ShapeDtypeStruct((n, LANES), I32), jax.ShapeDtypeStruct((n, LANES), F32)],
        grid=(n // OUT_TM,),
        in_specs=[half(crow), half(crow), full(crow), half(lrow), half(lrow), full(lrow),
                  pl.BlockSpec((d, d), const), pl.BlockSpec((1, N_ADA, d), seg), pl.BlockSpec((1, d), const),
                  pl.BlockSpec((2, d, LANES), lambda i: (0, 0, 0)), pl.BlockSpec((1, LANES), const)],
        out_specs=[full(row), full(row), pl.BlockSpec((OUT_TM, LANES), row), pl.BlockSpec((OUT_TM, LANES), row)],
        compiler_params=_cparams(("arbitrary",)),
        name="out_proj",
    )(mhg_c, mda_c, x_c, mhg_l, mda_l, x_l, w_bf16, mods, g.reshape(1, d), rw, rb)


def _dispatch_kernel(src_ref, ntile_ref, h_hbm, o_ref, x0, x1, sems, *, n_tok):
    i = pl.program_id(0)
    n_live = (ntile_ref[0] * MOE_TM + DSP_TM - 1) // DSP_TM
    xs = (x0, x1)

    def issue(step, slot):
        def body(r, carry):
            p = jnp.maximum(src_ref[step * DSP_TM + r], 0)
            tok = jnp.where(p >= n_tok, p - n_tok, p)
            pltpu.make_async_copy(h_hbm.at[pl.ds(tok, 1)], xs[slot].at[pl.ds(r, 1)], sems.at[slot]).start()
            return carry
        lax.fori_loop(0, DSP_TM, body, 0, unroll=DMA_UNROLL)

    def wait(slot):
        pltpu.make_async_copy(h_hbm.at[pl.ds(0, DSP_TM)], xs[slot], sems.at[slot]).wait()

    @pl.when(i == 0)
    def _():
        issue(0, 0)

    def step(cur):
        nxt = 1 - cur
        wait(cur)
        issue(i + 1, nxt)
        o_ref[...] = xs[cur][...].astype(BF16)

        @pl.when(i == n_live - 1)
        def _():
            wait(nxt)

    live = i < n_live
    pl.when(live & (i % 2 == 0))(lambda: step(0))
    pl.when(live & (i % 2 == 1))(lambda: step(1))

    @pl.when(jnp.logical_not(live))
    def _():
        o_ref[...] = jnp.zeros_like(o_ref)


def _dispatch(src, n_tiles, h2, max_tiles, n_tok):
    d = h2.shape[1]
    steps = max_tiles * MOE_TM // DSP_TM
    return pl.pallas_call(
        functools.partial(_dispatch_kernel, n_tok=n_tok),
        out_shape=jax.ShapeDtypeStruct((steps * DSP_TM, d), BF16),
        grid_spec=pltpu.PrefetchScalarGridSpec(
            num_scalar_prefetch=2,
            grid=(steps,),
            in_specs=[pl.BlockSpec(memory_space=pl.ANY)],
            out_specs=pl.BlockSpec((DSP_TM, d), lambda i, src, nt: (i, 0)),
            scratch_shapes=[pltpu.VMEM((DSP_TM, d), F32)] * 2 + [pltpu.SemaphoreType.DMA((2,))],
        ),
        compiler_params=_cparams(("arbitrary",)),
        name="dispatch",
    )(src, n_tiles, h2)


def _gmm_kernel(texp_ref, tend_ref, ntile_ref, x_ref, w1_hbm, w3_hbm, w2_hbm, y_ref,
                w1b, w3b, w2b, slot_ref, sems):
    i = pl.program_id(0)
    n_tiles = ntile_ref[0]
    live = i < n_tiles
    e = texp_ref[i]
    first = (i == 0) | (texp_ref[jnp.maximum(i - 1, 0)] != e)

    def weight_copies(expert, slot):
        return [pltpu.make_async_copy(w_hbm.at[expert], buf.at[slot], sems.at[slot])
                for w_hbm, buf in ((w1_hbm, w1b), (w3_hbm, w3b), (w2_hbm, w2b))]

    @pl.when(i == 0)
    def _():
        slot_ref[0] = 1
        for c in weight_copies(e, 0):
            c.start()

    @pl.when(live & first)
    def _():
        slot = 1 - slot_ref[0]
        slot_ref[0] = slot
        for c in weight_copies(e, slot):
            c.wait()
        end = tend_ref[e]

        @pl.when(end < n_tiles)
        def _():
            for c in weight_copies(texp_ref[end], 1 - slot):
                c.start()

    @pl.when(live)
    def _():
        slot = slot_ref[0]
        x = x_ref[...]
        a = jnp.dot(x, w1b[slot].astype(BF16), preferred_element_type=F32)
        b = jnp.dot(x, w3b[slot].astype(BF16), preferred_element_type=F32)
        hid = (_silu(a) * b).astype(BF16)
        y_ref[...] = jnp.dot(hid, w2b[slot].astype(BF16), preferred_element_type=F32)

    @pl.when(jnp.logical_not(live))
    def _():
        y_ref[...] = jnp.zeros_like(y_ref)


def _gmm(tile_expert, tile_end, n_tiles, x_sorted, w1, w3, w2):
    d = x_sorted.shape[1]
    steps = tile_expert.shape[0]
    tile = lambda i, te, tend, nt: (i, 0)
    hbm = pl.BlockSpec(memory_space=pl.ANY)
    return pl.pallas_call(
        _gmm_kernel,
        out_shape=jax.ShapeDtypeStruct((steps * MOE_TM, d), F32),
        grid_spec=pltpu.PrefetchScalarGridSpec(
            num_scalar_prefetch=3,
            grid=(steps,),
            in_specs=[pl.BlockSpec((MOE_TM, d), tile), hbm, hbm, hbm],
            out_specs=pl.BlockSpec((MOE_TM, d), tile),
            scratch_shapes=[pltpu.VMEM((2, d, D_EXPERT), F32), pltpu.VMEM((2, d, D_EXPERT), F32),
                            pltpu.VMEM((2, D_EXPERT, d), F32), pltpu.SMEM((1,), I32),
                            pltpu.SemaphoreType.DMA((2,))],
        ),
        compiler_params=_cparams(("arbitrary",)),
        name="gmm",
    )(tile_expert, tile_end, n_tiles, x_sorted, w1, w3, w2)


def _route_tables(ids, n_tok):
    n_pairs = n_tok * TOP_K
    max_tiles = n_pairs // MOE_TM + N_EXPERTS
    e_flat = ids[:, :TOP_K].T.reshape(n_pairs)
    onehot = (e_flat[:, None] == jnp.arange(N_EXPERTS, dtype=I32)[None, :]).astype(I32)
    csum = jnp.cumsum(onehot, axis=0)
    counts = csum[-1]
    rank = jnp.sum(csum * onehot, axis=1) - 1
    tiles_per = (counts + MOE_TM - 1) // MOE_TM
    tile_end = jnp.cumsum(tiles_per)
    row_start = (tile_end - tiles_per) * MOE_TM
    pos = row_start[e_flat] + rank
    src = jnp.full((max_tiles * MOE_TM + DSP_TM,), -1, I32).at[pos].set(jnp.arange(n_pairs, dtype=I32))
    steps = jnp.arange(max_tiles, dtype=I32)
    owner = jnp.sum((tile_end[None, :] <= steps[:, None]).astype(I32), axis=1)
    tile_expert = jnp.minimum(owner, N_EXPERTS - 1)
    return src, pos, tile_expert, tile_end.astype(I32), tile_end[-1:].astype(I32)


def _combine_kernel(pos_ref, x1_ref, wts_ref, mod_ref, g_ref, y_hbm, o_ref, ya0, yb0, ya1, yb1, sems,
                    *, tok0, n_tok, n_steps):
    i = pl.program_id(0)
    bufs = ((ya0, yb0), (ya1, yb1))

    def issue(tile, slot):
        def body(r, carry):
            for which in range(2):
                row = pos_ref[which * n_tok + tok0 + tile * CMB_TM + r]
                pltpu.make_async_copy(y_hbm.at[pl.ds(row, 1)], bufs[slot][which].at[pl.ds(r, 1)],
                                      sems.at[2 * slot + which]).start()
            return carry
        lax.fori_loop(0, CMB_TM, body, 0, unroll=DMA_UNROLL)

    def wait(slot):
        for which in range(2):
            pltpu.make_async_copy(y_hbm.at[pl.ds(0, CMB_TM)], bufs[slot][which], sems.at[2 * slot + which]).wait()

    @pl.when(i == 0)
    def _():
        issue(0, 0)

    def step(cur):
        nxt = 1 - cur
        wait(cur)
        issue(jnp.minimum(i + 1, n_steps - 1), nxt)
        w = wts_ref[...]
        moe = w[:, 0:1] * bufs[cur][0][...] + w[:, 1:2] * bufs[cur][1][...]
        x2 = x1_ref[...] + mod_ref[0, 5:6, :] * moe
        o_ref[...] = _rms(x2) * g_ref[...]

        @pl.when(i == n_steps - 1)
        def _():
            wait(nxt)

    pl.when(i % 2 == 0)(lambda: step(0))
    pl.when(i % 2 == 1)(lambda: step(1))


def _combine(pos, x1, y_sorted, wts, mods, mod_row0, g, row0, batch, t, n_tok):
    d = x1.shape[1]
    nt = t // CMB_TM
    rb0 = row0 // CMB_TM
    n_steps = batch * nt
    return pl.pallas_call(
        functools.partial(_combine_kernel, tok0=row0, n_tok=n_tok, n_steps=n_steps),
        out_shape=jax.ShapeDtypeStruct((batch * t, d), F32),
        grid_spec=pltpu.PrefetchScalarGridSpec(
            num_scalar_prefetch=1,
            grid=(n_steps,),
            in_specs=[pl.BlockSpec((CMB_TM, d), lambda i, pos: (rb0 + i, 0)),
                      pl.BlockSpec((CMB_TM, LANES), lambda i, pos: (rb0 + i, 0)),
                      pl.BlockSpec((1, N_ADA, d), lambda i, pos: (mod_row0 + i // nt, 0, 0)),
                      pl.BlockSpec((1, d), lambda i, pos: (0, 0)),
                      pl.BlockSpec(memory_space=pl.ANY)],
            out_specs=pl.BlockSpec((CMB_TM, d), lambda i, pos: (i, 0)),
            scratch_shapes=[pltpu.VMEM((CMB_TM, d), F32)] * 4 + [pltpu.SemaphoreType.DMA((4,))],
        ),
        compiler_params=_cparams(("arbitrary",)),
        name="combine",
    )(pos, x1, wts, mods, g.reshape(1, d), y_sorted)


def _rope_tables(n_lat):
    half = DA_DK // 2
    nf = half // 2
    t = jnp.arange(n_lat)
    row = (t // GRID_W).astype(F32)
    col = (t % GRID_W).astype(F32)
    inv = ROPE_THETA ** (-jnp.arange(nf, dtype=F32) / nf)
    lane = np.arange(HEAD_W)
    freq = inv[lane % nf]
    pos = jnp.where(((lane // half) % 2 == 0)[None, :], row[:, None], col[:, None])
    ang = pos * freq[None, :]
    sign = np.where((lane % half) < nf, -1.0, 1.0).astype(np.float32)
    return jnp.cos(ang), jnp.sin(ang) * sign[None, :]


def kernel(x_prompt, x_sample, cache_k, cache_v, state_hgrn_fwd, state_hgrn_bwd, c, c_ctx, w_ada, b_ada,
           norm1_g, norm2_g, norm_final_g, w_in, hg_lb_fwd, hg_lb_bwd, hg_norm_g, da_lambda_q1, da_lambda_k1,
           da_lambda_q2, da_lambda_k2, da_norm_g, w_out, router_g_w, router_g_b, router_e_w, router_e_b,
           exp_w1, exp_w3, exp_w2):
    l = 0
    batch, seq, d = x_prompt.shape
    dec_batch, dec_seq, _ = x_sample.shape
    n_ctx = batch * seq
    n_lat = dec_batch * dec_seq
    n_tok = n_ctx + n_lat
    lam_init = 0.8 - 0.6 * math.exp(-0.3 * l)

    cond = jnp.zeros((8, d), F32).at[0].set(c_ctx).at[1:1 + dec_batch].set(c)
    mods = _ada(cond, w_ada[l], b_ada[l]).reshape(8, N_ADA, d)

    w_in_bf16 = w_in[l].astype(BF16)
    proj_c = _in_proj(x_prompt.reshape(1, n_ctx, d), mods, 0, norm1_g[l], w_in_bf16, 1024, "in_proj_ctx")
    proj_l = _in_proj(x_sample, mods, 1, norm1_g[l], w_in_bf16, 1024, "in_proj_lat")
    lam_params = (da_lambda_q1[l], da_lambda_k1[l], da_lambda_q2[l], da_lambda_k2[l])

    da_ctx, new_k, new_v = _attn(proj_c, batch, seq, lam_params, da_norm_g[l], lam_init, emit_cache=True)
    hg_ctx, new_sf, new_sb = _hgrn(proj_c, batch, seq, hg_lb_fwd, hg_lb_bwd, hg_norm_g[l], emit_state=True)

    cos, sin = _rope_tables(dec_seq)
    (da_lat,) = _attn(proj_l, dec_batch, dec_seq, lam_params, da_norm_g[l], lam_init, cos=cos, sin=sin,
                      cache_k=cache_k[:, l:l + 1], cache_v=cache_v[:, l:l + 1])
    (hg_lat,) = _hgrn(proj_l, dec_batch, dec_seq, hg_lb_fwd, hg_lb_bwd, hg_norm_g[l],
                      s0f=state_hgrn_fwd[:, l:l + 1], s0b=state_hgrn_bwd[:, l:l + 1])

    rw = jnp.zeros((d, LANES), F32).at[:, :N_GROUPS].set(router_g_w[l]).at[:, N_GROUPS:N_GROUPS + N_EXPERTS].set(
        router_e_w[l])
    rw_hi = rw.astype(BF16)
    rw = jnp.stack([rw_hi, (rw - rw_hi.astype(F32)).astype(BF16)])
    rb = jnp.zeros((1, LANES), F32).at[0, :N_GROUPS].set(router_g_b[l]).at[0, N_GROUPS:N_GROUPS + N_EXPERTS].set(
        router_e_b[l])
    x1, h2, ids, wts = _out_proj(hg_ctx, da_ctx, x_prompt.reshape(n_ctx, d), hg_lat, da_lat,
                                 x_sample.reshape(n_lat, d), w_out[l].astype(BF16), mods, norm2_g[l], rw, rb, dec_seq)

    src, pos, tile_expert, tile_end, n_tiles = _route_tables(ids, n_tok)
    x_sorted = _dispatch(src, n_tiles, h2, tile_expert.shape[0], n_tok)
    y_sorted = _gmm(tile_expert, tile_end, n_tiles, x_sorted, exp_w1[l], exp_w3[l], exp_w2[l])

    y_ctx = _combine(pos, x1, y_sorted, wts, mods, 0, norm_final_g, 0, 1, n_ctx, n_tok)
    y_lat = _combine(pos, x1, y_sorted, wts, mods, 1, norm_final_g, n_ctx, dec_batch, dec_seq, n_tok)
    return (y_ctx.reshape(batch, seq, d), y_lat.reshape(dec_batch, dec_seq, d), new_k, new_v, new_sf, new_sb)
```

```python
import functools
import math

import numpy as np
import jax
import jax.numpy as jnp
from jax import lax
from jax.experimental import pallas as pl
from jax.experimental.pallas import tpu as pltpu

F32 = jnp.float32
BF16 = jnp.bfloat16
I32 = jnp.int32

GRID_W = 64
HG_WIDTH = 1024
HG_DK = 128
HG_HEADS = 8
DA_HEADS = 8
DA_DK = 64
HEAD_W = 128
ROPE_THETA = 10000.0
N_GROUPS = 4
EXP_PER_GROUP = 8
N_EXPERTS = 32
TOP_K = 2
D_EXPERT = 512
N_ADA = 6
RMS_EPS = 1e-6
CB_Q_HG, CB_F_FW, CB_F_BW, CB_I_HG, CB_G_HG, CB_Q_DA, CB_K_DA, CB_V_DA = (8 * i for i in range(8))

LANES = 128
SUBLANES = 8
VMEM_LIMIT = 56 * 1024 * 1024

ADA_TN = 1536
IN_TN = 512
OUT_TM = 256
HG_CHUNK = 64
HG_GROUP = 4
ATT_TQ = 256
ATT_TK = 512
MOE_TM = 256
CMB_TM = 512
DMA_UNROLL = 8
NEG_INF = float("-inf")


def _cparams(sem):
    return pltpu.CompilerParams(dimension_semantics=sem, vmem_limit_bytes=VMEM_LIMIT)


def _silu(x):
    return x * jax.nn.sigmoid(x)


def _rms(x):
    return x * lax.rsqrt(jnp.mean(x * x, axis=-1, keepdims=True) + RMS_EPS)


def _token_pitch(d):
    return d // LANES + 1


def _store_token_major(ref, x):
    rows, d = x.shape
    pitch = _token_pitch(d)
    for c in range(pitch - 1):
        ref[pl.ds(c, rows, stride=pitch), :] = x[:, c * LANES:(c + 1) * LANES]
    ref[pl.ds(pitch - 1, rows, stride=pitch), :] = jnp.zeros((rows, LANES), x.dtype)


def _load_token_major(ref, rows):
    pitch = ref.shape[0] // rows
    return jnp.concatenate([ref[pl.ds(c, rows, stride=pitch), :] for c in range(pitch - 1)], axis=1)


def _ada_kernel(cond_ref, w_ref, b_ref, o_ref):
    s = _silu(cond_ref[...]).astype(BF16)
    o_ref[...] = jnp.dot(s, w_ref[...].astype(BF16), preferred_element_type=F32) + b_ref[...]


def _ada(cond, w, b):
    rows, d = cond.shape
    n = w.shape[1]
    return pl.pallas_call(
        _ada_kernel,
        out_shape=jax.ShapeDtypeStruct((rows, n), F32),
        grid=(n // ADA_TN,),
        in_specs=[pl.BlockSpec((rows, d), lambda j: (0, 0)),
                  pl.BlockSpec((d, ADA_TN), lambda j: (0, j)),
                  pl.BlockSpec((1, ADA_TN), lambda j: (0, j))],
        out_specs=pl.BlockSpec((rows, ADA_TN), lambda j: (0, j)),
        compiler_params=_cparams(("arbitrary",)),
        name="ada",
    )(cond, w, b.reshape(1, n))


def _in_kernel(x_ref, mod_ref, g_ref, w_ref, o_ref, h_ref):
    @pl.when(pl.program_id(1) == 0)
    def _():
        y = _rms(x_ref[...]) * g_ref[...]
        h_ref[...] = (y * (1.0 + mod_ref[0, 1:2, :]) + mod_ref[0, 0:1, :]).astype(BF16)

    o_ref[...] = jnp.dot(h_ref[...], w_ref[...], preferred_element_type=F32)


def _in_proj(x, mods, mod_row0, g, w_bf16, tm, name):
    batch, t, d = x.shape
    cols = w_bf16.shape[1]
    nt = t // tm
    return pl.pallas_call(
        _in_kernel,
        out_shape=jax.ShapeDtypeStruct((batch * t, cols), F32),
        grid=(batch * nt, cols // IN_TN),
        in_specs=[pl.BlockSpec((tm, d), lambda i, j: (i, 0)),
                  pl.BlockSpec((1, N_ADA, d), lambda i, j: (mod_row0 + i // nt, 0, 0)),
                  pl.BlockSpec((1, d), lambda i, j: (0, 0)),
                  pl.BlockSpec((d, IN_TN), lambda i, j: (0, j))],
        out_specs=pl.BlockSpec((tm, IN_TN), lambda i, j: (i, j)),
        scratch_shapes=[pltpu.VMEM((tm, d), BF16)],
        compiler_params=_cparams(("arbitrary", "arbitrary")),
        name=name,
    )(x.reshape(batch * t, d), mods, g.reshape(1, d), w_bf16)


def _rope(x, cos, sin_signed):
    lane = lax.broadcasted_iota(I32, x.shape, 1)
    first = (lane % 32) < 16
    partner = jnp.where(first, pltpu.roll(x, LANES - 16, 1), pltpu.roll(x, 16, 1))
    return x * cos + partner * sin_signed


def _split_maps(k):
    lane = lax.broadcasted_iota(I32, k.shape, 1)
    m1 = lane < DA_DK
    return jnp.where(m1, k, 0.0).astype(BF16), jnp.where(m1, 0.0, k).astype(BF16)


def _lower_bound(lb_ref):
    p = lb_ref[...]
    e = jnp.exp(p - jnp.max(p, axis=0, keepdims=True))
    return e[0:1, :] / jnp.sum(e, axis=0, keepdims=True)


def _hgrn_pair_masks(mask_ref, c):
    ti = lax.broadcasted_iota(I32, (c, c), 0)
    si = lax.broadcasted_iota(I32, (c, c), 1)
    for d in range(2):
        mask_ref[d, 0] = (ti == si).astype(F32)
        for j in range(c.bit_length() - 1):
            same = (ti >> (j + 1)) == (si >> (j + 1))
            t_bit = ((ti >> j) & 1) == 1
            s_bit = ((si >> j) & 1) == 1
            pair = (s_bit & jnp.logical_not(t_bit)) if d else (t_bit & jnp.logical_not(s_bit))
            mask_ref[d, 1 + j] = (same & pair).astype(F32)


def _hgrn_chunk(q, f_logit, v, lb, mask_ref, rev):
    c = q.shape[0]
    n_piece = c // SUBLANES
    levels = c.bit_length() - 1
    low = SUBLANES.bit_length() - 1
    d = 1 if rev else 0
    nt = (((1,), (1,)), ((), ()))
    pieces = lambda x: [x[SUBLANES * i:SUBLANES * (i + 1)] for i in range(n_piece)]
    whole = lambda xs: jnp.concatenate(xs, axis=0)

    fg = lb + (1.0 - lb) * jax.nn.sigmoid(f_logit)
    k = 1.0 - fg
    q_bf = q.astype(BF16)
    k_bf = k.astype(BF16)
    qs, ks, fgs = pieces(q), pieces(k), pieces(fg)
    tot = pieces(jnp.log(fg))
    pre = list(tot)
    suf = [jnp.zeros((SUBLANES, LANES), F32)] * n_piece
    sub = lax.broadcasted_iota(I32, (SUBLANES, LANES), 0)

    def pair_scores(qt, kt, idx):
        return lax.dot_general(qt, kt, nt, preferred_element_type=F32) * mask_ref[d, idx]

    scores = pair_scores(q_bf, k_bf, 0)
    yield
    for j in range(levels):
        if j < low:
            sh = 1 << j
            bit = ((sub >> j) & 1) == 1
            late = jnp.logical_not(bit) if rev else bit
            if j == 0:
                qt = whole([qs[i] * jnp.where(late, fgs[i], 1.0) for i in range(n_piece)]).astype(BF16)
                kt = k_bf
            else:
                es = [jnp.exp(jnp.where(late, pre[i], suf[i])) for i in range(n_piece)]
                qt = whole([qs[i] * es[i] for i in range(n_piece)]).astype(BF16)
                kt = whole([ks[i] * es[i] for i in range(n_piece)]).astype(BF16)
            for i in range(n_piece):
                up = pltpu.roll(tot[i], sh, 0)
                dn = pltpu.roll(tot[i], SUBLANES - sh, 0)
                sib = jnp.where(late, dn, up) if rev else jnp.where(late, up, dn)
                pre[i] = pre[i] + jnp.where(late, sib, 0.0)
                suf[i] = suf[i] + jnp.where(late, 0.0, sib)
                tot[i] = tot[i] + sib
        else:
            half = 1 << (j - low)
            upper = [(i // half) % 2 == 1 for i in range(n_piece)]
            late = [(not u) if rev else u for u in upper]
            es = [jnp.exp(pre[i] if late[i] else suf[i]) for i in range(n_piece)]
            qt = whole([qs[i] * es[i] for i in range(n_piece)]).astype(BF16)
            kt = whole([ks[i] * es[i] for i in range(n_piece)]).astype(BF16)
            sib = [tot[i - half] if upper[i] else tot[i + half] for i in range(n_piece)]
            pre = [pre[i] + sib[i] if late[i] else pre[i] for i in range(n_piece)]
            suf = [suf[i] if late[i] else suf[i] + sib[i] for i in range(n_piece)]
            tot = [tot[i] + sib[i] for i in range(n_piece)]
        scores = scores + pair_scores(qt, kt, 1 + j)
        yield

    q_dec = whole([qs[i] * jnp.exp(pre[i]) for i in range(n_piece)]).astype(BF16)
    k_dec = whole([ks[i] * jnp.exp(suf[i]) for i in range(n_piece)]).astype(BF16)
    return dict(q_dec=q_dec, k_dec=k_dec, v_t=v.T.astype(BF16), v=v.astype(BF16),
                scores=scores.astype(BF16), decay=jnp.exp(tot[0][0:1, :]))


def _hgrn_state_step(p, st_ref):
    st = st_ref[...]
    o = lax.dot_general(p["q_dec"], st.astype(BF16), (((1,), (1,)), ((), ())), preferred_element_type=F32)
    st_ref[...] = p["decay"] * st + jnp.dot(p["v_t"], p["k_dec"], preferred_element_type=F32)
    return o


def _lockstep(gens):
    results = [None] * len(gens)
    live = list(range(len(gens)))
    while live:
        for idx in list(live):
            try:
                next(gens[idx])
            except StopIteration as stop:
                results[idx] = stop.value
                live.remove(idx)
    return results


def _hgrn_kernel(*refs, t, has_init, emit_state):
    q_ref, ff_ref, fb_ref, i_ref, g_ref, lbf_ref, lbb_ref, ng_ref = refs[:8]
    pos = 8
    if has_init:
        s0f_ref, s0b_ref = refs[pos:pos + 2]
        pos += 2
    o_ref = refs[pos]
    pos += 1
    if emit_state:
        sf_ref, sb_ref = refs[pos:pos + 2]
        pos += 2
    of_scr, ob_scr, stf, stb, mask_scr = refs[pos:pos + 5]

    c = HG_CHUNK
    n = t // c
    _hgrn_pair_masks(mask_scr, c)
    if has_init:
        stf[...] = s0f_ref[0, 0, 0].T
        stb[...] = s0b_ref[0, 0, 0].T
    else:
        stf[...] = jnp.zeros_like(stf)
        stb[...] = jnp.zeros_like(stb)
    lbf = _lower_bound(lbf_ref)
    lbb = _lower_bound(lbb_ref)

    def body(i, carry):
        work = []
        for u in range(HG_GROUP):
            ci = i * HG_GROUP + u
            work.append((pl.ds(pl.multiple_of(ci * c, c), c), ff_ref, lbf, stf, of_scr, False))
            work.append((pl.ds(pl.multiple_of((n - 1 - ci) * c, c), c), fb_ref, lbb, stb, ob_scr, True))
        parts = _lockstep([_hgrn_chunk(_silu(q_ref[sl, :]), f_ref[sl, :], i_ref[sl, :], lb, mask_scr, rev)
                           for sl, f_ref, lb, _, _, rev in work])
        outs = [_hgrn_state_step(p, w[3]) for p, w in zip(parts, work)]
        for p, w, o in zip(parts, work, outs):
            w[4][w[0], :] = o + jnp.dot(p["scores"], p["v"], preferred_element_type=F32)
        return carry

    lax.fori_loop(0, n // HG_GROUP, body, 0)

    fin = min(t, 256)

    def finish(i, carry):
        sl = pl.ds(pl.multiple_of(i * fin, fin), fin)
        o = _rms(of_scr[sl, :] + ob_scr[sl, :]) * ng_ref[...]
        o_ref[sl, :] = (o * _silu(g_ref[sl, :])).astype(BF16)
        return carry

    lax.fori_loop(0, t // fin, finish, 0)
    if emit_state:
        sf_ref[0, 0, 0] = stf[...].T
        sb_ref[0, 0, 0] = stb[...].T


def _hgrn(proj, batch, t, lb_fwd, lb_bwd, norm_g, s0f=None, s0b=None, emit_state=False):
    has_init = s0f is not None

    def col(cb):
        return pl.BlockSpec((t, HEAD_W), lambda b, h: (b, cb + h))

    lb_spec = pl.BlockSpec((lb_fwd.shape[0], HEAD_W), lambda b, h: (0, h))
    in_specs = [col(CB_Q_HG), col(CB_F_FW), col(CB_F_BW), col(CB_I_HG), col(CB_G_HG),
                lb_spec, lb_spec, pl.BlockSpec((1, HEAD_W), lambda b, h: (0, 0))]
    args = [proj] * 5 + [lb_fwd, lb_bwd, norm_g.reshape(1, HEAD_W)]
    st_spec = pl.BlockSpec((1, 1, 1, HG_DK, HEAD_W), lambda b, h: (b, 0, h, 0, 0))
    if has_init:
        in_specs += [st_spec, st_spec]
        args += [s0f, s0b]
    out_shape = [jax.ShapeDtypeStruct((batch * t, HG_WIDTH), BF16)]
    out_specs = [pl.BlockSpec((t, HEAD_W), lambda b, h: (b, h))]
    if emit_state:
        st = jax.ShapeDtypeStruct((batch, 1, HG_HEADS, HG_DK, HEAD_W), F32)
        out_shape += [st, st]
        out_specs += [st_spec, st_spec]
    return pl.pallas_call(
        functools.partial(_hgrn_kernel, t=t, has_init=has_init, emit_state=emit_state),
        out_shape=out_shape,
        grid=(batch, HG_HEADS),
        in_specs=in_specs,
        out_specs=out_specs,
        scratch_shapes=[pltpu.VMEM((t, HEAD_W), F32), pltpu.VMEM((t, HEAD_W), F32),
                        pltpu.VMEM((HEAD_W, HG_DK), F32), pltpu.VMEM((HEAD_W, HG_DK), F32),
                        pltpu.VMEM((2, HG_CHUNK.bit_length(), HG_CHUNK, HG_CHUNK), F32)],
        compiler_params=_cparams(("arbitrary", "arbitrary")),
        name="hgrn_lat" if has_init else "hgrn_ctx",
    )(*args)


def _attn_kernel(*refs, t, n_cache, use_rope, emit_cache, lam_init):
    it = iter(refs)
    q_ref, k_ref, v_ref = next(it), next(it), next(it)
    if use_rope:
        cosq_ref, sinq_ref, cos_ref, sin_ref = next(it), next(it), next(it), next(it)
    if n_cache:
        ck_ref, cv_ref = next(it), next(it)
    lq1_ref, lk1_ref, lq2_ref, lk2_ref, ng_ref, o_ref = (next(it) for _ in range(6))
    if emit_cache:
        nk_ref, nv_ref = next(it), next(it)
    k1_scr, k2_scr, v_scr, s_scr, p_scr = (next(it) for _ in range(5))
    k_scrs = (k1_scr, k2_scr)
    tq = q_ref.shape[0]

    @pl.when(pl.program_id(2) == 0)
    def _():
        rows = min(t, ATT_TK)

        def stage(i, carry):
            sl = pl.ds(pl.multiple_of(i * rows, rows), rows)
            k = k_ref[sl, :]
            v = v_ref[sl, :]
            if emit_cache:
                nk_ref[0, 0, 0, sl, :] = k
                nv_ref[0, 0, 0, sl, :] = v
            if use_rope:
                k = _rope(k, cos_ref[sl, :], sin_ref[sl, :])
            k1_scr[sl, :], k2_scr[sl, :] = _split_maps(k)
            v_scr[sl, :] = v.astype(BF16)
            return carry

        lax.fori_loop(0, t // rows, stage, 0)
        if n_cache:
            k1_scr[t:t + n_cache, :], k2_scr[t:t + n_cache, :] = _split_maps(ck_ref[0, 0, 0])
            v_scr[t:t + n_cache, :] = cv_ref[0, 0, 0].astype(BF16)

    q = q_ref[...]
    if use_rope:
        q = _rope(q, cosq_ref[...], sinq_ref[...])
    q = (q * (DA_DK ** -0.5)).astype(BF16)
    nt = (((1,), (1,)), ((), ()))
    n_keys = t + n_cache
    tiles = [(st, min(ATT_TK, n_keys - st)) for st in range(0, n_keys, ATT_TK)]

    def scores_tile(mp, st, sz, mx):
        s = lax.dot_general(q, k_scrs[mp][st:st + sz, :], nt, preferred_element_type=F32)
        s_scr[mp, :, st:st + sz] = s
        for j in range(sz // LANES):
            mx = jnp.maximum(mx, s[:, j * LANES:(j + 1) * LANES])
        return mx

    def exp_tile(mp, st, sz, m, part):
        for lo in range(st, st + sz, LANES):
            e = jnp.exp(s_scr[mp, :, lo:lo + LANES] - m)
            part = part + e
            p_scr[mp, :, lo:lo + LANES] = e.astype(BF16)
        return part

    def value_tile(mp, st, sz, acc):
        return acc + jnp.dot(p_scr[mp, :, st:st + sz], v_scr[st:st + sz, :], preferred_element_type=F32)

    def row_stat(x, op):
        return jnp.broadcast_to(op(x, axis=-1, keepdims=True), (tq, LANES))

    neg = jnp.full((tq, LANES), NEG_INF, F32)
    zero = jnp.zeros((tq, LANES), F32)
    mx = neg
    for st, sz in tiles:
        mx = scores_tile(0, st, sz, mx)
    m0 = row_stat(mx, jnp.max)
    mx, part0 = neg, zero
    for st, sz in tiles:
        mx = scores_tile(1, st, sz, mx)
        part0 = exp_tile(0, st, sz, m0, part0)
    m1 = row_stat(mx, jnp.max)
    acc0, part1 = zero, zero
    for st, sz in tiles:
        acc0 = value_tile(0, st, sz, acc0)
        part1 = exp_tile(1, st, sz, m1, part1)
    acc1 = jnp.dot(p_scr[1], v_scr[...], preferred_element_type=F32)

    lam = (jnp.exp(jnp.sum(lq1_ref[...] * lk1_ref[...], axis=-1, keepdims=True))
           - jnp.exp(jnp.sum(lq2_ref[...] * lk2_ref[...], axis=-1, keepdims=True)) + lam_init)
    o = acc0 / row_stat(part0, jnp.sum) - lam * (acc1 / row_stat(part1, jnp.sum))
    o_ref[...] = (_rms(o) * ng_ref[...] * (1.0 - lam_init)).astype(BF16)


def _attn(proj, batch, t, lam_params, norm_g, lam_init, cos=None, sin=None, cache_k=None, cache_v=None,
          emit_cache=False):
    use_rope = cos is not None
    n_cache = 0 if cache_k is None else cache_k.shape[3]
    tq = min(ATT_TQ, t)
    nq = t // tq
    n_keys = t + n_cache

    def col(cb, rows, row_map):
        return pl.BlockSpec((rows, HEAD_W), lambda b, h, i: (row_map(b, i), cb + h))

    in_specs = [col(CB_Q_DA, tq, lambda b, i: b * nq + i),
                col(CB_K_DA, t, lambda b, i: b), col(CB_V_DA, t, lambda b, i: b)]
    args = [proj, proj, proj]
    if use_rope:
        in_specs += [pl.BlockSpec((tq, HEAD_W), lambda b, h, i: (i, 0))] * 2
        in_specs += [pl.BlockSpec((t, HEAD_W), lambda b, h, i: (0, 0))] * 2
        args += [cos, sin, cos, sin]
    cache_spec = lambda n: pl.BlockSpec((1, 1, 1, n, HEAD_W), lambda b, h, i: (b, 0, h, 0, 0))
    if n_cache:
        in_specs += [cache_spec(n_cache)] * 2
        args += [cache_k, cache_v]
    small = pl.BlockSpec((1, DA_DK), lambda b, h, i: (0, 0))
    in_specs += [small] * 4 + [pl.BlockSpec((1, HEAD_W), lambda b, h, i: (0, 0))]
    args += [p.reshape(1, DA_DK) for p in lam_params] + [norm_g.reshape(1, HEAD_W)]
    out_shape = [jax.ShapeDtypeStruct((batch * t, DA_HEADS * HEAD_W), BF16)]
    out_specs = [pl.BlockSpec((tq, HEAD_W), lambda b, h, i: (b * nq + i, h))]
    if emit_cache:
        out_shape += [jax.ShapeDtypeStruct((batch, 1, DA_HEADS, t, HEAD_W), F32)] * 2
        out_specs += [cache_spec(t)] * 2
    scratch = [pltpu.VMEM((n_keys, HEAD_W), BF16)] * 3
    scratch += [pltpu.VMEM((2, tq, n_keys), F32), pltpu.VMEM((2, tq, n_keys), BF16)]
    return pl.pallas_call(
        functools.partial(_attn_kernel, t=t, n_cache=n_cache, use_rope=use_rope, emit_cache=emit_cache,
                          lam_init=lam_init),
        out_shape=out_shape,
        grid=(batch, DA_HEADS, nq),
        in_specs=in_specs,
        out_specs=out_specs,
        scratch_shapes=scratch,
        compiler_params=_cparams(("arbitrary",) * 3),
        name="attn_lat" if use_rope else "attn_ctx",
    )(*args)


def _out_kernel(mhg_c, mda_c, x_c, mhg_l, mda_l, x_l, w_ref, mod_ref, g_ref, rw_ref, rb_ref,
                x1_ref, h2_ref, ids_ref, wts_ref, *, ctx_tiles):
    def body(mhg_ref, mda_ref, x_ref):
        mix = jnp.dot(mhg_ref[...], w_ref[0:HG_WIDTH, :], preferred_element_type=F32)
        mix = mix + jnp.dot(mda_ref[...], w_ref[HG_WIDTH:, :], preferred_element_type=F32)
        x1 = x_ref[...] + mod_ref[0, 2:3, :] * mix
        x1_ref[...] = x1
        h2 = (_rms(x1) * g_ref[...]) * (1.0 + mod_ref[0, 4:5, :]) + mod_ref[0, 3:4, :]
        _store_token_major(h2_ref, h2)

        h_hi = h2.astype(BF16)
        h_lo = (h2 - h_hi.astype(F32)).astype(BF16)
        logit = (jnp.dot(h_hi, rw_ref[0], preferred_element_type=F32)
                 + jnp.dot(h_lo, rw_ref[0], preferred_element_type=F32)
                 + jnp.dot(h_hi, rw_ref[1], preferred_element_type=F32)) + rb_ref[...]
        lane = lax.broadcasted_iota(I32, logit.shape, 1)

        def first_max(x):
            m = jnp.max(x, axis=-1, keepdims=True)
            return m, jnp.min(jnp.where(x == m, lane, LANES), axis=-1, keepdims=True)

        gmask = lane < N_GROUPS
        gmax, gsel = first_max(jnp.where(gmask, logit, NEG_INF))
        p_grp = 1.0 / jnp.sum(jnp.where(gmask, jnp.exp(logit - gmax), 0.0), axis=-1, keepdims=True)
        lo = N_GROUPS + EXP_PER_GROUP * gsel
        le = jnp.where((lane >= lo) & (lane < lo + EXP_PER_GROUP), logit, NEG_INF)
        v1, i1 = first_max(le)
        v2, i2 = first_max(jnp.where(lane == i1, NEG_INF, le))
        e = jnp.exp(v2 - v1)
        w1 = p_grp / (1.0 + e)
        w2 = p_grp * e / (1.0 + e)
        ids_ref[...] = jnp.where(lane == 0, i1 - N_GROUPS, jnp.where(lane == 1, i2 - N_GROUPS, 0))
        wts_ref[...] = jnp.where(lane == 0, w1, jnp.where(lane == 1, w2, 0.0))

    i = pl.program_id(0)
    pl.when(i < ctx_tiles)(lambda: body(mhg_c, mda_c, x_c))
    pl.when(i >= ctx_tiles)(lambda: body(mhg_l, mda_l, x_l))


def _out_proj(mhg_c, mda_c, x_c, mhg_l, mda_l, x_l, w_bf16, mods, g, rw, rb, lat_t):
    n_ctx, d = x_c.shape
    n_lat = x_l.shape[0]
    n = n_ctx + n_lat
    ctx_tiles = n_ctx // OUT_TM
    lat_tiles = lat_t // OUT_TM
    row = lambda i: (i, 0)
    const = lambda i: (0, 0)
    crow = lambda i: (jnp.minimum(i, ctx_tiles - 1), 0)
    lrow = lambda i: (jnp.maximum(i - ctx_tiles, 0), 0)
    seg = lambda i: (jnp.where(i < ctx_tiles, 0, 1 + (i - ctx_tiles) // lat_tiles), 0, 0)
    half = lambda m: pl.BlockSpec((OUT_TM, HG_WIDTH), m)
    full = lambda m: pl.BlockSpec((OUT_TM, d), m)
    return pl.pallas_call(
        functools.partial(_out_kernel, ctx_tiles=ctx_tiles),
        out_shape=[jax.ShapeDtypeStruct((n, d), F32), jax.ShapeDtypeStruct((n * _token_pitch(d), LANES), F32),
                   jax.ShapeDtypeStruct((n, LANES), I32), jax.ShapeDtypeStruct((n, LANES), F32)],
        grid=(n // OUT_TM,),
        in_specs=[half(crow), half(crow), full(crow), half(lrow), half(lrow), full(lrow),
                  pl.BlockSpec((d, d), const), pl.BlockSpec((1, N_ADA, d), seg), pl.BlockSpec((1, d), const),
                  pl.BlockSpec((2, d, LANES), lambda i: (0, 0, 0)), pl.BlockSpec((1, LANES), const)],
        out_specs=[full(row), pl.BlockSpec((OUT_TM * _token_pitch(d), LANES), row),
                   pl.BlockSpec((OUT_TM, LANES), row), pl.BlockSpec((OUT_TM, LANES), row)],
        compiler_params=_cparams(("arbitrary",)),
        name="out_proj",
    )(mhg_c, mda_c, x_c, mhg_l, mda_l, x_l, w_bf16, mods, g.reshape(1, d), rw, rb)


def _gmm_kernel(src_ref, texp_ref, tend_ref, ntile_ref, h_hbm, w1_hbm, w3_hbm, w2_hbm, y_ref,
                x0, x1, w1b, w3b, w2b, slot_ref, sems, *, n_tok, chunks):
    i = pl.program_id(0)
    n_tiles = ntile_ref[0]
    live = i < n_tiles
    e = texp_ref[i]
    first = (i == 0) | (texp_ref[jnp.maximum(i - 1, 0)] != e)
    xs = (x0, x1)

    def gather_copy(tile, r, slot):
        p = jnp.maximum(src_ref[tile * MOE_TM + r], 0)
        tok = jnp.where(p >= n_tok, p - n_tok, p)
        return pltpu.make_async_copy(h_hbm.at[pl.ds(tok * chunks, chunks)],
                                     xs[slot].at[pl.ds(r * chunks, chunks)], sems.at[slot])

    def wait_gather(slot):
        pltpu.make_async_copy(h_hbm.at[pl.ds(0, MOE_TM * chunks)], xs[slot], sems.at[slot]).wait()

    def weight_copies(expert, slot):
        return [pltpu.make_async_copy(w_hbm.at[expert], buf.at[slot], sems.at[2 + slot])
                for w_hbm, buf in ((w1_hbm, w1b), (w3_hbm, w3b), (w2_hbm, w2b))]

    @pl.when(i == 0)
    def _():
        slot_ref[0] = 1
        for c in weight_copies(e, 0):
            c.start()
        for r in range(MOE_TM):
            gather_copy(0, r, 0).start()

    @pl.when(live & first)
    def _():
        slot = 1 - slot_ref[0]
        slot_ref[0] = slot
        for c in weight_copies(e, slot):
            c.wait()
        end = tend_ref[e]

        @pl.when(end < n_tiles)
        def _():
            for c in weight_copies(texp_ref[end], 1 - slot):
                c.start()

    def step(cur):
        nxt = 1 - cur
        wait_gather(cur)
        for r in range(MOE_TM):
            gather_copy(i + 1, r, nxt).start()
        slot = slot_ref[0]
        x = _load_token_major(xs[cur], MOE_TM).astype(BF16)
        a = jnp.dot(x, w1b[slot].astype(BF16), preferred_element_type=F32)
        b = jnp.dot(x, w3b[slot].astype(BF16), preferred_element_type=F32)
        hid = (_silu(a) * b).astype(BF16)
        _store_token_major(y_ref, jnp.dot(hid, w2b[slot].astype(BF16), preferred_element_type=F32))

        @pl.when(i == n_tiles - 1)
        def _():
            wait_gather(nxt)

    pl.when(live & (i % 2 == 0))(lambda: step(0))
    pl.when(live & (i % 2 == 1))(lambda: step(1))

    @pl.when(jnp.logical_not(live))
    def _():
        y_ref[...] = jnp.zeros_like(y_ref)


def _gmm(src, tile_expert, tile_end, n_tiles, h2_tm, w1, w3, w2, n_tok):
    d = w1.shape[1]
    chunks = _token_pitch(d)
    steps = tile_expert.shape[0]
    hbm = pl.BlockSpec(memory_space=pl.ANY)
    return pl.pallas_call(
        functools.partial(_gmm_kernel, n_tok=n_tok, chunks=chunks),
        out_shape=jax.ShapeDtypeStruct((steps * MOE_TM * chunks, LANES), F32),
        grid_spec=pltpu.PrefetchScalarGridSpec(
            num_scalar_prefetch=4,
            grid=(steps,),
            in_specs=[hbm, hbm, hbm, hbm],
            out_specs=pl.BlockSpec((MOE_TM * chunks, LANES), lambda i, src, te, tend, nt: (i, 0)),
            scratch_shapes=[pltpu.VMEM((MOE_TM * chunks, LANES), F32)] * 2
                           + [pltpu.VMEM((2, d, D_EXPERT), F32), pltpu.VMEM((2, d, D_EXPERT), F32),
                              pltpu.VMEM((2, D_EXPERT, d), F32), pltpu.SMEM((1,), I32),
                              pltpu.SemaphoreType.DMA((4,))],
        ),
        compiler_params=_cparams(("arbitrary",)),
        name="gmm",
    )(src, tile_expert, tile_end, n_tiles, h2_tm, w1, w3, w2)


def _route_tables(ids, n_tok):
    n_pairs = n_tok * TOP_K
    max_tiles = n_pairs // MOE_TM + N_EXPERTS
    e_flat = ids[:, :TOP_K].T.reshape(n_pairs)
    onehot = (e_flat[:, None] == jnp.arange(N_EXPERTS, dtype=I32)[None, :]).astype(I32)
    csum = jnp.cumsum(onehot, axis=0)
    counts = csum[-1]
    rank = jnp.sum(csum * onehot, axis=1) - 1
    tiles_per = (counts + MOE_TM - 1) // MOE_TM
    tile_end = jnp.cumsum(tiles_per)
    row_start = (tile_end - tiles_per) * MOE_TM
    pos = row_start[e_flat] + rank
    src = jnp.full(((max_tiles + 1) * MOE_TM,), -1, I32).at[pos].set(jnp.arange(n_pairs, dtype=I32))
    steps = jnp.arange(max_tiles, dtype=I32)
    owner = jnp.sum((tile_end[None, :] <= steps[:, None]).astype(I32), axis=1)
    tile_expert = jnp.minimum(owner, N_EXPERTS - 1)
    return src, pos, tile_expert, tile_end.astype(I32), tile_end[-1:].astype(I32)


def _combine_kernel(pos_ref, x1_ref, wts_ref, mod_ref, g_ref, y_hbm, o_ref, ya0, yb0, ya1, yb1, sems,
                    *, tok0, n_tok, n_steps):
    i = pl.program_id(0)
    bufs = ((ya0, yb0), (ya1, yb1))
    chunks = ya0.shape[0] // CMB_TM

    def issue(tile, slot):
        def body(r, carry):
            for which in range(2):
                row = pos_ref[which * n_tok + tok0 + tile * CMB_TM + r]
                pltpu.make_async_copy(y_hbm.at[pl.ds(row * chunks, chunks)],
                                      bufs[slot][which].at[pl.ds(r * chunks, chunks)],
                                      sems.at[2 * slot + which]).start()
            return carry
        lax.fori_loop(0, CMB_TM, body, 0, unroll=DMA_UNROLL)

    def wait(slot):
        for which in range(2):
            pltpu.make_async_copy(y_hbm.at[pl.ds(0, CMB_TM * chunks)], bufs[slot][which],
                                  sems.at[2 * slot + which]).wait()

    @pl.when(i == 0)
    def _():
        issue(0, 0)

    def step(cur):
        nxt = 1 - cur
        wait(cur)
        issue(jnp.minimum(i + 1, n_steps - 1), nxt)
        w = wts_ref[...]
        moe = (w[:, 0:1] * _load_token_major(bufs[cur][0], CMB_TM)
               + w[:, 1:2] * _load_token_major(bufs[cur][1], CMB_TM))
        x2 = x1_ref[...] + mod_ref[0, 5:6, :] * moe
        o_ref[...] = _rms(x2) * g_ref[...]

        @pl.when(i == n_steps - 1)
        def _():
            wait(nxt)

    pl.when(i % 2 == 0)(lambda: step(0))
    pl.when(i % 2 == 1)(lambda: step(1))


def _combine(pos, x1, y_sorted, wts, mods, mod_row0, g, row0, batch, t, n_tok):
    d = x1.shape[1]
    nt = t // CMB_TM
    rb0 = row0 // CMB_TM
    n_steps = batch * nt
    return pl.pallas_call(
        functools.partial(_combine_kernel, tok0=row0, n_tok=n_tok, n_steps=n_steps),
        out_shape=jax.ShapeDtypeStruct((batch * t, d), F32),
        grid_spec=pltpu.PrefetchScalarGridSpec(
            num_scalar_prefetch=1,
            grid=(n_steps,),
            in_specs=[pl.BlockSpec((CMB_TM, d), lambda i, pos: (rb0 + i, 0)),
                      pl.BlockSpec((CMB_TM, LANES), lambda i, pos: (rb0 + i, 0)),
                      pl.BlockSpec((1, N_ADA, d), lambda i, pos: (mod_row0 + i // nt, 0, 0)),
                      pl.BlockSpec((1, d), lambda i, pos: (0, 0)),
                      pl.BlockSpec(memory_space=pl.ANY)],
            out_specs=pl.BlockSpec((CMB_TM, d), lambda i, pos: (i, 0)),
            scratch_shapes=[pltpu.VMEM((CMB_TM * _token_pitch(d), LANES), F32)] * 4
                           + [pltpu.SemaphoreType.DMA((4,))],
        ),
        compiler_params=_cparams(("arbitrary",)),
        name="combine",
    )(pos, x1, wts, mods, g.reshape(1, d), y_sorted)


def _rope_tables(n_lat):
    half = DA_DK // 2
    nf = half // 2
    t = jnp.arange(n_lat)
    row = (t // GRID_W).astype(F32)
    col = (t % GRID_W).astype(F32)
    inv = ROPE_THETA ** (-jnp.arange(nf, dtype=F32) / nf)
    lane = np.arange(HEAD_W)
    freq = inv[lane % nf]
    pos = jnp.where(((lane // half) % 2 == 0)[None, :], row[:, None], col[:, None])
    ang = pos * freq[None, :]
    sign = np.where((lane % half) < nf, -1.0, 1.0).astype(np.float32)
    return jnp.cos(ang), jnp.sin(ang) * sign[None, :]


def kernel(x_prompt, x_sample, cache_k, cache_v, state_hgrn_fwd, state_hgrn_bwd, c, c_ctx, w_ada, b_ada,
           norm1_g, norm2_g, norm_final_g, w_in, hg_lb_fwd, hg_lb_bwd, hg_norm_g, da_lambda_q1, da_lambda_k1,
           da_lambda_q2, da_lambda_k2, da_norm_g, w_out, router_g_w, router_g_b, router_e_w, router_e_b,
           exp_w1, exp_w3, exp_w2):
    l = 0
    batch, seq, d = x_prompt.shape
    dec_batch, dec_seq, _ = x_sample.shape
    n_ctx = batch * seq
    n_lat = dec_batch * dec_seq
    n_tok = n_ctx + n_lat
    lam_init = 0.8 - 0.6 * math.exp(-0.3 * l)

    cond = jnp.zeros((8, d), F32).at[0].set(c_ctx).at[1:1 + dec_batch].set(c)
    mods = _ada(cond, w_ada[l], b_ada[l]).reshape(8, N_ADA, d)

    w_in_bf16 = w_in[l].astype(BF16)
    proj_c = _in_proj(x_prompt.reshape(1, n_ctx, d), mods, 0, norm1_g[l], w_in_bf16, 1024, "in_proj_ctx")
    proj_l = _in_proj(x_sample, mods, 1, norm1_g[l], w_in_bf16, 1024, "in_proj_lat")
    lam_params = (da_lambda_q1[l], da_lambda_k1[l], da_lambda_q2[l], da_lambda_k2[l])

    da_ctx, new_k, new_v = _attn(proj_c, batch, seq, lam_params, da_norm_g[l], lam_init, emit_cache=True)
    hg_ctx, new_sf, new_sb = _hgrn(proj_c, batch, seq, hg_lb_fwd, hg_lb_bwd, hg_norm_g[l], emit_state=True)

    cos, sin = _rope_tables(dec_seq)
    (da_lat,) = _attn(proj_l, dec_batch, dec_seq, lam_params, da_norm_g[l], lam_init, cos=cos, sin=sin,
                      cache_k=cache_k[:, l:l + 1], cache_v=cache_v[:, l:l + 1])
    (hg_lat,) = _hgrn(proj_l, dec_batch, dec_seq, hg_lb_fwd, hg_lb_bwd, hg_norm_g[l],
                      s0f=state_hgrn_fwd[:, l:l + 1], s0b=state_hgrn_bwd[:, l:l + 1])

    rw = jnp.zeros((d, LANES), F32).at[:, :N_GROUPS].set(router_g_w[l]).at[:, N_GROUPS:N_GROUPS + N_EXPERTS].set(
        router_e_w[l])
    rw_hi = rw.astype(BF16)
    rw = jnp.stack([rw_hi, (rw - rw_hi.astype(F32)).astype(BF16)])
    rb = jnp.zeros((1, LANES), F32).at[0, :N_GROUPS].set(router_g_b[l]).at[0, N_GROUPS:N_GROUPS + N_EXPERTS].set(
        router_e_b[l])
    x1, h2, ids, wts = _out_proj(hg_ctx, da_ctx, x_prompt.reshape(n_ctx, d), hg_lat, da_lat,
                                 x_sample.reshape(n_lat, d), w_out[l].astype(BF16), mods, norm2_g[l], rw, rb, dec_seq)

    src, pos, tile_expert, tile_end, n_tiles = _route_tables(ids, n_tok)
    y_sorted = _gmm(src, tile_expert, tile_end, n_tiles, h2, exp_w1[l], exp_w3[l], exp_w2[l], n_tok)

    y_ctx = _combine(pos, x1, y_sorted, wts, mods, 0, norm_final_g, 0, 1, n_ctx, n_tok)
    y_lat = _combine(pos, x1, y_sorted, wts, mods, 1, norm_final_g, n_ctx, dec_batch, dec_seq, n_tok)
    return (y_ctx.reshape(batch, seq, d), y_lat.reshape(dec_batch, dec_seq, d), new_k, new_v, new_sf, new_sb)
```

```python
import functools
import math

import numpy as np
import jax
import jax.numpy as jnp
from jax import lax
from jax.experimental import pallas as pl
from jax.experimental.pallas import tpu as pltpu

F32 = jnp.float32
BF16 = jnp.bfloat16
I32 = jnp.int32

GRID_W = 64
HG_WIDTH = 1024
HG_DK = 128
HG_HEADS = 8
DA_HEADS = 8
DA_DK = 64
HEAD_W = 128
ROPE_THETA = 10000.0
N_GROUPS = 4
EXP_PER_GROUP = 8
N_EXPERTS = 32
TOP_K = 2
D_EXPERT = 512
N_ADA = 6
RMS_EPS = 1e-6
CB_Q_HG, CB_F_FW, CB_F_BW, CB_I_HG, CB_G_HG, CB_Q_DA, CB_K_DA, CB_V_DA = (8 * i for i in range(8))

LANES = 128
SUBLANES = 8
VMEM_LIMIT = 56 * 1024 * 1024

ADA_TN = 1536
IN_TN = 512
OUT_TM = 256
HG_CHUNK = 64
HG_GROUP = 4
ATT_TQ = 256
ATT_TK = 512
MOE_TM = 256
GATHER_SEMS = 4
CMB_TM = 512
DMA_UNROLL = 8
NEG_INF = float("-inf")


def _cparams(sem):
    return pltpu.CompilerParams(dimension_semantics=sem, vmem_limit_bytes=VMEM_LIMIT)


def _silu(x):
    return x * jax.nn.sigmoid(x)


def _rms(x):
    return x * lax.rsqrt(jnp.mean(x * x, axis=-1, keepdims=True) + RMS_EPS)


def _token_pitch(d):
    return d // LANES + 1


def _store_token_major(ref, x):
    rows, d = x.shape
    pitch = _token_pitch(d)
    for c in range(pitch - 1):
        ref[pl.ds(c, rows, stride=pitch), :] = x[:, c * LANES:(c + 1) * LANES]
    ref[pl.ds(pitch - 1, rows, stride=pitch), :] = jnp.zeros((rows, LANES), x.dtype)


def _load_token_major(ref, rows):
    pitch = ref.shape[0] // rows
    return jnp.concatenate([ref[pl.ds(c, rows, stride=pitch), :] for c in range(pitch - 1)], axis=1)


def _ada_kernel(cond_ref, w_ref, b_ref, o_ref):
    s = _silu(cond_ref[...]).astype(BF16)
    o_ref[...] = jnp.dot(s, w_ref[...].astype(BF16), preferred_element_type=F32) + b_ref[...]


def _ada(cond, w, b):
    rows, d = cond.shape
    n = w.shape[1]
    return pl.pallas_call(
        _ada_kernel,
        out_shape=jax.ShapeDtypeStruct((rows, n), F32),
        grid=(n // ADA_TN,),
        in_specs=[pl.BlockSpec((rows, d), lambda j: (0, 0)),
                  pl.BlockSpec((d, ADA_TN), lambda j: (0, j)),
                  pl.BlockSpec((1, ADA_TN), lambda j: (0, j))],
        out_specs=pl.BlockSpec((rows, ADA_TN), lambda j: (0, j)),
        compiler_params=_cparams(("arbitrary",)),
        name="ada",
    )(cond, w, b.reshape(1, n))


def _in_kernel(x_ref, mod_ref, g_ref, w_ref, o_ref, h_ref):
    @pl.when(pl.program_id(1) == 0)
    def _():
        y = _rms(x_ref[...]) * g_ref[...]
        h_ref[...] = (y * (1.0 + mod_ref[0, 1:2, :]) + mod_ref[0, 0:1, :]).astype(BF16)

    o_ref[...] = jnp.dot(h_ref[...], w_ref[...], preferred_element_type=F32)


def _in_proj(x, mods, mod_row0, g, w_bf16, tm, name):
    batch, t, d = x.shape
    cols = w_bf16.shape[1]
    nt = t // tm
    return pl.pallas_call(
        _in_kernel,
        out_shape=jax.ShapeDtypeStruct((batch * t, cols), F32),
        grid=(batch * nt, cols // IN_TN),
        in_specs=[pl.BlockSpec((tm, d), lambda i, j: (i, 0)),
                  pl.BlockSpec((1, N_ADA, d), lambda i, j: (mod_row0 + i // nt, 0, 0)),
                  pl.BlockSpec((1, d), lambda i, j: (0, 0)),
                  pl.BlockSpec((d, IN_TN), lambda i, j: (0, j))],
        out_specs=pl.BlockSpec((tm, IN_TN), lambda i, j: (i, j)),
        scratch_shapes=[pltpu.VMEM((tm, d), BF16)],
        compiler_params=_cparams(("arbitrary", "arbitrary")),
        name=name,
    )(x.reshape(batch * t, d), mods, g.reshape(1, d), w_bf16)


def _rope(x, cos, sin_signed):
    lane = lax.broadcasted_iota(I32, x.shape, 1)
    first = (lane % 32) < 16
    partner = jnp.where(first, pltpu.roll(x, LANES - 16, 1), pltpu.roll(x, 16, 1))
    return x * cos + partner * sin_signed


def _split_maps(k):
    lane = lax.broadcasted_iota(I32, k.shape, 1)
    m1 = lane < DA_DK
    return jnp.where(m1, k, 0.0).astype(BF16), jnp.where(m1, 0.0, k).astype(BF16)


def _lower_bound(lb_ref):
    p = lb_ref[...]
    e = jnp.exp(p - jnp.max(p, axis=0, keepdims=True))
    return e[0:1, :] / jnp.sum(e, axis=0, keepdims=True)


def _hgrn_pair_masks(mask_ref, c):
    ti = lax.broadcasted_iota(I32, (c, c), 0)
    si = lax.broadcasted_iota(I32, (c, c), 1)
    for d in range(2):
        mask_ref[d, 0] = (ti == si).astype(F32)
        for j in range(c.bit_length() - 1):
            same = (ti >> (j + 1)) == (si >> (j + 1))
            t_bit = ((ti >> j) & 1) == 1
            s_bit = ((si >> j) & 1) == 1
            pair = (s_bit & jnp.logical_not(t_bit)) if d else (t_bit & jnp.logical_not(s_bit))
            mask_ref[d, 1 + j] = (same & pair).astype(F32)


def _hgrn_chunk(q, f_logit, v, lb, mask_ref, rev):
    c = q.shape[0]
    n_piece = c // SUBLANES
    levels = c.bit_length() - 1
    low = SUBLANES.bit_length() - 1
    d = 1 if rev else 0
    nt = (((1,), (1,)), ((), ()))
    pieces = lambda x: [x[SUBLANES * i:SUBLANES * (i + 1)] for i in range(n_piece)]
    whole = lambda xs: jnp.concatenate(xs, axis=0)

    fg = lb + (1.0 - lb) * jax.nn.sigmoid(f_logit)
    k = 1.0 - fg
    q_bf = q.astype(BF16)
    k_bf = k.astype(BF16)
    qs, ks, fgs = pieces(q), pieces(k), pieces(fg)
    tot = pieces(jnp.log(fg))
    pre = list(tot)
    suf = [jnp.zeros((SUBLANES, LANES), F32)] * n_piece
    sub = lax.broadcasted_iota(I32, (SUBLANES, LANES), 0)

    def pair_scores(qt, kt, idx):
        return lax.dot_general(qt, kt, nt, preferred_element_type=F32) * mask_ref[d, idx]

    scores = pair_scores(q_bf, k_bf, 0)
    yield
    for j in range(levels):
        if j < low:
            sh = 1 << j
            bit = ((sub >> j) & 1) == 1
            late = jnp.logical_not(bit) if rev else bit
            if j == 0:
                qt = whole([qs[i] * jnp.where(late, fgs[i], 1.0) for i in range(n_piece)]).astype(BF16)
                kt = k_bf
            else:
                es = [jnp.exp(jnp.where(late, pre[i], suf[i])) for i in range(n_piece)]
                qt = whole([qs[i] * es[i] for i in range(n_piece)]).astype(BF16)
                kt = whole([ks[i] * es[i] for i in range(n_piece)]).astype(BF16)
            for i in range(n_piece):
                up = pltpu.roll(tot[i], sh, 0)
                dn = pltpu.roll(tot[i], SUBLANES - sh, 0)
                sib = jnp.where(late, dn, up) if rev else jnp.where(late, up, dn)
                pre[i] = pre[i] + jnp.where(late, sib, 0.0)
                suf[i] = suf[i] + jnp.where(late, 0.0, sib)
                tot[i] = tot[i] + sib
        else:
            half = 1 << (j - low)
            upper = [(i // half) % 2 == 1 for i in range(n_piece)]
            late = [(not u) if rev else u for u in upper]
            es = [jnp.exp(pre[i] if late[i] else suf[i]) for i in range(n_piece)]
            qt = whole([qs[i] * es[i] for i in range(n_piece)]).astype(BF16)
            kt = whole([ks[i] * es[i] for i in range(n_piece)]).astype(BF16)
            sib = [tot[i - half] if upper[i] else tot[i + half] for i in range(n_piece)]
            pre = [pre[i] + sib[i] if late[i] else pre[i] for i in range(n_piece)]
            suf = [suf[i] if late[i] else suf[i] + sib[i] for i in range(n_piece)]
            tot = [tot[i] + sib[i] for i in range(n_piece)]
        scores = scores + pair_scores(qt, kt, 1 + j)
        yield

    q_dec = whole([qs[i] * jnp.exp(pre[i]) for i in range(n_piece)]).astype(BF16)
    k_dec = whole([ks[i] * jnp.exp(suf[i]) for i in range(n_piece)]).astype(BF16)
    return dict(q_dec=q_dec, k_dec=k_dec, v_t=v.T.astype(BF16), v=v.astype(BF16),
                scores=scores.astype(BF16), decay=jnp.exp(tot[0][0:1, :]))


def _hgrn_state_step(p, st_ref):
    st = st_ref[...]
    o = lax.dot_general(p["q_dec"], st.astype(BF16), (((1,), (1,)), ((), ())), preferred_element_type=F32)
    st_ref[...] = p["decay"] * st + jnp.dot(p["v_t"], p["k_dec"], preferred_element_type=F32)
    return o


def _lockstep(gens):
    results = [None] * len(gens)
    live = list(range(len(gens)))
    while live:
        for idx in list(live):
            try:
                next(gens[idx])
            except StopIteration as stop:
                results[idx] = stop.value
                live.remove(idx)
    return results


def _hgrn_kernel(*refs, t, has_init, emit_state):
    q_ref, ff_ref, fb_ref, i_ref, g_ref, lbf_ref, lbb_ref, ng_ref = refs[:8]
    pos = 8
    if has_init:
        s0f_ref, s0b_ref = refs[pos:pos + 2]
        pos += 2
    o_ref = refs[pos]
    pos += 1
    if emit_state:
        sf_ref, sb_ref = refs[pos:pos + 2]
        pos += 2
    of_scr, ob_scr, stf, stb, mask_scr = refs[pos:pos + 5]

    c = HG_CHUNK
    n = t // c
    _hgrn_pair_masks(mask_scr, c)
    if has_init:
        stf[...] = s0f_ref[0, 0, 0].T
        stb[...] = s0b_ref[0, 0, 0].T
    else:
        stf[...] = jnp.zeros_like(stf)
        stb[...] = jnp.zeros_like(stb)
    lbf = _lower_bound(lbf_ref)
    lbb = _lower_bound(lbb_ref)

    def body(i, carry):
        work = []
        for u in range(HG_GROUP):
            ci = i * HG_GROUP + u
            work.append((pl.ds(pl.multiple_of(ci * c, c), c), ff_ref, lbf, stf, of_scr, False))
            work.append((pl.ds(pl.multiple_of((n - 1 - ci) * c, c), c), fb_ref, lbb, stb, ob_scr, True))
        parts = _lockstep([_hgrn_chunk(_silu(q_ref[sl, :]), f_ref[sl, :], i_ref[sl, :], lb, mask_scr, rev)
                           for sl, f_ref, lb, _, _, rev in work])
        outs = [_hgrn_state_step(p, w[3]) for p, w in zip(parts, work)]
        for p, w, o in zip(parts, work, outs):
            w[4][w[0], :] = o + jnp.dot(p["scores"], p["v"], preferred_element_type=F32)
        return carry

    lax.fori_loop(0, n // HG_GROUP, body, 0)

    fin = min(t, 256)

    def finish(i, carry):
        sl = pl.ds(pl.multiple_of(i * fin, fin), fin)
        o = _rms(of_scr[sl, :] + ob_scr[sl, :]) * ng_ref[...]
        o_ref[sl, :] = (o * _silu(g_ref[sl, :])).astype(BF16)
        return carry

    lax.fori_loop(0, t // fin, finish, 0)
    if emit_state:
        sf_ref[0, 0, 0] = stf[...].T
        sb_ref[0, 0, 0] = stb[...].T


def _hgrn(proj, batch, t, lb_fwd, lb_bwd, norm_g, s0f=None, s0b=None, emit_state=False):
    has_init = s0f is not None

    def col(cb):
        return pl.BlockSpec((t, HEAD_W), lambda b, h: (b, cb + h))

    lb_spec = pl.BlockSpec((lb_fwd.shape[0], HEAD_W), lambda b, h: (0, h))
    in_specs = [col(CB_Q_HG), col(CB_F_FW), col(CB_F_BW), col(CB_I_HG), col(CB_G_HG),
                lb_spec, lb_spec, pl.BlockSpec((1, HEAD_W), lambda b, h: (0, 0))]
    args = [proj] * 5 + [lb_fwd, lb_bwd, norm_g.reshape(1, HEAD_W)]
    st_spec = pl.BlockSpec((1, 1, 1, HG_DK, HEAD_W), lambda b, h: (b, 0, h, 0, 0))
    if has_init:
        in_specs += [st_spec, st_spec]
        args += [s0f, s0b]
    out_shape = [jax.ShapeDtypeStruct((batch * t, HG_WIDTH), BF16)]
    out_specs = [pl.BlockSpec((t, HEAD_W), lambda b, h: (b, h))]
    if emit_state:
        st = jax.ShapeDtypeStruct((batch, 1, HG_HEADS, HG_DK, HEAD_W), F32)
        out_shape += [st, st]
        out_specs += [st_spec, st_spec]
    return pl.pallas_call(
        functools.partial(_hgrn_kernel, t=t, has_init=has_init, emit_state=emit_state),
        out_shape=out_shape,
        grid=(batch, HG_HEADS),
        in_specs=in_specs,
        out_specs=out_specs,
        scratch_shapes=[pltpu.VMEM((t, HEAD_W), F32), pltpu.VMEM((t, HEAD_W), F32),
                        pltpu.VMEM((HEAD_W, HG_DK), F32), pltpu.VMEM((HEAD_W, HG_DK), F32),
                        pltpu.VMEM((2, HG_CHUNK.bit_length(), HG_CHUNK, HG_CHUNK), F32)],
        compiler_params=_cparams(("arbitrary", "arbitrary")),
        name="hgrn_lat" if has_init else "hgrn_ctx",
    )(*args)


def _attn_kernel(*refs, t, n_cache, use_rope, emit_cache, lam_init):
    it = iter(refs)
    q_ref, k_ref, v_ref = next(it), next(it), next(it)
    if use_rope:
        cosq_ref, sinq_ref, cos_ref, sin_ref = next(it), next(it), next(it), next(it)
    if n_cache:
        ck_ref, cv_ref = next(it), next(it)
    lq1_ref, lk1_ref, lq2_ref, lk2_ref, ng_ref, o_ref = (next(it) for _ in range(6))
    if emit_cache:
        nk_ref, nv_ref = next(it), next(it)
    k1_scr, k2_scr, v_scr, s_scr, p_scr = (next(it) for _ in range(5))
    k_scrs = (k1_scr, k2_scr)
    tq = q_ref.shape[0]

    @pl.when(pl.program_id(2) == 0)
    def _():
        rows = min(t, ATT_TK)

        def stage(i, carry):
            sl = pl.ds(pl.multiple_of(i * rows, rows), rows)
            k = k_ref[sl, :]
            v = v_ref[sl, :]
            if emit_cache:
                nk_ref[0, 0, 0, sl, :] = k
                nv_ref[0, 0, 0, sl, :] = v
            if use_rope:
                k = _rope(k, cos_ref[sl, :], sin_ref[sl, :])
            k1_scr[sl, :], k2_scr[sl, :] = _split_maps(k)
            v_scr[sl, :] = v.astype(BF16)
            return carry

        lax.fori_loop(0, t // rows, stage, 0)
        if n_cache:
            k1_scr[t:t + n_cache, :], k2_scr[t:t + n_cache, :] = _split_maps(ck_ref[0, 0, 0])
            v_scr[t:t + n_cache, :] = cv_ref[0, 0, 0].astype(BF16)

    q = q_ref[...]
    if use_rope:
        q = _rope(q, cosq_ref[...], sinq_ref[...])
    q = (q * (DA_DK ** -0.5)).astype(BF16)
    nt = (((1,), (1,)), ((), ()))
    n_keys = t + n_cache
    tiles = [(st, min(ATT_TK, n_keys - st)) for st in range(0, n_keys, ATT_TK)]

    def scores_tile(mp, st, sz, mx):
        s = lax.dot_general(q, k_scrs[mp][st:st + sz, :], nt, preferred_element_type=F32)
        s_scr[mp, :, st:st + sz] = s
        for j in range(sz // LANES):
            mx = jnp.maximum(mx, s[:, j * LANES:(j + 1) * LANES])
        return mx

    def exp_tile(mp, st, sz, m, part):
        for lo in range(st, st + sz, LANES):
            e = jnp.exp(s_scr[mp, :, lo:lo + LANES] - m)
            part = part + e
            p_scr[mp, :, lo:lo + LANES] = e.astype(BF16)
        return part

    def value_tile(mp, st, sz, acc):
        return acc + jnp.dot(p_scr[mp, :, st:st + sz], v_scr[st:st + sz, :], preferred_element_type=F32)

    def row_stat(x, op):
        return jnp.broadcast_to(op(x, axis=-1, keepdims=True), (tq, LANES))

    neg = jnp.full((tq, LANES), NEG_INF, F32)
    zero = jnp.zeros((tq, LANES), F32)
    mx = neg
    for st, sz in tiles:
        mx = scores_tile(0, st, sz, mx)
    m0 = row_stat(mx, jnp.max)
    mx, part0 = neg, zero
    for st, sz in tiles:
        mx = scores_tile(1, st, sz, mx)
        part0 = exp_tile(0, st, sz, m0, part0)
    m1 = row_stat(mx, jnp.max)
    acc0, part1 = zero, zero
    for st, sz in tiles:
        acc0 = value_tile(0, st, sz, acc0)
        part1 = exp_tile(1, st, sz, m1, part1)
    acc1 = jnp.dot(p_scr[1], v_scr[...], preferred_element_type=F32)

    lam = (jnp.exp(jnp.sum(lq1_ref[...] * lk1_ref[...], axis=-1, keepdims=True))
           - jnp.exp(jnp.sum(lq2_ref[...] * lk2_ref[...], axis=-1, keepdims=True)) + lam_init)
    o = acc0 / row_stat(part0, jnp.sum) - lam * (acc1 / row_stat(part1, jnp.sum))
    o_ref[...] = (_rms(o) * ng_ref[...] * (1.0 - lam_init)).astype(BF16)


def _attn(proj, batch, t, lam_params, norm_g, lam_init, cos=None, sin=None, cache_k=None, cache_v=None,
          emit_cache=False):
    use_rope = cos is not None
    n_cache = 0 if cache_k is None else cache_k.shape[3]
    tq = min(ATT_TQ, t)
    nq = t // tq
    n_keys = t + n_cache

    def col(cb, rows, row_map):
        return pl.BlockSpec((rows, HEAD_W), lambda b, h, i: (row_map(b, i), cb + h))

    in_specs = [col(CB_Q_DA, tq, lambda b, i: b * nq + i),
                col(CB_K_DA, t, lambda b, i: b), col(CB_V_DA, t, lambda b, i: b)]
    args = [proj, proj, proj]
    if use_rope:
        in_specs += [pl.BlockSpec((tq, HEAD_W), lambda b, h, i: (i, 0))] * 2
        in_specs += [pl.BlockSpec((t, HEAD_W), lambda b, h, i: (0, 0))] * 2
        args += [cos, sin, cos, sin]
    cache_spec = lambda n: pl.BlockSpec((1, 1, 1, n, HEAD_W), lambda b, h, i: (b, 0, h, 0, 0))
    if n_cache:
        in_specs += [cache_spec(n_cache)] * 2
        args += [cache_k, cache_v]
    small = pl.BlockSpec((1, DA_DK), lambda b, h, i: (0, 0))
    in_specs += [small] * 4 + [pl.BlockSpec((1, HEAD_W), lambda b, h, i: (0, 0))]
    args += [p.reshape(1, DA_DK) for p in lam_params] + [norm_g.reshape(1, HEAD_W)]
    out_shape = [jax.ShapeDtypeStruct((batch * t, DA_HEADS * HEAD_W), BF16)]
    out_specs = [pl.BlockSpec((tq, HEAD_W), lambda b, h, i: (b * nq + i, h))]
    if emit_cache:
        out_shape += [jax.ShapeDtypeStruct((batch, 1, DA_HEADS, t, HEAD_W), F32)] * 2
        out_specs += [cache_spec(t)] * 2
    scratch = [pltpu.VMEM((n_keys, HEAD_W), BF16)] * 3
    scratch += [pltpu.VMEM((2, tq, n_keys), F32), pltpu.VMEM((2, tq, n_keys), BF16)]
    return pl.pallas_call(
        functools.partial(_attn_kernel, t=t, n_cache=n_cache, use_rope=use_rope, emit_cache=emit_cache,
                          lam_init=lam_init),
        out_shape=out_shape,
        grid=(batch, DA_HEADS, nq),
        in_specs=in_specs,
        out_specs=out_specs,
        scratch_shapes=scratch,
        compiler_params=_cparams(("arbitrary",) * 3),
        name="attn_lat" if use_rope else "attn_ctx",
    )(*args)


def _out_kernel(mhg_c, mda_c, x_c, mhg_l, mda_l, x_l, w_ref, mod_ref, g_ref, rw_ref, rb_ref,
                x1_ref, h2_ref, ids_ref, wts_ref, *, ctx_tiles):
    def body(mhg_ref, mda_ref, x_ref):
        mix = jnp.dot(mhg_ref[...], w_ref[0:HG_WIDTH, :], preferred_element_type=F32)
        mix = mix + jnp.dot(mda_ref[...], w_ref[HG_WIDTH:, :], preferred_element_type=F32)
        x1 = x_ref[...] + mod_ref[0, 2:3, :] * mix
        x1_ref[...] = x1
        h2 = (_rms(x1) * g_ref[...]) * (1.0 + mod_ref[0, 4:5, :]) + mod_ref[0, 3:4, :]
        _store_token_major(h2_ref, h2)

        h_hi = h2.astype(BF16)
        h_lo = (h2 - h_hi.astype(F32)).astype(BF16)
        logit = (jnp.dot(h_hi, rw_ref[0], preferred_element_type=F32)
                 + jnp.dot(h_lo, rw_ref[0], preferred_element_type=F32)
                 + jnp.dot(h_hi, rw_ref[1], preferred_element_type=F32)) + rb_ref[...]
        lane = lax.broadcasted_iota(I32, logit.shape, 1)

        def first_max(x):
            m = jnp.max(x, axis=-1, keepdims=True)
            return m, jnp.min(jnp.where(x == m, lane, LANES), axis=-1, keepdims=True)

        gmask = lane < N_GROUPS
        gmax, gsel = first_max(jnp.where(gmask, logit, NEG_INF))
        p_grp = 1.0 / jnp.sum(jnp.where(gmask, jnp.exp(logit - gmax), 0.0), axis=-1, keepdims=True)
        lo = N_GROUPS + EXP_PER_GROUP * gsel
        le = jnp.where((lane >= lo) & (lane < lo + EXP_PER_GROUP), logit, NEG_INF)
        v1, i1 = first_max(le)
        v2, i2 = first_max(jnp.where(lane == i1, NEG_INF, le))
        e = jnp.exp(v2 - v1)
        w1 = p_grp / (1.0 + e)
        w2 = p_grp * e / (1.0 + e)
        ids_ref[...] = jnp.where(lane == 0, i1 - N_GROUPS, jnp.where(lane == 1, i2 - N_GROUPS, 0))
        wts_ref[...] = jnp.where(lane == 0, w1, jnp.where(lane == 1, w2, 0.0))

    i = pl.program_id(0)
    pl.when(i < ctx_tiles)(lambda: body(mhg_c, mda_c, x_c))
    pl.when(i >= ctx_tiles)(lambda: body(mhg_l, mda_l, x_l))


def _out_proj(mhg_c, mda_c, x_c, mhg_l, mda_l, x_l, w_bf16, mods, g, rw, rb, lat_t):
    n_ctx, d = x_c.shape
    n_lat = x_l.shape[0]
    n = n_ctx + n_lat
    ctx_tiles = n_ctx // OUT_TM
    lat_tiles = lat_t // OUT_TM
    row = lambda i: (i, 0)
    const = lambda i: (0, 0)
    crow = lambda i: (jnp.minimum(i, ctx_tiles - 1), 0)
    lrow = lambda i: (jnp.maximum(i - ctx_tiles, 0), 0)
    seg = lambda i: (jnp.where(i < ctx_tiles, 0, 1 + (i - ctx_tiles) // lat_tiles), 0, 0)
    half = lambda m: pl.BlockSpec((OUT_TM, HG_WIDTH), m)
    full = lambda m: pl.BlockSpec((OUT_TM, d), m)
    return pl.pallas_call(
        functools.partial(_out_kernel, ctx_tiles=ctx_tiles),
        out_shape=[jax.ShapeDtypeStruct((n, d), F32), jax.ShapeDtypeStruct((n * _token_pitch(d), LANES), F32),
                   jax.ShapeDtypeStruct((n, LANES), I32), jax.ShapeDtypeStruct((n, LANES), F32)],
        grid=(n // OUT_TM,),
        in_specs=[half(crow), half(crow), full(crow), half(lrow), half(lrow), full(lrow),
                  pl.BlockSpec((d, d), const), pl.BlockSpec((1, N_ADA, d), seg), pl.BlockSpec((1, d), const),
                  pl.BlockSpec((2, d, LANES), lambda i: (0, 0, 0)), pl.BlockSpec((1, LANES), const)],
        out_specs=[full(row), pl.BlockSpec((OUT_TM * _token_pitch(d), LANES), row),
                   pl.BlockSpec((OUT_TM, LANES), row), pl.BlockSpec((OUT_TM, LANES), row)],
        compiler_params=_cparams(("arbitrary",)),
        name="out_proj",
    )(mhg_c, mda_c, x_c, mhg_l, mda_l, x_l, w_bf16, mods, g.reshape(1, d), rw, rb)


def _gmm_kernel(src_ref, texp_ref, tend_ref, ntile_ref, h_hbm, w1_hbm, w3_hbm, w2_hbm, y_ref,
                x0, x1, w1b, w3b, w2b, slot_ref, sems, *, n_tok, chunks):
    i = pl.program_id(0)
    n_tiles = ntile_ref[0]
    live = i < n_tiles
    e = texp_ref[i]
    first = (i == 0) | (texp_ref[jnp.maximum(i - 1, 0)] != e)
    xs = (x0, x1)

    def gather_copy(tile, r, slot):
        p = jnp.maximum(src_ref[tile * MOE_TM + r], 0)
        tok = jnp.where(p >= n_tok, p - n_tok, p)
        return pltpu.make_async_copy(h_hbm.at[pl.ds(tok * chunks, chunks)],
                                     xs[slot].at[pl.ds(r * chunks, chunks)],
                                     sems.at[slot * GATHER_SEMS + r % GATHER_SEMS])

    def wait_gather(slot):
        part = MOE_TM // GATHER_SEMS * chunks
        for s in range(GATHER_SEMS):
            pltpu.make_async_copy(h_hbm.at[pl.ds(0, part)], xs[slot].at[pl.ds(0, part)],
                                  sems.at[slot * GATHER_SEMS + s]).wait()

    def weight_copies(expert, slot):
        return [pltpu.make_async_copy(w_hbm.at[expert], buf.at[slot], sems.at[2 * GATHER_SEMS + slot])
                for w_hbm, buf in ((w1_hbm, w1b), (w3_hbm, w3b), (w2_hbm, w2b))]

    @pl.when(i == 0)
    def _():
        slot_ref[0] = 1
        for c in weight_copies(e, 0):
            c.start()
        for r in range(MOE_TM):
            gather_copy(0, r, 0).start()

    @pl.when(live & first)
    def _():
        slot = 1 - slot_ref[0]
        slot_ref[0] = slot
        for c in weight_copies(e, slot):
            c.wait()
        end = tend_ref[e]

        @pl.when(end < n_tiles)
        def _():
            for c in weight_copies(texp_ref[end], 1 - slot):
                c.start()

    def step(cur):
        nxt = 1 - cur
        wait_gather(cur)
        for r in range(MOE_TM):
            gather_copy(i + 1, r, nxt).start()
        slot = slot_ref[0]
        x = _load_token_major(xs[cur], MOE_TM).astype(BF16)
        a = jnp.dot(x, w1b[slot].astype(BF16), preferred_element_type=F32)
        b = jnp.dot(x, w3b[slot].astype(BF16), preferred_element_type=F32)
        hid = (_silu(a) * b).astype(BF16)
        _store_token_major(y_ref, jnp.dot(hid, w2b[slot].astype(BF16), preferred_element_type=F32))

        @pl.when(i == n_tiles - 1)
        def _():
            wait_gather(nxt)

    pl.when(live & (i % 2 == 0))(lambda: step(0))
    pl.when(live & (i % 2 == 1))(lambda: step(1))

    @pl.when(jnp.logical_not(live))
    def _():
        y_ref[...] = jnp.zeros_like(y_ref)


def _gmm(src, tile_expert, tile_end, n_tiles, h2_tm, w1, w3, w2, n_tok):
    d = w1.shape[1]
    chunks = _token_pitch(d)
    steps = tile_expert.shape[0]
    hbm = pl.BlockSpec(memory_space=pl.ANY)
    return pl.pallas_call(
        functools.partial(_gmm_kernel, n_tok=n_tok, chunks=chunks),
        out_shape=jax.ShapeDtypeStruct((steps * MOE_TM * chunks, LANES), F32),
        grid_spec=pltpu.PrefetchScalarGridSpec(
            num_scalar_prefetch=4,
            grid=(steps,),
            in_specs=[hbm, hbm, hbm, hbm],
            out_specs=pl.BlockSpec((MOE_TM * chunks, LANES), lambda i, src, te, tend, nt: (i, 0)),
            scratch_shapes=[pltpu.VMEM((MOE_TM * chunks, LANES), F32)] * 2
                           + [pltpu.VMEM((2, d, D_EXPERT), F32), pltpu.VMEM((2, d, D_EXPERT), F32),
                              pltpu.VMEM((2, D_EXPERT, d), F32), pltpu.SMEM((1,), I32),
                              pltpu.SemaphoreType.DMA((2 * GATHER_SEMS + 2,))],
        ),
        compiler_params=_cparams(("arbitrary",)),
        name="gmm",
    )(src, tile_expert, tile_end, n_tiles, h2_tm, w1, w3, w2)


def _route_tables(ids, n_tok):
    n_pairs = n_tok * TOP_K
    max_tiles = n_pairs // MOE_TM + N_EXPERTS
    e_flat = ids[:, :TOP_K].T.reshape(n_pairs)
    onehot = (e_flat[:, None] == jnp.arange(N_EXPERTS, dtype=I32)[None, :]).astype(I32)
    csum = jnp.cumsum(onehot, axis=0)
    counts = csum[-1]
    rank = jnp.sum(csum * onehot, axis=1) - 1
    tiles_per = (counts + MOE_TM - 1) // MOE_TM
    tile_end = jnp.cumsum(tiles_per)
    row_start = (tile_end - tiles_per) * MOE_TM
    pos = row_start[e_flat] + rank
    src = jnp.full(((max_tiles + 1) * MOE_TM,), -1, I32).at[pos].set(jnp.arange(n_pairs, dtype=I32))
    steps = jnp.arange(max_tiles, dtype=I32)
    owner = jnp.sum((tile_end[None, :] <= steps[:, None]).astype(I32), axis=1)
    tile_expert = jnp.minimum(owner, N_EXPERTS - 1)
    return src, pos, tile_expert, tile_end.astype(I32), tile_end[-1:].astype(I32)


def _combine_kernel(pos_ref, x1_ref, wts_ref, mod_ref, g_ref, y_hbm, o_ref, ya0, yb0, ya1, yb1, sems,
                    *, tok0, n_tok, n_steps):
    i = pl.program_id(0)
    bufs = ((ya0, yb0), (ya1, yb1))
    chunks = ya0.shape[0] // CMB_TM

    def issue(tile, slot):
        def body(r, carry):
            for which in range(2):
                row = pos_ref[which * n_tok + tok0 + tile * CMB_TM + r]
                pltpu.make_async_copy(y_hbm.at[pl.ds(row * chunks, chunks)],
                                      bufs[slot][which].at[pl.ds(r * chunks, chunks)],
                                      sems.at[(2 * slot + which) * GATHER_SEMS + r % GATHER_SEMS]).start()
            return carry
        lax.fori_loop(0, CMB_TM, body, 0, unroll=DMA_UNROLL)

    def wait(slot):
        part = CMB_TM // GATHER_SEMS * chunks
        for which in range(2):
            for s in range(GATHER_SEMS):
                pltpu.make_async_copy(y_hbm.at[pl.ds(0, part)], bufs[slot][which].at[pl.ds(0, part)],
                                      sems.at[(2 * slot + which) * GATHER_SEMS + s]).wait()

    @pl.when(i == 0)
    def _():
        issue(0, 0)

    def step(cur):
        nxt = 1 - cur
        wait(cur)
        issue(jnp.minimum(i + 1, n_steps - 1), nxt)
        w = wts_ref[...]
        moe = (w[:, 0:1] * _load_token_major(bufs[cur][0], CMB_TM)
               + w[:, 1:2] * _load_token_major(bufs[cur][1], CMB_TM))
        x2 = x1_ref[...] + mod_ref[0, 5:6, :] * moe
        o_ref[...] = _rms(x2) * g_ref[...]

        @pl.when(i == n_steps - 1)
        def _():
            wait(nxt)

    pl.when(i % 2 == 0)(lambda: step(0))
    pl.when(i % 2 == 1)(lambda: step(1))


def _combine(pos, x1, y_sorted, wts, mods, mod_row0, g, row0, batch, t, n_tok):
    d = x1.shape[1]
    nt = t // CMB_TM
    rb0 = row0 // CMB_TM
    n_steps = batch * nt
    return pl.pallas_call(
        functools.partial(_combine_kernel, tok0=row0, n_tok=n_tok, n_steps=n_steps),
        out_shape=jax.ShapeDtypeStruct((batch * t, d), F32),
        grid_spec=pltpu.PrefetchScalarGridSpec(
            num_scalar_prefetch=1,
            grid=(n_steps,),
            in_specs=[pl.BlockSpec((CMB_TM, d), lambda i, pos: (rb0 + i, 0)),
                      pl.BlockSpec((CMB_TM, LANES), lambda i, pos: (rb0 + i, 0)),
                      pl.BlockSpec((1, N_ADA, d), lambda i, pos: (mod_row0 + i // nt, 0, 0)),
                      pl.BlockSpec((1, d), lambda i, pos: (0, 0)),
                      pl.BlockSpec(memory_space=pl.ANY)],
            out_specs=pl.BlockSpec((CMB_TM, d), lambda i, pos: (i, 0)),
            scratch_shapes=[pltpu.VMEM((CMB_TM * _token_pitch(d), LANES), F32)] * 4
                           + [pltpu.SemaphoreType.DMA((4 * GATHER_SEMS,))],
        ),
        compiler_params=_cparams(("arbitrary",)),
        name="combine",
    )(pos, x1, wts, mods, g.reshape(1, d), y_sorted)


def _rope_tables(n_lat):
    half = DA_DK // 2
    nf = half // 2
    t = jnp.arange(n_lat)
    row = (t // GRID_W).astype(F32)
    col = (t % GRID_W).astype(F32)
    inv = ROPE_THETA ** (-jnp.arange(nf, dtype=F32) / nf)
    lane = np.arange(HEAD_W)
    freq = inv[lane % nf]
    pos = jnp.where(((lane // half) % 2 == 0)[None, :], row[:, None], col[:, None])
    ang = pos * freq[None, :]
    sign = np.where((lane % half) < nf, -1.0, 1.0).astype(np.float32)
    return jnp.cos(ang), jnp.sin(ang) * sign[None, :]


def kernel(x_prompt, x_sample, cache_k, cache_v, state_hgrn_fwd, state_hgrn_bwd, c, c_ctx, w_ada, b_ada,
           norm1_g, norm2_g, norm_final_g, w_in, hg_lb_fwd, hg_lb_bwd, hg_norm_g, da_lambda_q1, da_lambda_k1,
           da_lambda_q2, da_lambda_k2, da_norm_g, w_out, router_g_w, router_g_b, router_e_w, router_e_b,
           exp_w1, exp_w3, exp_w2):
    l = 0
    batch, seq, d = x_prompt.shape
    dec_batch, dec_seq, _ = x_sample.shape
    n_ctx = batch * seq
    n_lat = dec_batch * dec_seq
    n_tok = n_ctx + n_lat
    lam_init = 0.8 - 0.6 * math.exp(-0.3 * l)

    cond = jnp.zeros((8, d), F32).at[0].set(c_ctx).at[1:1 + dec_batch].set(c)
    mods = _ada(cond, w_ada[l], b_ada[l]).reshape(8, N_ADA, d)

    w_in_bf16 = w_in[l].astype(BF16)
    proj_c = _in_proj(x_prompt.reshape(1, n_ctx, d), mods, 0, norm1_g[l], w_in_bf16, 1024, "in_proj_ctx")
    proj_l = _in_proj(x_sample, mods, 1, norm1_g[l], w_in_bf16, 1024, "in_proj_lat")
    lam_params = (da_lambda_q1[l], da_lambda_k1[l], da_lambda_q2[l], da_lambda_k2[l])

    da_ctx, new_k, new_v = _attn(proj_c, batch, seq, lam_params, da_norm_g[l], lam_init, emit_cache=True)
    hg_ctx, new_sf, new_sb = _hgrn(proj_c, batch, seq, hg_lb_fwd, hg_lb_bwd, hg_norm_g[l], emit_state=True)

    cos, sin = _rope_tables(dec_seq)
    (da_lat,) = _attn(proj_l, dec_batch, dec_seq, lam_params, da_norm_g[l], lam_init, cos=cos, sin=sin,
                      cache_k=cache_k[:, l:l + 1], cache_v=cache_v[:, l:l + 1])
    (hg_lat,) = _hgrn(proj_l, dec_batch, dec_seq, hg_lb_fwd, hg_lb_bwd, hg_norm_g[l],
                      s0f=state_hgrn_fwd[:, l:l + 1], s0b=state_hgrn_bwd[:, l:l + 1])

    rw = jnp.zeros((d, LANES), F32).at[:, :N_GROUPS].set(router_g_w[l]).at[:, N_GROUPS:N_GROUPS + N_EXPERTS].set(
        router_e_w[l])
    rw_hi = rw.astype(BF16)
    rw = jnp.stack([rw_hi, (rw - rw_hi.astype(F32)).astype(BF16)])
    rb = jnp.zeros((1, LANES), F32).at[0, :N_GROUPS].set(router_g_b[l]).at[0, N_GROUPS:N_GROUPS + N_EXPERTS].set(
        router_e_b[l])
    x1, h2, ids, wts = _out_proj(hg_ctx, da_ctx, x_prompt.reshape(n_ctx, d), hg_lat, da_lat,
                                 x_sample.reshape(n_lat, d), w_out[l].astype(BF16), mods, norm2_g[l], rw, rb, dec_seq)

    src, pos, tile_expert, tile_end, n_tiles = _route_tables(ids, n_tok)
    y_sorted = _gmm(src, tile_expert, tile_end, n_tiles, h2, exp_w1[l], exp_w3[l], exp_w2[l], n_tok)

    y_ctx = _combine(pos, x1, y_sorted, wts, mods, 0, norm_final_g, 0, 1, n_ctx, n_tok)
    y_lat = _combine(pos, x1, y_sorted, wts, mods, 1, norm_final_g, n_ctx, dec_batch, dec_seq, n_tok)
    return (y_ctx.reshape(batch, seq, d), y_lat.reshape(dec_batch, dec_seq, d), new_k, new_v, new_sf, new_sb)
```

```python
import functools
import math

import numpy as np
import jax
import jax.numpy as jnp
from jax import lax
from jax.experimental import pallas as pl
from jax.experimental.pallas import tpu as pltpu

F32 = jnp.float32
BF16 = jnp.bfloat16
I32 = jnp.int32

GRID_W = 64
HG_WIDTH = 1024
HG_DK = 128
HG_HEADS = 8
DA_HEADS = 8
DA_DK = 64
HEAD_W = 128
ROPE_THETA = 10000.0
N_GROUPS = 4
EXP_PER_GROUP = 8
N_EXPERTS = 32
TOP_K = 2
D_EXPERT = 512
N_ADA = 6
RMS_EPS = 1e-6
CB_Q_HG, CB_F_FW, CB_F_BW, CB_I_HG, CB_G_HG, CB_Q_DA, CB_K_DA, CB_V_DA = (8 * i for i in range(8))

LANES = 128
SUBLANES = 8
VMEM_LIMIT = 56 * 1024 * 1024

ADA_TN = 1536
IN_TN = 512
OUT_TM = 256
HG_CHUNK = 64
HG_GROUP = 4
ATT_TQ = 256
ATT_TK = 512
MOE_TM = 256
CMB_TM = 512
NEG_INF = float("-inf")


def _cparams(sem):
    return pltpu.CompilerParams(dimension_semantics=sem, vmem_limit_bytes=VMEM_LIMIT)


def _silu(x):
    return x * jax.nn.sigmoid(x)


def _rms(x):
    return x * lax.rsqrt(jnp.mean(x * x, axis=-1, keepdims=True) + RMS_EPS)


def _ada_kernel(cond_ref, w_ref, b_ref, o_ref):
    s = _silu(cond_ref[...]).astype(BF16)
    o_ref[...] = jnp.dot(s, w_ref[...].astype(BF16), preferred_element_type=F32) + b_ref[...]


def _ada(cond, w, b):
    rows, d = cond.shape
    n = w.shape[1]
    return pl.pallas_call(
        _ada_kernel,
        out_shape=jax.ShapeDtypeStruct((rows, n), F32),
        grid=(n // ADA_TN,),
        in_specs=[pl.BlockSpec((rows, d), lambda j: (0, 0)),
                  pl.BlockSpec((d, ADA_TN), lambda j: (0, j)),
                  pl.BlockSpec((1, ADA_TN), lambda j: (0, j))],
        out_specs=pl.BlockSpec((rows, ADA_TN), lambda j: (0, j)),
        compiler_params=_cparams(("arbitrary",)),
        name="ada",
    )(cond, w, b.reshape(1, n))


def _in_kernel(x_ref, mod_ref, g_ref, w_ref, o_ref, h_ref):
    @pl.when(pl.program_id(1) == 0)
    def _():
        y = _rms(x_ref[...]) * g_ref[...]
        h_ref[...] = (y * (1.0 + mod_ref[0, 1:2, :]) + mod_ref[0, 0:1, :]).astype(BF16)

    o_ref[...] = jnp.dot(h_ref[...], w_ref[...], preferred_element_type=F32)


def _in_proj(x, mods, mod_row0, g, w_bf16, tm, name):
    batch, t, d = x.shape
    cols = w_bf16.shape[1]
    nt = t // tm
    return pl.pallas_call(
        _in_kernel,
        out_shape=jax.ShapeDtypeStruct((batch * t, cols), F32),
        grid=(batch * nt, cols // IN_TN),
        in_specs=[pl.BlockSpec((tm, d), lambda i, j: (i, 0)),
                  pl.BlockSpec((1, N_ADA, d), lambda i, j: (mod_row0 + i // nt, 0, 0)),
                  pl.BlockSpec((1, d), lambda i, j: (0, 0)),
                  pl.BlockSpec((d, IN_TN), lambda i, j: (0, j))],
        out_specs=pl.BlockSpec((tm, IN_TN), lambda i, j: (i, j)),
        scratch_shapes=[pltpu.VMEM((tm, d), BF16)],
        compiler_params=_cparams(("arbitrary", "arbitrary")),
        name=name,
    )(x.reshape(batch * t, d), mods, g.reshape(1, d), w_bf16)


def _rope(x, cos, sin_signed):
    lane = lax.broadcasted_iota(I32, x.shape, 1)
    first = (lane % 32) < 16
    partner = jnp.where(first, pltpu.roll(x, LANES - 16, 1), pltpu.roll(x, 16, 1))
    return x * cos + partner * sin_signed


def _split_maps(k):
    lane = lax.broadcasted_iota(I32, k.shape, 1)
    m1 = lane < DA_DK
    return jnp.where(m1, k, 0.0).astype(BF16), jnp.where(m1, 0.0, k).astype(BF16)


def _lower_bound(lb_ref):
    p = lb_ref[...]
    e = jnp.exp(p - jnp.max(p, axis=0, keepdims=True))
    return e[0:1, :] / jnp.sum(e, axis=0, keepdims=True)


def _hgrn_pair_masks(mask_ref, c):
    ti = lax.broadcasted_iota(I32, (c, c), 0)
    si = lax.broadcasted_iota(I32, (c, c), 1)
    for d in range(2):
        mask_ref[d, 0] = (ti == si).astype(F32)
        for j in range(c.bit_length() - 1):
            same = (ti >> (j + 1)) == (si >> (j + 1))
            t_bit = ((ti >> j) & 1) == 1
            s_bit = ((si >> j) & 1) == 1
            pair = (s_bit & jnp.logical_not(t_bit)) if d else (t_bit & jnp.logical_not(s_bit))
            mask_ref[d, 1 + j] = (same & pair).astype(F32)


def _hgrn_chunk(q, f_logit, v, lb, mask_ref, rev):
    c = q.shape[0]
    n_piece = c // SUBLANES
    levels = c.bit_length() - 1
    low = SUBLANES.bit_length() - 1
    d = 1 if rev else 0
    nt = (((1,), (1,)), ((), ()))
    pieces = lambda x: [x[SUBLANES * i:SUBLANES * (i + 1)] for i in range(n_piece)]
    whole = lambda xs: jnp.concatenate(xs, axis=0)

    fg = lb + (1.0 - lb) * jax.nn.sigmoid(f_logit)
    k = 1.0 - fg
    q_bf = q.astype(BF16)
    k_bf = k.astype(BF16)
    qs, ks, fgs = pieces(q), pieces(k), pieces(fg)
    tot = pieces(jnp.log(fg))
    pre = list(tot)
    suf = [jnp.zeros((SUBLANES, LANES), F32)] * n_piece
    sub = lax.broadcasted_iota(I32, (SUBLANES, LANES), 0)

    def pair_scores(qt, kt, idx):
        return lax.dot_general(qt, kt, nt, preferred_element_type=F32) * mask_ref[d, idx]

    scores = pair_scores(q_bf, k_bf, 0)
    yield
    for j in range(levels):
        if j < low:
            sh = 1 << j
            bit = ((sub >> j) & 1) == 1
            late = jnp.logical_not(bit) if rev else bit
            if j == 0:
                qt = whole([qs[i] * jnp.where(late, fgs[i], 1.0) for i in range(n_piece)]).astype(BF16)
                kt = k_bf
            else:
                es = [jnp.exp(jnp.where(late, pre[i], suf[i])) for i in range(n_piece)]
                qt = whole([qs[i] * es[i] for i in range(n_piece)]).astype(BF16)
                kt = whole([ks[i] * es[i] for i in range(n_piece)]).astype(BF16)
            for i in range(n_piece):
                up = pltpu.roll(tot[i], sh, 0)
                dn = pltpu.roll(tot[i], SUBLANES - sh, 0)
                sib = jnp.where(late, dn, up) if rev else jnp.where(late, up, dn)
                pre[i] = pre[i] + jnp.where(late, sib, 0.0)
                suf[i] = suf[i] + jnp.where(late, 0.0, sib)
                tot[i] = tot[i] + sib
        else:
            half = 1 << (j - low)
            upper = [(i // half) % 2 == 1 for i in range(n_piece)]
            late = [(not u) if rev else u for u in upper]
            es = [jnp.exp(pre[i] if late[i] else suf[i]) for i in range(n_piece)]
            qt = whole([qs[i] * es[i] for i in range(n_piece)]).astype(BF16)
            kt = whole([ks[i] * es[i] for i in range(n_piece)]).astype(BF16)
            sib = [tot[i - half] if upper[i] else tot[i + half] for i in range(n_piece)]
            pre = [pre[i] + sib[i] if late[i] else pre[i] for i in range(n_piece)]
            suf = [suf[i] if late[i] else suf[i] + sib[i] for i in range(n_piece)]
            tot = [tot[i] + sib[i] for i in range(n_piece)]
        scores = scores + pair_scores(qt, kt, 1 + j)
        yield

    q_dec = whole([qs[i] * jnp.exp(pre[i]) for i in range(n_piece)]).astype(BF16)
    k_dec = whole([ks[i] * jnp.exp(suf[i]) for i in range(n_piece)]).astype(BF16)
    return dict(q_dec=q_dec, k_dec=k_dec, v_t=v.T.astype(BF16), v=v.astype(BF16),
                scores=scores.astype(BF16), decay=jnp.exp(tot[0][0:1, :]))


def _hgrn_state_step(p, st_ref):
    st = st_ref[...]
    o = lax.dot_general(p["q_dec"], st.astype(BF16), (((1,), (1,)), ((), ())), preferred_element_type=F32)
    st_ref[...] = p["decay"] * st + jnp.dot(p["v_t"], p["k_dec"], preferred_element_type=F32)
    return o


def _lockstep(gens):
    results = [None] * len(gens)
    live = list(range(len(gens)))
    while live:
        for idx in list(live):
            try:
                next(gens[idx])
            except StopIteration as stop:
                results[idx] = stop.value
                live.remove(idx)
    return results


def _hgrn_kernel(*refs, t, has_init, emit_state):
    q_ref, ff_ref, fb_ref, i_ref, g_ref, lbf_ref, lbb_ref, ng_ref = refs[:8]
    pos = 8
    if has_init:
        s0f_ref, s0b_ref = refs[pos:pos + 2]
        pos += 2
    o_ref = refs[pos]
    pos += 1
    if emit_state:
        sf_ref, sb_ref = refs[pos:pos + 2]
        pos += 2
    of_scr, ob_scr, stf, stb, mask_scr = refs[pos:pos + 5]

    c = HG_CHUNK
    n = t // c

    @pl.when((pl.program_id(0) == 0) & (pl.program_id(1) == 0))
    def _():
        _hgrn_pair_masks(mask_scr, c)

    if has_init:
        stf[...] = s0f_ref[0, 0, 0].T
        stb[...] = s0b_ref[0, 0, 0].T
    else:
        stf[...] = jnp.zeros_like(stf)
        stb[...] = jnp.zeros_like(stb)
    lbf = _lower_bound(lbf_ref)
    lbb = _lower_bound(lbb_ref)

    def body(i, carry):
        work = []
        for u in range(HG_GROUP):
            ci = i * HG_GROUP + u
            work.append((pl.ds(pl.multiple_of(ci * c, c), c), ff_ref, lbf, stf, of_scr, False))
            work.append((pl.ds(pl.multiple_of((n - 1 - ci) * c, c), c), fb_ref, lbb, stb, ob_scr, True))
        parts = _lockstep([_hgrn_chunk(_silu(q_ref[sl, :]), f_ref[sl, :], i_ref[sl, :], lb, mask_scr, rev)
                           for sl, f_ref, lb, _, _, rev in work])
        outs = [_hgrn_state_step(p, w[3]) for p, w in zip(parts, work)]
        for p, w, o in zip(parts, work, outs):
            w[4][w[0], :] = o + jnp.dot(p["scores"], p["v"], preferred_element_type=F32)
        return carry

    lax.fori_loop(0, n // HG_GROUP, body, 0)

    fin = min(t, 256)

    def finish(i, carry):
        sl = pl.ds(pl.multiple_of(i * fin, fin), fin)
        o = _rms(of_scr[sl, :] + ob_scr[sl, :]) * ng_ref[...]
        o_ref[sl, :] = (o * _silu(g_ref[sl, :])).astype(BF16)
        return carry

    lax.fori_loop(0, t // fin, finish, 0)
    if emit_state:
        sf_ref[0, 0, 0] = stf[...].T
        sb_ref[0, 0, 0] = stb[...].T


def _hgrn(proj, batch, t, lb_fwd, lb_bwd, norm_g, s0f=None, s0b=None, emit_state=False):
    has_init = s0f is not None

    def col(cb):
        return pl.BlockSpec((t, HEAD_W), lambda b, h: (b, cb + h))

    lb_spec = pl.BlockSpec((lb_fwd.shape[0], HEAD_W), lambda b, h: (0, h))
    in_specs = [col(CB_Q_HG), col(CB_F_FW), col(CB_F_BW), col(CB_I_HG), col(CB_G_HG),
                lb_spec, lb_spec, pl.BlockSpec((1, HEAD_W), lambda b, h: (0, 0))]
    args = [proj] * 5 + [lb_fwd, lb_bwd, norm_g.reshape(1, HEAD_W)]
    st_spec = pl.BlockSpec((1, 1, 1, HG_DK, HEAD_W), lambda b, h: (b, 0, h, 0, 0))
    if has_init:
        in_specs += [st_spec, st_spec]
        args += [s0f, s0b]
    out_shape = [jax.ShapeDtypeStruct((batch * t, HG_WIDTH), BF16)]
    out_specs = [pl.BlockSpec((t, HEAD_W), lambda b, h: (b, h))]
    if emit_state:
        st = jax.ShapeDtypeStruct((batch, 1, HG_HEADS, HG_DK, HEAD_W), F32)
        out_shape += [st, st]
        out_specs += [st_spec, st_spec]
    return pl.pallas_call(
        functools.partial(_hgrn_kernel, t=t, has_init=has_init, emit_state=emit_state),
        out_shape=out_shape,
        grid=(batch, HG_HEADS),
        in_specs=in_specs,
        out_specs=out_specs,
        scratch_shapes=[pltpu.VMEM((t, HEAD_W), F32), pltpu.VMEM((t, HEAD_W), F32),
                        pltpu.VMEM((HEAD_W, HG_DK), F32), pltpu.VMEM((HEAD_W, HG_DK), F32),
                        pltpu.VMEM((2, HG_CHUNK.bit_length(), HG_CHUNK, HG_CHUNK), F32)],
        compiler_params=_cparams(("arbitrary", "arbitrary")),
        name="hgrn_lat" if has_init else "hgrn_ctx",
    )(*args)


def _attn_kernel(*refs, t, n_cache, use_rope, emit_cache, lam_init):
    it = iter(refs)
    q_ref, k_ref, v_ref = next(it), next(it), next(it)
    if use_rope:
        cosq_ref, sinq_ref, cos_ref, sin_ref = next(it), next(it), next(it), next(it)
    if n_cache:
        ck_ref, cv_ref = next(it), next(it)
    lq1_ref, lk1_ref, lq2_ref, lk2_ref, ng_ref, o_ref = (next(it) for _ in range(6))
    if emit_cache:
        nk_ref, nv_ref = next(it), next(it)
    k1_scr, k2_scr, v_scr, s_scr, p_scr = (next(it) for _ in range(5))
    k_scrs = (k1_scr, k2_scr)
    tq = q_ref.shape[0]

    @pl.when(pl.program_id(2) == 0)
    def _():
        rows = min(t, ATT_TK)

        def stage(i, carry):
            sl = pl.ds(pl.multiple_of(i * rows, rows), rows)
            k = k_ref[sl, :]
            v = v_ref[sl, :]
            if emit_cache:
                nk_ref[0, 0, 0, sl, :] = k
                nv_ref[0, 0, 0, sl, :] = v
            if use_rope:
                k = _rope(k, cos_ref[sl, :], sin_ref[sl, :])
            k1_scr[sl, :], k2_scr[sl, :] = _split_maps(k)
            v_scr[sl, :] = v.astype(BF16)
            return carry

        lax.fori_loop(0, t // rows, stage, 0)
        if n_cache:
            k1_scr[t:t + n_cache, :], k2_scr[t:t + n_cache, :] = _split_maps(ck_ref[0, 0, 0])
            v_scr[t:t + n_cache, :] = cv_ref[0, 0, 0].astype(BF16)

    q = q_ref[...]
    if use_rope:
        q = _rope(q, cosq_ref[...], sinq_ref[...])
    q = (q * (DA_DK ** -0.5)).astype(BF16)
    nt = (((1,), (1,)), ((), ()))
    n_keys = t + n_cache
    tiles = [(st, min(ATT_TK, n_keys - st)) for st in range(0, n_keys, ATT_TK)]

    def scores_tile(mp, st, sz, mx):
        s = lax.dot_general(q, k_scrs[mp][st:st + sz, :], nt, preferred_element_type=F32)
        s_scr[mp, :, st:st + sz] = s
        for j in range(sz // LANES):
            mx = jnp.maximum(mx, s[:, j * LANES:(j + 1) * LANES])
        return mx

    def exp_tile(mp, st, sz, m, part):
        for lo in range(st, st + sz, LANES):
            e = jnp.exp(s_scr[mp, :, lo:lo + LANES] - m)
            part = part + e
            p_scr[mp, :, lo:lo + LANES] = e.astype(BF16)
        return part

    def value_tile(mp, st, sz, acc):
        return acc + jnp.dot(p_scr[mp, :, st:st + sz], v_scr[st:st + sz, :], preferred_element_type=F32)

    def row_stat(x, op):
        return jnp.broadcast_to(op(x, axis=-1, keepdims=True), (tq, LANES))

    neg = jnp.full((tq, LANES), NEG_INF, F32)
    zero = jnp.zeros((tq, LANES), F32)
    mx = neg
    for st, sz in tiles:
        mx = scores_tile(0, st, sz, mx)
    m0 = row_stat(mx, jnp.max)
    mx, part0 = neg, zero
    for st, sz in tiles:
        mx = scores_tile(1, st, sz, mx)
        part0 = exp_tile(0, st, sz, m0, part0)
    m1 = row_stat(mx, jnp.max)
    acc0, part1 = zero, zero
    for st, sz in tiles:
        acc0 = value_tile(0, st, sz, acc0)
        part1 = exp_tile(1, st, sz, m1, part1)
    acc1 = jnp.dot(p_scr[1], v_scr[...], preferred_element_type=F32)

    lam = (jnp.exp(jnp.sum(lq1_ref[...] * lk1_ref[...], axis=-1, keepdims=True))
           - jnp.exp(jnp.sum(lq2_ref[...] * lk2_ref[...], axis=-1, keepdims=True)) + lam_init)
    o = acc0 / row_stat(part0, jnp.sum) - lam * (acc1 / row_stat(part1, jnp.sum))
    o_ref[...] = (_rms(o) * ng_ref[...] * (1.0 - lam_init)).astype(BF16)


def _attn(proj, batch, t, lam_params, norm_g, lam_init, cos=None, sin=None, cache_k=None, cache_v=None,
          emit_cache=False):
    use_rope = cos is not None
    n_cache = 0 if cache_k is None else cache_k.shape[3]
    nq = t // ATT_TQ
    n_keys = t + n_cache

    def col(cb, rows, row_map):
        return pl.BlockSpec((rows, HEAD_W), lambda b, h, i: (row_map(b, i), cb + h))

    in_specs = [col(CB_Q_DA, ATT_TQ, lambda b, i: b * nq + i),
                col(CB_K_DA, t, lambda b, i: b), col(CB_V_DA, t, lambda b, i: b)]
    args = [proj, proj, proj]
    if use_rope:
        in_specs += [pl.BlockSpec((ATT_TQ, HEAD_W), lambda b, h, i: (i, 0))] * 2
        in_specs += [pl.BlockSpec((t, HEAD_W), lambda b, h, i: (0, 0))] * 2
        args += [cos, sin, cos, sin]
    cache_spec = lambda n: pl.BlockSpec((1, 1, 1, n, HEAD_W), lambda b, h, i: (b, 0, h, 0, 0))
    if n_cache:
        in_specs += [cache_spec(n_cache)] * 2
        args += [cache_k, cache_v]
    small = pl.BlockSpec((1, DA_DK), lambda b, h, i: (0, 0))
    in_specs += [small] * 4 + [pl.BlockSpec((1, HEAD_W), lambda b, h, i: (0, 0))]
    args += [p.reshape(1, DA_DK) for p in lam_params] + [norm_g.reshape(1, HEAD_W)]
    out_shape = [jax.ShapeDtypeStruct((batch * t, DA_HEADS * HEAD_W), BF16)]
    out_specs = [pl.BlockSpec((ATT_TQ, HEAD_W), lambda b, h, i: (b * nq + i, h))]
    if emit_cache:
        out_shape += [jax.ShapeDtypeStruct((batch, 1, DA_HEADS, t, HEAD_W), F32)] * 2
        out_specs += [cache_spec(t)] * 2
    scratch = [pltpu.VMEM((n_keys, HEAD_W), BF16)] * 3
    scratch += [pltpu.VMEM((2, ATT_TQ, n_keys), F32), pltpu.VMEM((2, ATT_TQ, n_keys), BF16)]
    return pl.pallas_call(
        functools.partial(_attn_kernel, t=t, n_cache=n_cache, use_rope=use_rope, emit_cache=emit_cache,
                          lam_init=lam_init),
        out_shape=out_shape,
        grid=(batch, DA_HEADS, nq),
        in_specs=in_specs,
        out_specs=out_specs,
        scratch_shapes=scratch,
        compiler_params=_cparams(("arbitrary",) * 3),
        name="attn_lat" if use_rope else "attn_ctx",
    )(*args)


def _out_kernel(mhg_c, mda_c, x_c, mhg_l, mda_l, x_l, w_ref, mod_ref, g_ref, rw_ref, rb_ref,
                x1_ref, h2_ref, ids_ref, wts_ref, *, ctx_tiles):
    def body(mhg_ref, mda_ref, x_ref):
        mix = jnp.dot(mhg_ref[...], w_ref[0:HG_WIDTH, :], preferred_element_type=F32)
        mix = mix + jnp.dot(mda_ref[...], w_ref[HG_WIDTH:, :], preferred_element_type=F32)
        x1 = x_ref[...] + mod_ref[0, 2:3, :] * mix
        x1_ref[...] = x1
        h2 = (_rms(x1) * g_ref[...]) * (1.0 + mod_ref[0, 4:5, :]) + mod_ref[0, 3:4, :]
        h2_ref[...] = h2

        h_hi = h2.astype(BF16)
        h_lo = (h2 - h_hi.astype(F32)).astype(BF16)
        logit = (jnp.dot(h_hi, rw_ref[...], preferred_element_type=F32)
                 + jnp.dot(h_lo, rw_ref[...], preferred_element_type=F32)) + rb_ref[...]
        lane = lax.broadcasted_iota(I32, logit.shape, 1)

        def first_max(x):
            m = jnp.max(x, axis=-1, keepdims=True)
            return m, jnp.min(jnp.where(x == m, lane, LANES), axis=-1, keepdims=True)

        gmask = lane < N_GROUPS
        gmax, gsel = first_max(jnp.where(gmask, logit, NEG_INF))
        p_grp = 1.0 / jnp.sum(jnp.where(gmask, jnp.exp(logit - gmax), 0.0), axis=-1, keepdims=True)
        lo = N_GROUPS + EXP_PER_GROUP * gsel
        le = jnp.where((lane >= lo) & (lane < lo + EXP_PER_GROUP), logit, NEG_INF)
        v1, i1 = first_max(le)
        v2, i2 = first_max(jnp.where(lane == i1, NEG_INF, le))
        e = jnp.exp(v2 - v1)
        w1 = p_grp / (1.0 + e)
        w2 = p_grp * e / (1.0 + e)
        ids_ref[...] = jnp.where(lane == 0, i1 - N_GROUPS, jnp.where(lane == 1, i2 - N_GROUPS, 0))
        wts_ref[...] = jnp.where(lane == 0, w1, jnp.where(lane == 1, w2, 0.0))

    i = pl.program_id(0)
    pl.when(i < ctx_tiles)(lambda: body(mhg_c, mda_c, x_c))
    pl.when(i >= ctx_tiles)(lambda: body(mhg_l, mda_l, x_l))


def _out_proj(mhg_c, mda_c, x_c, mhg_l, mda_l, x_l, w_bf16, mods, g, rw, rb, lat_t):
    n_ctx, d = x_c.shape
    n_lat = x_l.shape[0]
    n = n_ctx + n_lat
    ctx_tiles = n_ctx // OUT_TM
    lat_tiles = lat_t // OUT_TM
    row = lambda i: (i, 0)
    const = lambda i: (0, 0)
    crow = lambda i: (jnp.minimum(i, ctx_tiles - 1), 0)
    lrow = lambda i: (jnp.maximum(i - ctx_tiles, 0), 0)
    seg = lambda i: (jnp.where(i < ctx_tiles, 0, 1 + (i - ctx_tiles) // lat_tiles), 0, 0)
    half = lambda m: pl.BlockSpec((OUT_TM, HG_WIDTH), m)
    full = lambda m: pl.BlockSpec((OUT_TM, d), m)
    return pl.pallas_call(
        functools.partial(_out_kernel, ctx_tiles=ctx_tiles),
        out_shape=[jax.ShapeDtypeStruct((n, d), F32), jax.ShapeDtypeStruct((n, d), F32),
                   jax.ShapeDtypeStruct((n, LANES), I32), jax.ShapeDtypeStruct((n, LANES), F32)],
        grid=(n // OUT_TM,),
        in_specs=[half(crow), half(crow), full(crow), half(lrow), half(lrow), full(lrow),
                  pl.BlockSpec((d, d), const), pl.BlockSpec((1, N_ADA, d), seg), pl.BlockSpec((1, d), const),
                  pl.BlockSpec((d, LANES), const), pl.BlockSpec((1, LANES), const)],
        out_specs=[full(row), full(row), pl.BlockSpec((OUT_TM, LANES), row), pl.BlockSpec((OUT_TM, LANES), row)],
        compiler_params=_cparams(("arbitrary",)),
        name="out_proj",
    )(mhg_c, mda_c, x_c, mhg_l, mda_l, x_l, w_bf16, mods, g.reshape(1, d), rw, rb)


def _gmm_kernel(src_ref, texp_ref, ntile_ref, h_hbm, w1_ref, w3_ref, w2_ref, y_hbm,
                x0, x1, y0, y1, sems, *, n_tok):
    i = pl.program_id(0)
    n_tiles = ntile_ref[0]
    xs = (x0, x1)
    ys = (y0, y1)

    def gather_copy(tile, r, slot):
        p = jnp.maximum(src_ref[(tile + 1) * MOE_TM + r], 0)
        tok = jnp.where(p >= n_tok, p - n_tok, p)
        return pltpu.make_async_copy(h_hbm.at[pl.ds(tok, 1)], xs[slot].at[pl.ds(r, 1)], sems.at[slot])

    def scatter_copy(tile, r, slot):
        p = src_ref[(tile + 1) * MOE_TM + r]
        dst = jnp.where(p >= 0, p, TOP_K * n_tok + r)
        return pltpu.make_async_copy(ys[slot].at[pl.ds(r, 1)], y_hbm.at[pl.ds(dst, 1)], sems.at[2 + slot])

    @pl.when(i == 0)
    def _():
        y1[...] = jnp.zeros_like(y1)
        for r in range(MOE_TM):
            gather_copy(0, r, 0).start()

    def wait_gather(slot):
        pltpu.make_async_copy(h_hbm.at[pl.ds(0, MOE_TM)], xs[slot], sems.at[slot]).wait()

    def wait_scatter(slot):
        pltpu.make_async_copy(ys[slot], y_hbm.at[pl.ds(0, MOE_TM)], sems.at[2 + slot]).wait()

    def step(cur):
        nxt = 1 - cur
        wait_gather(cur)
        for r in range(MOE_TM):
            gather_copy(i + 1, r, nxt).start()
        for r in range(MOE_TM):
            scatter_copy(i - 1, r, nxt).start()
        x = xs[cur][...].astype(BF16)
        a = jnp.dot(x, w1_ref[0].astype(BF16), preferred_element_type=F32)
        b = jnp.dot(x, w3_ref[0].astype(BF16), preferred_element_type=F32)
        hid = (_silu(a) * b).astype(BF16)
        ys[cur][...] = jnp.dot(hid, w2_ref[0].astype(BF16), preferred_element_type=F32)
        wait_scatter(nxt)

        @pl.when(i == n_tiles)
        def _():
            wait_gather(nxt)

    live = i <= n_tiles
    pl.when(live & (i % 2 == 0))(lambda: step(0))
    pl.when(live & (i % 2 == 1))(lambda: step(1))


def _gmm(src, tile_expert, n_tiles, h2, w1, w3, w2, n_tok):
    d = h2.shape[1]
    steps = tile_expert.shape[0]
    wspec = lambda shape: pl.BlockSpec((1,) + shape, lambda i, src, te, nt: (te[i], 0, 0))
    return pl.pallas_call(
        functools.partial(_gmm_kernel, n_tok=n_tok),
        out_shape=jax.ShapeDtypeStruct((TOP_K * n_tok + MOE_TM, d), F32),
        grid_spec=pltpu.PrefetchScalarGridSpec(
            num_scalar_prefetch=3,
            grid=(steps,),
            in_specs=[pl.BlockSpec(memory_space=pl.ANY),
                      wspec((d, D_EXPERT)), wspec((d, D_EXPERT)), wspec((D_EXPERT, d))],
            out_specs=pl.BlockSpec(memory_space=pl.ANY),
            scratch_shapes=[pltpu.VMEM((MOE_TM, d), F32)] * 4 + [pltpu.SemaphoreType.DMA((4,))],
        ),
        compiler_params=_cparams(("arbitrary",)),
        name="gmm",
    )(src, tile_expert, n_tiles, h2, w1, w3, w2)


def _route_tables(ids, n_tok):
    n_pairs = n_tok * TOP_K
    max_tiles = n_pairs // MOE_TM + N_EXPERTS
    e_flat = ids[:, :TOP_K].T.reshape(n_pairs)
    onehot = (e_flat[:, None] == jnp.arange(N_EXPERTS, dtype=I32)[None, :]).astype(I32)
    csum = jnp.cumsum(onehot, axis=0)
    counts = csum[-1]
    rank = jnp.sum(csum * onehot, axis=1) - 1
    tiles_per = (counts + MOE_TM - 1) // MOE_TM
    tile_end = jnp.cumsum(tiles_per)
    row_start = (tile_end - tiles_per + 1) * MOE_TM
    pos = row_start[e_flat] + rank
    src = jnp.full(((max_tiles + 3) * MOE_TM,), -1, I32).at[pos].set(jnp.arange(n_pairs, dtype=I32))
    steps = jnp.arange(max_tiles + 1, dtype=I32)
    owner = jnp.sum((tile_end[None, :] <= steps[:, None]).astype(I32), axis=1)
    tile_expert = jnp.minimum(owner, N_EXPERTS - 1)
    return src, tile_expert, tile_end[-1:].astype(I32)


def _combine_kernel(x1_ref, y0_ref, y1_ref, wts_ref, mod_ref, g_ref, o_ref):
    w = wts_ref[...]
    moe = w[:, 0:1] * y0_ref[...] + w[:, 1:2] * y1_ref[...]
    x2 = x1_ref[...] + mod_ref[0, 5:6, :] * moe
    o_ref[...] = _rms(x2) * g_ref[...]


def _combine(x1, y_pairs, wts, mods, mod_row0, g, row0, batch, t, n_tok):
    d = x1.shape[1]
    nt = t // CMB_TM
    rb0 = row0 // CMB_TM
    slot1 = n_tok // CMB_TM
    return pl.pallas_call(
        _combine_kernel,
        out_shape=jax.ShapeDtypeStruct((batch * t, d), F32),
        grid=(batch * nt,),
        in_specs=[pl.BlockSpec((CMB_TM, d), lambda i: (rb0 + i, 0)),
                  pl.BlockSpec((CMB_TM, d), lambda i: (rb0 + i, 0)),
                  pl.BlockSpec((CMB_TM, d), lambda i: (slot1 + rb0 + i, 0)),
                  pl.BlockSpec((CMB_TM, LANES), lambda i: (rb0 + i, 0)),
                  pl.BlockSpec((1, N_ADA, d), lambda i: (mod_row0 + i // nt, 0, 0)),
                  pl.BlockSpec((1, d), lambda i: (0, 0))],
        out_specs=pl.BlockSpec((CMB_TM, d), lambda i: (i, 0)),
        compiler_params=_cparams(("arbitrary",)),
        name="combine",
    )(x1, y_pairs, y_pairs, wts, mods, g.reshape(1, d))


def _rope_tables(n_lat):
    half = DA_DK // 2
    nf = half // 2
    t = jnp.arange(n_lat)
    row = (t // GRID_W).astype(F32)
    col = (t % GRID_W).astype(F32)
    inv = ROPE_THETA ** (-jnp.arange(nf, dtype=F32) / nf)
    lane = np.arange(HEAD_W)
    freq = inv[lane % nf]
    pos = jnp.where(((lane // half) % 2 == 0)[None, :], row[:, None], col[:, None])
    ang = pos * freq[None, :]
    sign = np.where((lane % half) < nf, -1.0, 1.0).astype(np.float32)
    return jnp.cos(ang), jnp.sin(ang) * sign[None, :]


def kernel(x_prompt, x_sample, cache_k, cache_v, state_hgrn_fwd, state_hgrn_bwd, c, c_ctx, w_ada, b_ada,
           norm1_g, norm2_g, norm_final_g, w_in, hg_lb_fwd, hg_lb_bwd, hg_norm_g, da_lambda_q1, da_lambda_k1,
           da_lambda_q2, da_lambda_k2, da_norm_g, w_out, router_g_w, router_g_b, router_e_w, router_e_b,
           exp_w1, exp_w3, exp_w2):
    l = 0
    batch, seq, d = x_prompt.shape
    dec_batch, dec_seq, _ = x_sample.shape
    n_ctx = batch * seq
    n_lat = dec_batch * dec_seq
    n_tok = n_ctx + n_lat
    lam_init = 0.8 - 0.6 * math.exp(-0.3 * l)

    cond = jnp.zeros((8, d), F32).at[0].set(c_ctx).at[1:1 + dec_batch].set(c)
    mods = _ada(cond, w_ada[l], b_ada[l]).reshape(8, N_ADA, d)

    w_in_bf16 = w_in[l].astype(BF16)
    proj_c = _in_proj(x_prompt.reshape(1, n_ctx, d), mods, 0, norm1_g[l], w_in_bf16, 1024, "in_proj_ctx")
    proj_l = _in_proj(x_sample, mods, 1, norm1_g[l], w_in_bf16, 1024, "in_proj_lat")
    lam_params = (da_lambda_q1[l], da_lambda_k1[l], da_lambda_q2[l], da_lambda_k2[l])

    da_ctx, new_k, new_v = _attn(proj_c, batch, seq, lam_params, da_norm_g[l], lam_init, emit_cache=True)
    hg_ctx, new_sf, new_sb = _hgrn(proj_c, batch, seq, hg_lb_fwd, hg_lb_bwd, hg_norm_g[l], emit_state=True)

    cos, sin = _rope_tables(dec_seq)
    (da_lat,) = _attn(proj_l, dec_batch, dec_seq, lam_params, da_norm_g[l], lam_init, cos=cos, sin=sin,
                      cache_k=cache_k[:, l:l + 1], cache_v=cache_v[:, l:l + 1])
    (hg_lat,) = _hgrn(proj_l, dec_batch, dec_seq, hg_lb_fwd, hg_lb_bwd, hg_norm_g[l],
                      s0f=state_hgrn_fwd[:, l:l + 1], s0b=state_hgrn_bwd[:, l:l + 1])

    rw = jnp.zeros((d, LANES), F32).at[:, :N_GROUPS].set(router_g_w[l]).at[:, N_GROUPS:N_GROUPS + N_EXPERTS].set(
        router_e_w[l]).astype(BF16)
    rb = jnp.zeros((1, LANES), F32).at[0, :N_GROUPS].set(router_g_b[l]).at[0, N_GROUPS:N_GROUPS + N_EXPERTS].set(
        router_e_b[l])
    x1, h2, ids, wts = _out_proj(hg_ctx, da_ctx, x_prompt.reshape(n_ctx, d), hg_lat, da_lat,
                                 x_sample.reshape(n_lat, d), w_out[l].astype(BF16), mods, norm2_g[l], rw, rb, dec_seq)

    src, tile_expert, n_tiles = _route_tables(ids, n_tok)
    y_pairs = _gmm(src, tile_expert, n_tiles, h2, exp_w1[l], exp_w3[l], exp_w2[l], n_tok)

    y_ctx = _combine(x1, y_pairs, wts, mods, 0, norm_final_g, 0, 1, n_ctx, n_tok)
    y_lat = _combine(x1, y_pairs, wts, mods, 1, norm_final_g, n_ctx, dec_batch, dec_seq, n_tok)
    return (y_ctx.reshape(batch, seq, d), y_lat.reshape(dec_batch, dec_seq, d), new_k, new_v, new_sf, new_sb)
```

```python
import functools
import math

import numpy as np
import jax
import jax.numpy as jnp
from jax import lax
from jax.experimental import pallas as pl
from jax.experimental.pallas import tpu as pltpu

F32 = jnp.float32
BF16 = jnp.bfloat16
I32 = jnp.int32

GRID_W = 64
HG_WIDTH = 1024
HG_DK = 128
HG_HEADS = 8
DA_HEADS = 8
DA_DK = 64
HEAD_W = 128
ROPE_THETA = 10000.0
N_GROUPS = 4
EXP_PER_GROUP = 8
N_EXPERTS = 32
TOP_K = 2
D_EXPERT = 512
N_ADA = 6
RMS_EPS = 1e-6
CB_Q_HG, CB_F_FW, CB_F_BW, CB_I_HG, CB_G_HG, CB_Q_DA, CB_K_DA, CB_V_DA = (8 * i for i in range(8))

LANES = 128
SUBLANES = 8
VMEM_LIMIT = 56 * 1024 * 1024

ADA_TN = 1536
IN_TN = 512
OUT_TM = 256
HG_CHUNK = 64
HG_GROUP = 4
ATT_TQ = 256
ATT_TK = 512
MOE_TM = 256
GATHER_PARTS = 4
CMB_TM = 512
NEG_INF = float("-inf")


def _cparams(sem):
    return pltpu.CompilerParams(dimension_semantics=sem, vmem_limit_bytes=VMEM_LIMIT)


def _silu(x):
    return x * jax.nn.sigmoid(x)


def _rms(x):
    return x * lax.rsqrt(jnp.mean(x * x, axis=-1, keepdims=True) + RMS_EPS)


def _ada_kernel(cond_ref, w_ref, b_ref, o_ref):
    s = _silu(cond_ref[...]).astype(BF16)
    o_ref[...] = jnp.dot(s, w_ref[...].astype(BF16), preferred_element_type=F32) + b_ref[...]


def _ada(cond, w, b):
    rows, d = cond.shape
    n = w.shape[1]
    return pl.pallas_call(
        _ada_kernel,
        out_shape=jax.ShapeDtypeStruct((rows, n), F32),
        grid=(n // ADA_TN,),
        in_specs=[pl.BlockSpec((rows, d), lambda j: (0, 0)),
                  pl.BlockSpec((d, ADA_TN), lambda j: (0, j)),
                  pl.BlockSpec((1, ADA_TN), lambda j: (0, j))],
        out_specs=pl.BlockSpec((rows, ADA_TN), lambda j: (0, j)),
        compiler_params=_cparams(("arbitrary",)),
        name="ada",
    )(cond, w, b.reshape(1, n))


def _in_kernel(x_ref, mod_ref, g_ref, w_ref, o_ref, h_ref):
    @pl.when(pl.program_id(1) == 0)
    def _():
        y = _rms(x_ref[...]) * g_ref[...]
        h_ref[...] = (y * (1.0 + mod_ref[0, 1:2, :]) + mod_ref[0, 0:1, :]).astype(BF16)

    o_ref[...] = jnp.dot(h_ref[...], w_ref[...], preferred_element_type=F32)


def _in_proj(x, mods, mod_row0, g, w_bf16, tm, name):
    batch, t, d = x.shape
    cols = w_bf16.shape[1]
    nt = t // tm
    return pl.pallas_call(
        _in_kernel,
        out_shape=jax.ShapeDtypeStruct((batch * t, cols), F32),
        grid=(batch * nt, cols // IN_TN),
        in_specs=[pl.BlockSpec((tm, d), lambda i, j: (i, 0)),
                  pl.BlockSpec((1, N_ADA, d), lambda i, j: (mod_row0 + i // nt, 0, 0)),
                  pl.BlockSpec((1, d), lambda i, j: (0, 0)),
                  pl.BlockSpec((d, IN_TN), lambda i, j: (0, j))],
        out_specs=pl.BlockSpec((tm, IN_TN), lambda i, j: (i, j)),
        scratch_shapes=[pltpu.VMEM((tm, d), BF16)],
        compiler_params=_cparams(("arbitrary", "arbitrary")),
        name=name,
    )(x.reshape(batch * t, d), mods, g.reshape(1, d), w_bf16)


def _rope(x, cos, sin_signed):
    lane = lax.broadcasted_iota(I32, x.shape, 1)
    first = (lane % 32) < 16
    partner = jnp.where(first, pltpu.roll(x, LANES - 16, 1), pltpu.roll(x, 16, 1))
    return x * cos + partner * sin_signed


def _split_maps(k):
    lane = lax.broadcasted_iota(I32, k.shape, 1)
    m1 = lane < DA_DK
    return jnp.where(m1, k, 0.0).astype(BF16), jnp.where(m1, 0.0, k).astype(BF16)


def _lower_bound(lb_ref):
    p = lb_ref[...]
    e = jnp.exp(p - jnp.max(p, axis=0, keepdims=True))
    return e[0:1, :] / jnp.sum(e, axis=0, keepdims=True)


def _hgrn_pair_masks(mask_ref, c):
    ti = lax.broadcasted_iota(I32, (c, c), 0)
    si = lax.broadcasted_iota(I32, (c, c), 1)
    for d in range(2):
        mask_ref[d, 0] = (ti == si).astype(F32)
        for j in range(c.bit_length() - 1):
            same = (ti >> (j + 1)) == (si >> (j + 1))
            t_bit = ((ti >> j) & 1) == 1
            s_bit = ((si >> j) & 1) == 1
            pair = (s_bit & jnp.logical_not(t_bit)) if d else (t_bit & jnp.logical_not(s_bit))
            mask_ref[d, 1 + j] = (same & pair).astype(F32)


def _hgrn_chunk(q, f_logit, v, lb, mask_ref, rev):
    c = q.shape[0]
    n_piece = c // SUBLANES
    levels = c.bit_length() - 1
    low = SUBLANES.bit_length() - 1
    d = 1 if rev else 0
    nt = (((1,), (1,)), ((), ()))
    pieces = lambda x: [x[SUBLANES * i:SUBLANES * (i + 1)] for i in range(n_piece)]
    whole = lambda xs: jnp.concatenate(xs, axis=0)

    fg = lb + (1.0 - lb) * jax.nn.sigmoid(f_logit)
    k = 1.0 - fg
    q_bf = q.astype(BF16)
    k_bf = k.astype(BF16)
    qs, ks, fgs = pieces(q), pieces(k), pieces(fg)
    tot = pieces(jnp.log(fg))
    pre = list(tot)
    suf = [jnp.zeros((SUBLANES, LANES), F32)] * n_piece
    sub = lax.broadcasted_iota(I32, (SUBLANES, LANES), 0)

    def pair_scores(qt, kt, idx):
        return lax.dot_general(qt, kt, nt, preferred_element_type=F32) * mask_ref[d, idx]

    scores = pair_scores(q_bf, k_bf, 0)
    yield
    for j in range(levels):
        if j < low:
            sh = 1 << j
            bit = ((sub >> j) & 1) == 1
            late = jnp.logical_not(bit) if rev else bit
            if j == 0:
                qt = whole([qs[i] * jnp.where(late, fgs[i], 1.0) for i in range(n_piece)]).astype(BF16)
                kt = k_bf
            else:
                es = [jnp.exp(jnp.where(late, pre[i], suf[i])) for i in range(n_piece)]
                qt = whole([qs[i] * es[i] for i in range(n_piece)]).astype(BF16)
                kt = whole([ks[i] * es[i] for i in range(n_piece)]).astype(BF16)
            for i in range(n_piece):
                up = pltpu.roll(tot[i], sh, 0)
                dn = pltpu.roll(tot[i], SUBLANES - sh, 0)
                sib = jnp.where(late, dn, up) if rev else jnp.where(late, up, dn)
                pre[i] = pre[i] + jnp.where(late, sib, 0.0)
                suf[i] = suf[i] + jnp.where(late, 0.0, sib)
                tot[i] = tot[i] + sib
        else:
            half = 1 << (j - low)
            upper = [(i // half) % 2 == 1 for i in range(n_piece)]
            late = [(not u) if rev else u for u in upper]
            es = [jnp.exp(pre[i] if late[i] else suf[i]) for i in range(n_piece)]
            qt = whole([qs[i] * es[i] for i in range(n_piece)]).astype(BF16)
            kt = whole([ks[i] * es[i] for i in range(n_piece)]).astype(BF16)
            sib = [tot[i - half] if upper[i] else tot[i + half] for i in range(n_piece)]
            pre = [pre[i] + sib[i] if late[i] else pre[i] for i in range(n_piece)]
            suf = [suf[i] if late[i] else suf[i] + sib[i] for i in range(n_piece)]
            tot = [tot[i] + sib[i] for i in range(n_piece)]
        scores = scores + pair_scores(qt, kt, 1 + j)
        yield

    q_dec = whole([qs[i] * jnp.exp(pre[i]) for i in range(n_piece)]).astype(BF16)
    k_dec = whole([ks[i] * jnp.exp(suf[i]) for i in range(n_piece)]).astype(BF16)
    return dict(q_dec=q_dec, k_dec=k_dec, v_t=v.T.astype(BF16), v=v.astype(BF16),
                scores=scores.astype(BF16), decay=jnp.exp(tot[0][0:1, :]))


def _hgrn_state_step(p, st_ref):
    st = st_ref[...]
    o = lax.dot_general(p["q_dec"], st.astype(BF16), (((1,), (1,)), ((), ())), preferred_element_type=F32)
    st_ref[...] = p["decay"] * st + jnp.dot(p["v_t"], p["k_dec"], preferred_element_type=F32)
    return o


def _lockstep(gens):
    results = [None] * len(gens)
    live = list(range(len(gens)))
    while live:
        for idx in list(live):
            try:
                next(gens[idx])
            except StopIteration as stop:
                results[idx] = stop.value
                live.remove(idx)
    return results


def _hgrn_kernel(*refs, t, has_init, emit_state):
    q_ref, ff_ref, fb_ref, i_ref, g_ref, lbf_ref, lbb_ref, ng_ref = refs[:8]
    pos = 8
    if has_init:
        s0f_ref, s0b_ref = refs[pos:pos + 2]
        pos += 2
    o_ref = refs[pos]
    pos += 1
    if emit_state:
        sf_ref, sb_ref = refs[pos:pos + 2]
        pos += 2
    of_scr, ob_scr, stf, stb, mask_scr = refs[pos:pos + 5]

    c = HG_CHUNK
    n = t // c

    @pl.when((pl.program_id(0) == 0) & (pl.program_id(1) == 0))
    def _():
        _hgrn_pair_masks(mask_scr, c)

    if has_init:
        stf[...] = s0f_ref[0, 0, 0].T
        stb[...] = s0b_ref[0, 0, 0].T
    else:
        stf[...] = jnp.zeros_like(stf)
        stb[...] = jnp.zeros_like(stb)
    lbf = _lower_bound(lbf_ref)
    lbb = _lower_bound(lbb_ref)

    def body(i, carry):
        work = []
        for u in range(HG_GROUP):
            ci = i * HG_GROUP + u
            work.append((pl.ds(pl.multiple_of(ci * c, c), c), ff_ref, lbf, stf, of_scr, False))
            work.append((pl.ds(pl.multiple_of((n - 1 - ci) * c, c), c), fb_ref, lbb, stb, ob_scr, True))
        parts = _lockstep([_hgrn_chunk(_silu(q_ref[sl, :]), f_ref[sl, :], i_ref[sl, :], lb, mask_scr, rev)
                           for sl, f_ref, lb, _, _, rev in work])
        outs = [_hgrn_state_step(p, w[3]) for p, w in zip(parts, work)]
        for p, w, o in zip(parts, work, outs):
            w[4][w[0], :] = o + jnp.dot(p["scores"], p["v"], preferred_element_type=F32)
        return carry

    lax.fori_loop(0, n // HG_GROUP, body, 0)

    fin = min(t, 256)

    def finish(i, carry):
        sl = pl.ds(pl.multiple_of(i * fin, fin), fin)
        o = _rms(of_scr[sl, :] + ob_scr[sl, :]) * ng_ref[...]
        o_ref[sl, :] = (o * _silu(g_ref[sl, :])).astype(BF16)
        return carry

    lax.fori_loop(0, t // fin, finish, 0)
    if emit_state:
        sf_ref[0, 0, 0] = stf[...].T
        sb_ref[0, 0, 0] = stb[...].T


def _hgrn(proj, batch, t, lb_fwd, lb_bwd, norm_g, s0f=None, s0b=None, emit_state=False):
    has_init = s0f is not None

    def col(cb):
        return pl.BlockSpec((t, HEAD_W), lambda b, h: (b, cb + h))

    lb_spec = pl.BlockSpec((lb_fwd.shape[0], HEAD_W), lambda b, h: (0, h))
    in_specs = [col(CB_Q_HG), col(CB_F_FW), col(CB_F_BW), col(CB_I_HG), col(CB_G_HG),
                lb_spec, lb_spec, pl.BlockSpec((1, HEAD_W), lambda b, h: (0, 0))]
    args = [proj] * 5 + [lb_fwd, lb_bwd, norm_g.reshape(1, HEAD_W)]
    st_spec = pl.BlockSpec((1, 1, 1, HG_DK, HEAD_W), lambda b, h: (b, 0, h, 0, 0))
    if has_init:
        in_specs += [st_spec, st_spec]
        args += [s0f, s0b]
    out_shape = [jax.ShapeDtypeStruct((batch * t, HG_WIDTH), BF16)]
    out_specs = [pl.BlockSpec((t, HEAD_W), lambda b, h: (b, h))]
    if emit_state:
        st = jax.ShapeDtypeStruct((batch, 1, HG_HEADS, HG_DK, HEAD_W), F32)
        out_shape += [st, st]
        out_specs += [st_spec, st_spec]
    return pl.pallas_call(
        functools.partial(_hgrn_kernel, t=t, has_init=has_init, emit_state=emit_state),
        out_shape=out_shape,
        grid=(batch, HG_HEADS),
        in_specs=in_specs,
        out_specs=out_specs,
        scratch_shapes=[pltpu.VMEM((t, HEAD_W), F32), pltpu.VMEM((t, HEAD_W), F32),
                        pltpu.VMEM((HEAD_W, HG_DK), F32), pltpu.VMEM((HEAD_W, HG_DK), F32),
                        pltpu.VMEM((2, HG_CHUNK.bit_length(), HG_CHUNK, HG_CHUNK), F32)],
        compiler_params=_cparams(("arbitrary", "arbitrary")),
        name="hgrn_lat" if has_init else "hgrn_ctx",
    )(*args)


def _attn_kernel(*refs, t, n_cache, use_rope, emit_cache, lam_init):
    it = iter(refs)
    q_ref, k_ref, v_ref = next(it), next(it), next(it)
    if use_rope:
        cosq_ref, sinq_ref, cos_ref, sin_ref = next(it), next(it), next(it), next(it)
    if n_cache:
        ck_ref, cv_ref = next(it), next(it)
    lq1_ref, lk1_ref, lq2_ref, lk2_ref, ng_ref, o_ref = (next(it) for _ in range(6))
    if emit_cache:
        nk_ref, nv_ref = next(it), next(it)
    k1_scr, k2_scr, v_scr, s_scr, p_scr = (next(it) for _ in range(5))
    k_scrs = (k1_scr, k2_scr)
    tq = q_ref.shape[0]

    @pl.when(pl.program_id(2) == 0)
    def _():
        rows = min(t, ATT_TK)

        def stage(i, carry):
            sl = pl.ds(pl.multiple_of(i * rows, rows), rows)
            k = k_ref[sl, :]
            v = v_ref[sl, :]
            if emit_cache:
                nk_ref[0, 0, 0, sl, :] = k
                nv_ref[0, 0, 0, sl, :] = v
            if use_rope:
                k = _rope(k, cos_ref[sl, :], sin_ref[sl, :])
            k1_scr[sl, :], k2_scr[sl, :] = _split_maps(k)
            v_scr[sl, :] = v.astype(BF16)
            return carry

        lax.fori_loop(0, t // rows, stage, 0)
        if n_cache:
            k1_scr[t:t + n_cache, :], k2_scr[t:t + n_cache, :] = _split_maps(ck_ref[0, 0, 0])
            v_scr[t:t + n_cache, :] = cv_ref[0, 0, 0].astype(BF16)

    q = q_ref[...]
    if use_rope:
        q = _rope(q, cosq_ref[...], sinq_ref[...])
    q = (q * (DA_DK ** -0.5)).astype(BF16)
    nt = (((1,), (1,)), ((), ()))
    n_keys = t + n_cache
    tiles = [(st, min(ATT_TK, n_keys - st)) for st in range(0, n_keys, ATT_TK)]

    def scores_tile(mp, st, sz, mx):
        s = lax.dot_general(q, k_scrs[mp][st:st + sz, :], nt, preferred_element_type=F32)
        s_scr[mp, :, st:st + sz] = s
        for j in range(sz // LANES):
            mx = jnp.maximum(mx, s[:, j * LANES:(j + 1) * LANES])
        return mx

    def exp_tile(mp, st, sz, m, part):
        for lo in range(st, st + sz, LANES):
            e = jnp.exp(s_scr[mp, :, lo:lo + LANES] - m)
            part = part + e
            p_scr[mp, :, lo:lo + LANES] = e.astype(BF16)
        return part

    def value_tile(mp, st, sz, acc):
        return acc + jnp.dot(p_scr[mp, :, st:st + sz], v_scr[st:st + sz, :], preferred_element_type=F32)

    def row_stat(x, op):
        return jnp.broadcast_to(op(x, axis=-1, keepdims=True), (tq, LANES))

    neg = jnp.full((tq, LANES), NEG_INF, F32)
    zero = jnp.zeros((tq, LANES), F32)
    mx = neg
    for st, sz in tiles:
        mx = scores_tile(0, st, sz, mx)
    m0 = row_stat(mx, jnp.max)
    mx, part0 = neg, zero
    for st, sz in tiles:
        mx = scores_tile(1, st, sz, mx)
        part0 = exp_tile(0, st, sz, m0, part0)
    m1 = row_stat(mx, jnp.max)
    acc0, part1 = zero, zero
    for st, sz in tiles:
        acc0 = value_tile(0, st, sz, acc0)
        part1 = exp_tile(1, st, sz, m1, part1)
    acc1 = jnp.dot(p_scr[1], v_scr[...], preferred_element_type=F32)

    lam = (jnp.exp(jnp.sum(lq1_ref[...] * lk1_ref[...], axis=-1, keepdims=True))
           - jnp.exp(jnp.sum(lq2_ref[...] * lk2_ref[...], axis=-1, keepdims=True)) + lam_init)
    o = acc0 / row_stat(part0, jnp.sum) - lam * (acc1 / row_stat(part1, jnp.sum))
    o_ref[...] = (_rms(o) * ng_ref[...] * (1.0 - lam_init)).astype(BF16)


def _attn(proj, batch, t, lam_params, norm_g, lam_init, cos=None, sin=None, cache_k=None, cache_v=None,
          emit_cache=False):
    use_rope = cos is not None
    n_cache = 0 if cache_k is None else cache_k.shape[3]
    nq = t // ATT_TQ
    n_keys = t + n_cache

    def col(cb, rows, row_map):
        return pl.BlockSpec((rows, HEAD_W), lambda b, h, i: (row_map(b, i), cb + h))

    in_specs = [col(CB_Q_DA, ATT_TQ, lambda b, i: b * nq + i),
                col(CB_K_DA, t, lambda b, i: b), col(CB_V_DA, t, lambda b, i: b)]
    args = [proj, proj, proj]
    if use_rope:
        in_specs += [pl.BlockSpec((ATT_TQ, HEAD_W), lambda b, h, i: (i, 0))] * 2
        in_specs += [pl.BlockSpec((t, HEAD_W), lambda b, h, i: (0, 0))] * 2
        args += [cos, sin, cos, sin]
    cache_spec = lambda n: pl.BlockSpec((1, 1, 1, n, HEAD_W), lambda b, h, i: (b, 0, h, 0, 0))
    if n_cache:
        in_specs += [cache_spec(n_cache)] * 2
        args += [cache_k, cache_v]
    small = pl.BlockSpec((1, DA_DK), lambda b, h, i: (0, 0))
    in_specs += [small] * 4 + [pl.BlockSpec((1, HEAD_W), lambda b, h, i: (0, 0))]
    args += [p.reshape(1, DA_DK) for p in lam_params] + [norm_g.reshape(1, HEAD_W)]
    out_shape = [jax.ShapeDtypeStruct((batch * t, DA_HEADS * HEAD_W), BF16)]
    out_specs = [pl.BlockSpec((ATT_TQ, HEAD_W), lambda b, h, i: (b * nq + i, h))]
    if emit_cache:
        out_shape += [jax.ShapeDtypeStruct((batch, 1, DA_HEADS, t, HEAD_W), F32)] * 2
        out_specs += [cache_spec(t)] * 2
    scratch = [pltpu.VMEM((n_keys, HEAD_W), BF16)] * 3
    scratch += [pltpu.VMEM((2, ATT_TQ, n_keys), F32), pltpu.VMEM((2, ATT_TQ, n_keys), BF16)]
    return pl.pallas_call(
        functools.partial(_attn_kernel, t=t, n_cache=n_cache, use_rope=use_rope, emit_cache=emit_cache,
                          lam_init=lam_init),
        out_shape=out_shape,
        grid=(batch, DA_HEADS, nq),
        in_specs=in_specs,
        out_specs=out_specs,
        scratch_shapes=scratch,
        compiler_params=_cparams(("arbitrary",) * 3),
        name="attn_lat" if use_rope else "attn_ctx",
    )(*args)


def _out_kernel(mhg_c, mda_c, x_c, mhg_l, mda_l, x_l, w_ref, mod_ref, g_ref, rw_ref, rb_ref,
                x1_ref, h2_ref, ids_ref, wts_ref, *, ctx_tiles):
    def body(mhg_ref, mda_ref, x_ref):
        mix = jnp.dot(mhg_ref[...], w_ref[0:HG_WIDTH, :], preferred_element_type=F32)
        mix = mix + jnp.dot(mda_ref[...], w_ref[HG_WIDTH:, :], preferred_element_type=F32)
        x1 = x_ref[...] + mod_ref[0, 2:3, :] * mix
        x1_ref[...] = x1
        h2 = (_rms(x1) * g_ref[...]) * (1.0 + mod_ref[0, 4:5, :]) + mod_ref[0, 3:4, :]
        h2_ref[...] = h2

        h_hi = h2.astype(BF16)
        h_lo = (h2 - h_hi.astype(F32)).astype(BF16)
        logit = (jnp.dot(h_hi, rw_ref[...], preferred_element_type=F32)
                 + jnp.dot(h_lo, rw_ref[...], preferred_element_type=F32)) + rb_ref[...]
        lane = lax.broadcasted_iota(I32, logit.shape, 1)

        def first_max(x):
            m = jnp.max(x, axis=-1, keepdims=True)
            return m, jnp.min(jnp.where(x == m, lane, LANES), axis=-1, keepdims=True)

        gmask = lane < N_GROUPS
        gmax, gsel = first_max(jnp.where(gmask, logit, NEG_INF))
        p_grp = 1.0 / jnp.sum(jnp.where(gmask, jnp.exp(logit - gmax), 0.0), axis=-1, keepdims=True)
        lo = N_GROUPS + EXP_PER_GROUP * gsel
        le = jnp.where((lane >= lo) & (lane < lo + EXP_PER_GROUP), logit, NEG_INF)
        v1, i1 = first_max(le)
        v2, i2 = first_max(jnp.where(lane == i1, NEG_INF, le))
        e = jnp.exp(v2 - v1)
        w1 = p_grp / (1.0 + e)
        w2 = p_grp * e / (1.0 + e)
        ids_ref[...] = jnp.where(lane == 0, i1 - N_GROUPS, jnp.where(lane == 1, i2 - N_GROUPS, 0))
        wts_ref[...] = jnp.where(lane == 0, w1, jnp.where(lane == 1, w2, 0.0))

    i = pl.program_id(0)
    pl.when(i < ctx_tiles)(lambda: body(mhg_c, mda_c, x_c))
    pl.when(i >= ctx_tiles)(lambda: body(mhg_l, mda_l, x_l))


def _out_proj(mhg_c, mda_c, x_c, mhg_l, mda_l, x_l, w_bf16, mods, g, rw, rb, lat_t):
    n_ctx, d = x_c.shape
    n_lat = x_l.shape[0]
    n = n_ctx + n_lat
    ctx_tiles = n_ctx // OUT_TM
    lat_tiles = lat_t // OUT_TM
    row = lambda i: (i, 0)
    const = lambda i: (0, 0)
    crow = lambda i: (jnp.minimum(i, ctx_tiles - 1), 0)
    lrow = lambda i: (jnp.maximum(i - ctx_tiles, 0), 0)
    seg = lambda i: (jnp.where(i < ctx_tiles, 0, 1 + (i - ctx_tiles) // lat_tiles), 0, 0)
    half = lambda m: pl.BlockSpec((OUT_TM, HG_WIDTH), m)
    full = lambda m: pl.BlockSpec((OUT_TM, d), m)
    return pl.pallas_call(
        functools.partial(_out_kernel, ctx_tiles=ctx_tiles),
        out_shape=[jax.ShapeDtypeStruct((n, d), F32), jax.ShapeDtypeStruct((n, d), F32),
                   jax.ShapeDtypeStruct((n, LANES), I32), jax.ShapeDtypeStruct((n, LANES), F32)],
        grid=(n // OUT_TM,),
        in_specs=[half(crow), half(crow), full(crow), half(lrow), half(lrow), full(lrow),
                  pl.BlockSpec((d, d), const), pl.BlockSpec((1, N_ADA, d), seg), pl.BlockSpec((1, d), const),
                  pl.BlockSpec((d, LANES), const), pl.BlockSpec((1, LANES), const)],
        out_specs=[full(row), full(row), pl.BlockSpec((OUT_TM, LANES), row), pl.BlockSpec((OUT_TM, LANES), row)],
        compiler_params=_cparams(("arbitrary",)),
        name="out_proj",
    )(mhg_c, mda_c, x_c, mhg_l, mda_l, x_l, w_bf16, mods, g.reshape(1, d), rw, rb)


def _gmm_kernel(src_ref, texp_ref, ntile_ref, h_hbm, w1_ref, w3_ref, w2_ref, y_hbm, *scratch, n_tok):
    i = pl.program_id(0)
    n_tiles = ntile_ref[0]
    xs = (scratch[:GATHER_PARTS], scratch[GATHER_PARTS:2 * GATHER_PARTS])
    y0, y1, sems = scratch[2 * GATHER_PARTS:]
    ys = (y0, y1)
    part_rows = MOE_TM // GATHER_PARTS

    def gather_copy(tile, r, slot):
        p = jnp.maximum(src_ref[(tile + 1) * MOE_TM + r], 0)
        tok = jnp.where(p >= n_tok, p - n_tok, p)
        part, row = divmod(r, part_rows)
        return pltpu.make_async_copy(h_hbm.at[pl.ds(tok, 1)], xs[slot][part].at[pl.ds(row, 1)],
                                     sems.at[slot * GATHER_PARTS + part])

    def scatter_copy(tile, r, slot):
        p = src_ref[(tile + 1) * MOE_TM + r]
        dst = jnp.where(p >= 0, p, TOP_K * n_tok + r)
        return pltpu.make_async_copy(ys[slot].at[pl.ds(r, 1)], y_hbm.at[pl.ds(dst, 1)],
                                     sems.at[2 * GATHER_PARTS + slot])

    @pl.when(i == 0)
    def _():
        y1[...] = jnp.zeros_like(y1)
        for r in range(MOE_TM):
            gather_copy(0, r, 0).start()

    def wait_gather(slot):
        for part in range(GATHER_PARTS):
            pltpu.make_async_copy(h_hbm.at[pl.ds(0, part_rows)], xs[slot][part],
                                  sems.at[slot * GATHER_PARTS + part]).wait()

    def wait_scatter(slot):
        pltpu.make_async_copy(ys[slot], y_hbm.at[pl.ds(0, MOE_TM)], sems.at[2 * GATHER_PARTS + slot]).wait()

    def step(cur):
        nxt = 1 - cur
        wait_gather(cur)
        for r in range(MOE_TM):
            gather_copy(i + 1, r, nxt).start()
        for r in range(MOE_TM):
            scatter_copy(i - 1, r, nxt).start()
        x = jnp.concatenate([buf[...] for buf in xs[cur]], axis=0).astype(BF16)
        a = jnp.dot(x, w1_ref[0].astype(BF16), preferred_element_type=F32)
        b = jnp.dot(x, w3_ref[0].astype(BF16), preferred_element_type=F32)
        hid = (_silu(a) * b).astype(BF16)
        ys[cur][...] = jnp.dot(hid, w2_ref[0].astype(BF16), preferred_element_type=F32)
        wait_scatter(nxt)

        @pl.when(i == n_tiles)
        def _():
            wait_gather(nxt)

    live = i <= n_tiles
    pl.when(live & (i % 2 == 0))(lambda: step(0))
    pl.when(live & (i % 2 == 1))(lambda: step(1))


def _gmm(src, tile_expert, n_tiles, h2, w1, w3, w2, n_tok):
    d = h2.shape[1]
    steps = tile_expert.shape[0]
    wspec = lambda shape: pl.BlockSpec((1,) + shape, lambda i, src, te, nt: (te[i], 0, 0))
    return pl.pallas_call(
        functools.partial(_gmm_kernel, n_tok=n_tok),
        out_shape=jax.ShapeDtypeStruct((TOP_K * n_tok + MOE_TM, d), F32),
        grid_spec=pltpu.PrefetchScalarGridSpec(
            num_scalar_prefetch=3,
            grid=(steps,),
            in_specs=[pl.BlockSpec(memory_space=pl.ANY),
                      wspec((d, D_EXPERT)), wspec((d, D_EXPERT)), wspec((D_EXPERT, d))],
            out_specs=pl.BlockSpec(memory_space=pl.ANY),
            scratch_shapes=[pltpu.VMEM((MOE_TM // GATHER_PARTS, d), F32)] * (2 * GATHER_PARTS)
                           + [pltpu.VMEM((MOE_TM, d), F32)] * 2
                           + [pltpu.SemaphoreType.DMA((2 * GATHER_PARTS + 2,))],
        ),
        compiler_params=_cparams(("arbitrary",)),
        name="gmm",
    )(src, tile_expert, n_tiles, h2, w1, w3, w2)


def _route_tables(ids, n_tok):
    n_pairs = n_tok * TOP_K
    max_tiles = n_pairs // MOE_TM + N_EXPERTS
    e_flat = ids[:, :TOP_K].T.reshape(n_pairs)
    onehot = (e_flat[:, None] == jnp.arange(N_EXPERTS, dtype=I32)[None, :]).astype(I32)
    csum = jnp.cumsum(onehot, axis=0)
    counts = csum[-1]
    rank = jnp.sum(csum * onehot, axis=1) - 1
    tiles_per = (counts + MOE_TM - 1) // MOE_TM
    tile_end = jnp.cumsum(tiles_per)
    row_start = (tile_end - tiles_per + 1) * MOE_TM
    pos = row_start[e_flat] + rank
    src = jnp.full(((max_tiles + 3) * MOE_TM,), -1, I32).at[pos].set(jnp.arange(n_pairs, dtype=I32))
    steps = jnp.arange(max_tiles + 1, dtype=I32)
    owner = jnp.sum((tile_end[None, :] <= steps[:, None]).astype(I32), axis=1)
    tile_expert = jnp.minimum(owner, N_EXPERTS - 1)
    return src, tile_expert, tile_end[-1:].astype(I32)


def _combine_kernel(x1_ref, y0_ref, y1_ref, wts_ref, mod_ref, g_ref, o_ref):
    w = wts_ref[...]
    moe = w[:, 0:1] * y0_ref[...] + w[:, 1:2] * y1_ref[...]
    x2 = x1_ref[...] + mod_ref[0, 5:6, :] * moe
    o_ref[...] = _rms(x2) * g_ref[...]


def _combine(x1, y_pairs, wts, mods, mod_row0, g, row0, batch, t, n_tok):
    d = x1.shape[1]
    nt = t // CMB_TM
    rb0 = row0 // CMB_TM
    slot1 = n_tok // CMB_TM
    return pl.pallas_call(
        _combine_kernel,
        out_shape=jax.ShapeDtypeStruct((batch * t, d), F32),
        grid=(batch * nt,),
        in_specs=[pl.BlockSpec((CMB_TM, d), lambda i: (rb0 + i, 0)),
                  pl.BlockSpec((CMB_TM, d), lambda i: (rb0 + i, 0)),
                  pl.BlockSpec((CMB_TM, d), lambda i: (slot1 + rb0 + i, 0)),
                  pl.BlockSpec((CMB_TM, LANES), lambda i: (rb0 + i, 0)),
                  pl.BlockSpec((1, N_ADA, d), lambda i: (mod_row0 + i // nt, 0, 0)),
                  pl.BlockSpec((1, d), lambda i: (0, 0))],
        out_specs=pl.BlockSpec((CMB_TM, d), lambda i: (i, 0)),
        compiler_params=_cparams(("arbitrary",)),
        name="combine",
    )(x1, y_pairs, y_pairs, wts, mods, g.reshape(1, d))


def _rope_tables(n_lat):
    half = DA_DK // 2
    nf = half // 2
    t = jnp.arange(n_lat)
    row = (t // GRID_W).astype(F32)
    col = (t % GRID_W).astype(F32)
    inv = ROPE_THETA ** (-jnp.arange(nf, dtype=F32) / nf)
    lane = np.arange(HEAD_W)
    freq = inv[lane % nf]
    pos = jnp.where(((lane // half) % 2 == 0)[None, :], row[:, None], col[:, None])
    ang = pos * freq[None, :]
    sign = np.where((lane % half) < nf, -1.0, 1.0).astype(np.float32)
    return jnp.cos(ang), jnp.sin(ang) * sign[None, :]


def kernel(x_prompt, x_sample, cache_k, cache_v, state_hgrn_fwd, state_hgrn_bwd, c, c_ctx, w_ada, b_ada,
           norm1_g, norm2_g, norm_final_g, w_in, hg_lb_fwd, hg_lb_bwd, hg_norm_g, da_lambda_q1, da_lambda_k1,
           da_lambda_q2, da_lambda_k2, da_norm_g, w_out, router_g_w, router_g_b, router_e_w, router_e_b,
           exp_w1, exp_w3, exp_w2):
    l = 0
    batch, seq, d = x_prompt.shape
    dec_batch, dec_seq, _ = x_sample.shape
    n_ctx = batch * seq
    n_lat = dec_batch * dec_seq
    n_tok = n_ctx + n_lat
    lam_init = 0.8 - 0.6 * math.exp(-0.3 * l)

    cond = jnp.zeros((8, d), F32).at[0].set(c_ctx).at[1:1 + dec_batch].set(c)
    mods = _ada(cond, w_ada[l], b_ada[l]).reshape(8, N_ADA, d)

    w_in_bf16 = w_in[l].astype(BF16)
    proj_c = _in_proj(x_prompt.reshape(1, n_ctx, d), mods, 0, norm1_g[l], w_in_bf16, 1024, "in_proj_ctx")
    proj_l = _in_proj(x_sample, mods, 1, norm1_g[l], w_in_bf16, 1024, "in_proj_lat")
    lam_params = (da_lambda_q1[l], da_lambda_k1[l], da_lambda_q2[l], da_lambda_k2[l])

    da_ctx, new_k, new_v = _attn(proj_c, batch, seq, lam_params, da_norm_g[l], lam_init, emit_cache=True)
    hg_ctx, new_sf, new_sb = _hgrn(proj_c, batch, seq, hg_lb_fwd, hg_lb_bwd, hg_norm_g[l], emit_state=True)

    cos, sin = _rope_tables(dec_seq)
    (da_lat,) = _attn(proj_l, dec_batch, dec_seq, lam_params, da_norm_g[l], lam_init, cos=cos, sin=sin,
                      cache_k=cache_k[:, l:l + 1], cache_v=cache_v[:, l:l + 1])
    (hg_lat,) = _hgrn(proj_l, dec_batch, dec_seq, hg_lb_fwd, hg_lb_bwd, hg_norm_g[l],
                      s0f=state_hgrn_fwd[:, l:l + 1], s0b=state_hgrn_bwd[:, l:l + 1])

    rw = jnp.zeros((d, LANES), F32).at[:, :N_GROUPS].set(router_g_w[l]).at[:, N_GROUPS:N_GROUPS + N_EXPERTS].set(
        router_e_w[l]).astype(BF16)
    rb = jnp.zeros((1, LANES), F32).at[0, :N_GROUPS].set(router_g_b[l]).at[0, N_GROUPS:N_GROUPS + N_EXPERTS].set(
        router_e_b[l])
    x1, h2, ids, wts = _out_proj(hg_ctx, da_ctx, x_prompt.reshape(n_ctx, d), hg_lat, da_lat,
                                 x_sample.reshape(n_lat, d), w_out[l].astype(BF16), mods, norm2_g[l], rw, rb, dec_seq)

    src, tile_expert, n_tiles = _route_tables(ids, n_tok)
    y_pairs = _gmm(src, tile_expert, n_tiles, h2, exp_w1[l], exp_w3[l], exp_w2[l], n_tok)

    y_ctx = _combine(x1, y_pairs, wts, mods, 0, norm_final_g, 0, 1, n_ctx, n_tok)
    y_lat = _combine(x1, y_pairs, wts, mods, 1, norm_final_g, n_ctx, dec_batch, dec_seq, n_tok)
    return (y_ctx.reshape(batch, seq, d), y_lat.reshape(dec_batch, dec_seq, d), new_k, new_v, new_sf, new_sb)
```

```python
import functools
import math

import numpy as np
import jax
import jax.numpy as jnp
from jax import lax
from jax.experimental import pallas as pl
from jax.experimental.pallas import tpu as pltpu

F32 = jnp.float32
BF16 = jnp.bfloat16
I32 = jnp.int32

GRID_W = 64
HG_WIDTH = 1024
HG_DK = 128
HG_HEADS = 8
DA_HEADS = 8
DA_DK = 64
HEAD_W = 128
ROPE_THETA = 10000.0
N_GROUPS = 4
EXP_PER_GROUP = 8
N_EXPERTS = 32
TOP_K = 2
D_EXPERT = 512
N_ADA = 6
RMS_EPS = 1e-6
CB_Q_HG, CB_F_FW, CB_F_BW, CB_I_HG, CB_G_HG, CB_Q_DA, CB_K_DA, CB_V_DA = (8 * i for i in range(8))

LANES = 128
SUBLANES = 8
VMEM_LIMIT = 56 * 1024 * 1024

ADA_TN = 1536
IN_TN = 1024
OUT_TM = 256
HG_CHUNK = 64
HG_GROUP = 4
ATT_TQ = 256
ATT_TK = 512
MOE_TM = 256
CMB_TM = 512
NEG_INF = float("-inf")


def _cparams(sem):
    return pltpu.CompilerParams(dimension_semantics=sem, vmem_limit_bytes=VMEM_LIMIT)


def _silu(x):
    return x * jax.nn.sigmoid(x)


def _rms(x):
    return x * lax.rsqrt(jnp.mean(x * x, axis=-1, keepdims=True) + RMS_EPS)


def _ada_kernel(cond_ref, w_ref, b_ref, o_ref):
    s = _silu(cond_ref[...]).astype(BF16)
    o_ref[...] = jnp.dot(s, w_ref[...].astype(BF16), preferred_element_type=F32) + b_ref[...]


def _ada(cond, w, b):
    rows, d = cond.shape
    n = w.shape[1]
    return pl.pallas_call(
        _ada_kernel,
        out_shape=jax.ShapeDtypeStruct((rows, n), F32),
        grid=(n // ADA_TN,),
        in_specs=[pl.BlockSpec((rows, d), lambda j: (0, 0)),
                  pl.BlockSpec((d, ADA_TN), lambda j: (0, j)),
                  pl.BlockSpec((1, ADA_TN), lambda j: (0, j))],
        out_specs=pl.BlockSpec((rows, ADA_TN), lambda j: (0, j)),
        compiler_params=_cparams(("arbitrary",)),
        name="ada",
    )(cond, w, b.reshape(1, n))


def _in_kernel(x_ref, mod_ref, g_ref, w_ref, o_ref, h_ref):
    @pl.when(pl.program_id(1) == 0)
    def _():
        y = _rms(x_ref[...]) * g_ref[...]
        h_ref[...] = (y * (1.0 + mod_ref[0, 1:2, :]) + mod_ref[0, 0:1, :]).astype(BF16)

    o_ref[...] = jnp.dot(h_ref[...], w_ref[...], preferred_element_type=F32)


def _in_proj(x, mods, mod_row0, g, w_bf16, tm, name):
    batch, t, d = x.shape
    cols = w_bf16.shape[1]
    nt = t // tm
    return pl.pallas_call(
        _in_kernel,
        out_shape=jax.ShapeDtypeStruct((batch * t, cols), F32),
        grid=(batch * nt, cols // IN_TN),
        in_specs=[pl.BlockSpec((tm, d), lambda i, j: (i, 0)),
                  pl.BlockSpec((1, N_ADA, d), lambda i, j: (mod_row0 + i // nt, 0, 0)),
                  pl.BlockSpec((1, d), lambda i, j: (0, 0)),
                  pl.BlockSpec((d, IN_TN), lambda i, j: (0, j))],
        out_specs=pl.BlockSpec((tm, IN_TN), lambda i, j: (i, j)),
        scratch_shapes=[pltpu.VMEM((tm, d), BF16)],
        compiler_params=_cparams(("arbitrary", "arbitrary")),
        name=name,
    )(x.reshape(batch * t, d), mods, g.reshape(1, d), w_bf16)


def _rope(x, cos, sin_signed):
    lane = lax.broadcasted_iota(I32, x.shape, 1)
    first = (lane % 32) < 16
    partner = jnp.where(first, pltpu.roll(x, LANES - 16, 1), pltpu.roll(x, 16, 1))
    return x * cos + partner * sin_signed


def _split_maps(k):
    lane = lax.broadcasted_iota(I32, k.shape, 1)
    m1 = lane < DA_DK
    return jnp.where(m1, k, 0.0).astype(BF16), jnp.where(m1, 0.0, k).astype(BF16)


def _lower_bound(lb_ref):
    p = lb_ref[...]
    e = jnp.exp(p - jnp.max(p, axis=0, keepdims=True))
    return e[0:1, :] / jnp.sum(e, axis=0, keepdims=True)


def _hgrn_pair_masks(mask_ref, c):
    ti = lax.broadcasted_iota(I32, (c, c), 0)
    si = lax.broadcasted_iota(I32, (c, c), 1)
    for d in range(2):
        mask_ref[d, 0] = (ti == si).astype(F32)
        for j in range(c.bit_length() - 1):
            same = (ti >> (j + 1)) == (si >> (j + 1))
            t_bit = ((ti >> j) & 1) == 1
            s_bit = ((si >> j) & 1) == 1
            pair = (s_bit & jnp.logical_not(t_bit)) if d else (t_bit & jnp.logical_not(s_bit))
            mask_ref[d, 1 + j] = (same & pair).astype(F32)


def _hgrn_chunk(q, f_logit, v, lb, mask_ref, rev):
    c = q.shape[0]
    n_piece = c // SUBLANES
    levels = c.bit_length() - 1
    low = SUBLANES.bit_length() - 1
    d = 1 if rev else 0
    nt = (((1,), (1,)), ((), ()))
    pieces = lambda x: [x[SUBLANES * i:SUBLANES * (i + 1)] for i in range(n_piece)]
    whole = lambda xs: jnp.concatenate(xs, axis=0)

    fg = lb + (1.0 - lb) * jax.nn.sigmoid(f_logit)
    k = 1.0 - fg
    q_bf = q.astype(BF16)
    k_bf = k.astype(BF16)
    qs, ks, fgs = pieces(q), pieces(k), pieces(fg)
    tot = pieces(jnp.log(fg))
    pre = list(tot)
    suf = [jnp.zeros((SUBLANES, LANES), F32)] * n_piece
    sub = lax.broadcasted_iota(I32, (SUBLANES, LANES), 0)

    def pair_scores(qt, kt, idx):
        return lax.dot_general(qt, kt, nt, preferred_element_type=F32) * mask_ref[d, idx]

    scores = pair_scores(q_bf, k_bf, 0)
    yield
    for j in range(levels):
        if j < low:
            sh = 1 << j
            bit = ((sub >> j) & 1) == 1
            late = jnp.logical_not(bit) if rev else bit
            if j == 0:
                qt = whole([qs[i] * jnp.where(late, fgs[i], 1.0) for i in range(n_piece)]).astype(BF16)
                kt = k_bf
            else:
                es = [jnp.exp(jnp.where(late, pre[i], suf[i])) for i in range(n_piece)]
                qt = whole([qs[i] * es[i] for i in range(n_piece)]).astype(BF16)
                kt = whole([ks[i] * es[i] for i in range(n_piece)]).astype(BF16)
            for i in range(n_piece):
                up = pltpu.roll(tot[i], sh, 0)
                dn = pltpu.roll(tot[i], SUBLANES - sh, 0)
                sib = jnp.where(late, dn, up) if rev else jnp.where(late, up, dn)
                pre[i] = pre[i] + jnp.where(late, sib, 0.0)
                suf[i] = suf[i] + jnp.where(late, 0.0, sib)
                tot[i] = tot[i] + sib
        else:
            half = 1 << (j - low)
            upper = [(i // half) % 2 == 1 for i in range(n_piece)]
            late = [(not u) if rev else u for u in upper]
            es = [jnp.exp(pre[i] if late[i] else suf[i]) for i in range(n_piece)]
            qt = whole([qs[i] * es[i] for i in range(n_piece)]).astype(BF16)
            kt = whole([ks[i] * es[i] for i in range(n_piece)]).astype(BF16)
            sib = [tot[i - half] if upper[i] else tot[i + half] for i in range(n_piece)]
            pre = [pre[i] + sib[i] if late[i] else pre[i] for i in range(n_piece)]
            suf = [suf[i] if late[i] else suf[i] + sib[i] for i in range(n_piece)]
            tot = [tot[i] + sib[i] for i in range(n_piece)]
        scores = scores + pair_scores(qt, kt, 1 + j)
        yield

    q_dec = whole([qs[i] * jnp.exp(pre[i]) for i in range(n_piece)]).astype(BF16)
    k_dec = whole([ks[i] * jnp.exp(suf[i]) for i in range(n_piece)]).astype(BF16)
    return dict(q_dec=q_dec, k_dec=k_dec, v_t=v.T.astype(BF16), v=v.astype(BF16),
                scores=scores.astype(BF16), decay=jnp.exp(tot[0][0:1, :]))


def _hgrn_state_step(p, st_ref):
    st = st_ref[...]
    o = lax.dot_general(p["q_dec"], st.astype(BF16), (((1,), (1,)), ((), ())), preferred_element_type=F32)
    st_ref[...] = p["decay"] * st + jnp.dot(p["v_t"], p["k_dec"], preferred_element_type=F32)
    return o


def _lockstep(gens):
    results = [None] * len(gens)
    live = list(range(len(gens)))
    while live:
        for idx in list(live):
            try:
                next(gens[idx])
            except StopIteration as stop:
                results[idx] = stop.value
                live.remove(idx)
    return results


def _hgrn_kernel(*refs, t, has_init, emit_state):
    q_ref, ff_ref, fb_ref, i_ref, g_ref, lbf_ref, lbb_ref, ng_ref = refs[:8]
    pos = 8
    if has_init:
        s0f_ref, s0b_ref = refs[pos:pos + 2]
        pos += 2
    o_ref = refs[pos]
    pos += 1
    if emit_state:
        sf_ref, sb_ref = refs[pos:pos + 2]
        pos += 2
    of_scr, ob_scr, stf, stb, mask_scr = refs[pos:pos + 5]

    c = HG_CHUNK
    n = t // c

    @pl.when((pl.program_id(0) == 0) & (pl.program_id(1) == 0))
    def _():
        _hgrn_pair_masks(mask_scr, c)

    if has_init:
        stf[...] = s0f_ref[0, 0, 0].T
        stb[...] = s0b_ref[0, 0, 0].T
    else:
        stf[...] = jnp.zeros_like(stf)
        stb[...] = jnp.zeros_like(stb)
    lbf = _lower_bound(lbf_ref)
    lbb = _lower_bound(lbb_ref)

    def body(i, carry):
        work = []
        for u in range(HG_GROUP):
            ci = i * HG_GROUP + u
            work.append((pl.ds(pl.multiple_of(ci * c, c), c), ff_ref, lbf, stf, of_scr, False))
            work.append((pl.ds(pl.multiple_of((n - 1 - ci) * c, c), c), fb_ref, lbb, stb, ob_scr, True))
        parts = _lockstep([_hgrn_chunk(_silu(q_ref[sl, :]), f_ref[sl, :], i_ref[sl, :], lb, mask_scr, rev)
                           for sl, f_ref, lb, _, _, rev in work])
        outs = [_hgrn_state_step(p, w[3]) for p, w in zip(parts, work)]
        for p, w, o in zip(parts, work, outs):
            w[4][w[0], :] = o + jnp.dot(p["scores"], p["v"], preferred_element_type=F32)
        return carry

    lax.fori_loop(0, n // HG_GROUP, body, 0)

    fin = min(t, 256)

    def finish(i, carry):
        sl = pl.ds(pl.multiple_of(i * fin, fin), fin)
        o = _rms(of_scr[sl, :] + ob_scr[sl, :]) * ng_ref[...]
        o_ref[sl, :] = (o * _silu(g_ref[sl, :])).astype(BF16)
        return carry

    lax.fori_loop(0, t // fin, finish, 0)
    if emit_state:
        sf_ref[0, 0, 0] = stf[...].T
        sb_ref[0, 0, 0] = stb[...].T


def _hgrn(proj, batch, t, lb_fwd, lb_bwd, norm_g, s0f=None, s0b=None, emit_state=False):
    has_init = s0f is not None

    def col(cb):
        return pl.BlockSpec((t, HEAD_W), lambda b, h: (b, cb + h))

    lb_spec = pl.BlockSpec((lb_fwd.shape[0], HEAD_W), lambda b, h: (0, h))
    in_specs = [col(CB_Q_HG), col(CB_F_FW), col(CB_F_BW), col(CB_I_HG), col(CB_G_HG),
                lb_spec, lb_spec, pl.BlockSpec((1, HEAD_W), lambda b, h: (0, 0))]
    args = [proj] * 5 + [lb_fwd, lb_bwd, norm_g.reshape(1, HEAD_W)]
    st_spec = pl.BlockSpec((1, 1, 1, HG_DK, HEAD_W), lambda b, h: (b, 0, h, 0, 0))
    if has_init:
        in_specs += [st_spec, st_spec]
        args += [s0f, s0b]
    out_shape = [jax.ShapeDtypeStruct((batch * t, HG_WIDTH), BF16)]
    out_specs = [pl.BlockSpec((t, HEAD_W), lambda b, h: (b, h))]
    if emit_state:
        st = jax.ShapeDtypeStruct((batch, 1, HG_HEADS, HG_DK, HEAD_W), F32)
        out_shape += [st, st]
        out_specs += [st_spec, st_spec]
    return pl.pallas_call(
        functools.partial(_hgrn_kernel, t=t, has_init=has_init, emit_state=emit_state),
        out_shape=out_shape,
        grid=(batch, HG_HEADS),
        in_specs=in_specs,
        out_specs=out_specs,
        scratch_shapes=[pltpu.VMEM((t, HEAD_W), F32), pltpu.VMEM((t, HEAD_W), F32),
                        pltpu.VMEM((HEAD_W, HG_DK), F32), pltpu.VMEM((HEAD_W, HG_DK), F32),
                        pltpu.VMEM((2, HG_CHUNK.bit_length(), HG_CHUNK, HG_CHUNK), F32)],
        compiler_params=_cparams(("arbitrary", "arbitrary")),
        name="hgrn_lat" if has_init else "hgrn_ctx",
    )(*args)


def _attn_kernel(*refs, t, n_cache, use_rope, emit_cache, lam_init):
    it = iter(refs)
    q_ref, k_ref, v_ref = next(it), next(it), next(it)
    if use_rope:
        cosq_ref, sinq_ref, cos_ref, sin_ref = next(it), next(it), next(it), next(it)
    if n_cache:
        ck_ref, cv_ref = next(it), next(it)
    lq1_ref, lk1_ref, lq2_ref, lk2_ref, ng_ref, o_ref = (next(it) for _ in range(6))
    if emit_cache:
        nk_ref, nv_ref = next(it), next(it)
    k1_scr, k2_scr, v_scr, s_scr, p_scr = (next(it) for _ in range(5))
    k_scrs = (k1_scr, k2_scr)
    tq = q_ref.shape[0]

    @pl.when(pl.program_id(2) == 0)
    def _():
        rows = min(t, ATT_TK)

        def stage(i, carry):
            sl = pl.ds(pl.multiple_of(i * rows, rows), rows)
            k = k_ref[sl, :]
            v = v_ref[sl, :]
            if emit_cache:
                nk_ref[0, 0, 0, sl, :] = k
                nv_ref[0, 0, 0, sl, :] = v
            if use_rope:
                k = _rope(k, cos_ref[sl, :], sin_ref[sl, :])
            k1_scr[sl, :], k2_scr[sl, :] = _split_maps(k)
            v_scr[sl, :] = v.astype(BF16)
            return carry

        lax.fori_loop(0, t // rows, stage, 0)
        if n_cache:
            k1_scr[t:t + n_cache, :], k2_scr[t:t + n_cache, :] = _split_maps(ck_ref[0, 0, 0])
            v_scr[t:t + n_cache, :] = cv_ref[0, 0, 0].astype(BF16)

    q = q_ref[...]
    if use_rope:
        q = _rope(q, cosq_ref[...], sinq_ref[...])
    q = (q * (DA_DK ** -0.5)).astype(BF16)
    nt = (((1,), (1,)), ((), ()))
    n_keys = t + n_cache
    tiles = [(st, min(ATT_TK, n_keys - st)) for st in range(0, n_keys, ATT_TK)]

    def scores_tile(mp, st, sz, mx):
        s = lax.dot_general(q, k_scrs[mp][st:st + sz, :], nt, preferred_element_type=F32)
        s_scr[mp, :, st:st + sz] = s
        for j in range(sz // LANES):
            mx = jnp.maximum(mx, s[:, j * LANES:(j + 1) * LANES])
        return mx

    def exp_tile(mp, st, sz, m, part):
        for lo in range(st, st + sz, LANES):
            e = jnp.exp(s_scr[mp, :, lo:lo + LANES] - m)
            part = part + e
            p_scr[mp, :, lo:lo + LANES] = e.astype(BF16)
        return part

    def value_tile(mp, st, sz, acc):
        return acc + jnp.dot(p_scr[mp, :, st:st + sz], v_scr[st:st + sz, :], preferred_element_type=F32)

    def row_stat(x, op):
        return jnp.broadcast_to(op(x, axis=-1, keepdims=True), (tq, LANES))

    neg = jnp.full((tq, LANES), NEG_INF, F32)
    zero = jnp.zeros((tq, LANES), F32)
    mx = neg
    for st, sz in tiles:
        mx = scores_tile(0, st, sz, mx)
    m0 = row_stat(mx, jnp.max)
    mx, part0 = neg, zero
    for st, sz in tiles:
        mx = scores_tile(1, st, sz, mx)
        part0 = exp_tile(0, st, sz, m0, part0)
    m1 = row_stat(mx, jnp.max)
    acc0, part1 = zero, zero
    for st, sz in tiles:
        acc0 = value_tile(0, st, sz, acc0)
        part1 = exp_tile(1, st, sz, m1, part1)
    acc1 = jnp.dot(p_scr[1], v_scr[...], preferred_element_type=F32)

    lam = (jnp.exp(jnp.sum(lq1_ref[...] * lk1_ref[...], axis=-1, keepdims=True))
           - jnp.exp(jnp.sum(lq2_ref[...] * lk2_ref[...], axis=-1, keepdims=True)) + lam_init)
    o = acc0 / row_stat(part0, jnp.sum) - lam * (acc1 / row_stat(part1, jnp.sum))
    o_ref[...] = (_rms(o) * ng_ref[...] * (1.0 - lam_init)).astype(BF16)


def _attn(proj, batch, t, lam_params, norm_g, lam_init, cos=None, sin=None, cache_k=None, cache_v=None,
          emit_cache=False):
    use_rope = cos is not None
    n_cache = 0 if cache_k is None else cache_k.shape[3]
    nq = t // ATT_TQ
    n_keys = t + n_cache

    def col(cb, rows, row_map):
        return pl.BlockSpec((rows, HEAD_W), lambda b, h, i: (row_map(b, i), cb + h))

    in_specs = [col(CB_Q_DA, ATT_TQ, lambda b, i: b * nq + i),
                col(CB_K_DA, t, lambda b, i: b), col(CB_V_DA, t, lambda b, i: b)]
    args = [proj, proj, proj]
    if use_rope:
        in_specs += [pl.BlockSpec((ATT_TQ, HEAD_W), lambda b, h, i: (i, 0))] * 2
        in_specs += [pl.BlockSpec((t, HEAD_W), lambda b, h, i: (0, 0))] * 2
        args += [cos, sin, cos, sin]
    cache_spec = lambda n: pl.BlockSpec((1, 1, 1, n, HEAD_W), lambda b, h, i: (b, 0, h, 0, 0))
    if n_cache:
        in_specs += [cache_spec(n_cache)] * 2
        args += [cache_k, cache_v]
    small = pl.BlockSpec((1, DA_DK), lambda b, h, i: (0, 0))
    in_specs += [small] * 4 + [pl.BlockSpec((1, HEAD_W), lambda b, h, i: (0, 0))]
    args += [p.reshape(1, DA_DK) for p in lam_params] + [norm_g.reshape(1, HEAD_W)]
    out_shape = [jax.ShapeDtypeStruct((batch * t, DA_HEADS * HEAD_W), BF16)]
    out_specs = [pl.BlockSpec((ATT_TQ, HEAD_W), lambda b, h, i: (b * nq + i, h))]
    if emit_cache:
        out_shape += [jax.ShapeDtypeStruct((batch, 1, DA_HEADS, t, HEAD_W), F32)] * 2
        out_specs += [cache_spec(t)] * 2
    scratch = [pltpu.VMEM((n_keys, HEAD_W), BF16)] * 3
    scratch += [pltpu.VMEM((2, ATT_TQ, n_keys), F32), pltpu.VMEM((2, ATT_TQ, n_keys), BF16)]
    return pl.pallas_call(
        functools.partial(_attn_kernel, t=t, n_cache=n_cache, use_rope=use_rope, emit_cache=emit_cache,
                          lam_init=lam_init),
        out_shape=out_shape,
        grid=(batch, DA_HEADS, nq),
        in_specs=in_specs,
        out_specs=out_specs,
        scratch_shapes=scratch,
        compiler_params=_cparams(("arbitrary",) * 3),
        name="attn_lat" if use_rope else "attn_ctx",
    )(*args)


def _out_kernel(mhg_c, mda_c, x_c, mhg_l, mda_l, x_l, w_ref, mod_ref, g_ref, rw_ref, rb_ref,
                x1_ref, h2_ref, ids_ref, wts_ref, *, ctx_tiles):
    def body(mhg_ref, mda_ref, x_ref):
        mix = jnp.dot(mhg_ref[...], w_ref[0:HG_WIDTH, :], preferred_element_type=F32)
        mix = mix + jnp.dot(mda_ref[...], w_ref[HG_WIDTH:, :], preferred_element_type=F32)
        x1 = x_ref[...] + mod_ref[0, 2:3, :] * mix
        x1_ref[...] = x1
        h2 = (_rms(x1) * g_ref[...]) * (1.0 + mod_ref[0, 4:5, :]) + mod_ref[0, 3:4, :]
        h2_ref[...] = h2

        h_hi = h2.astype(BF16)
        h_lo = (h2 - h_hi.astype(F32)).astype(BF16)
        logit = (jnp.dot(h_hi, rw_ref[...], preferred_element_type=F32)
                 + jnp.dot(h_lo, rw_ref[...], preferred_element_type=F32)) + rb_ref[...]
        lane = lax.broadcasted_iota(I32, logit.shape, 1)

        def first_max(x):
            m = jnp.max(x, axis=-1, keepdims=True)
            return m, jnp.min(jnp.where(x == m, lane, LANES), axis=-1, keepdims=True)

        gmask = lane < N_GROUPS
        gmax, gsel = first_max(jnp.where(gmask, logit, NEG_INF))
        p_grp = 1.0 / jnp.sum(jnp.where(gmask, jnp.exp(logit - gmax), 0.0), axis=-1, keepdims=True)
        lo = N_GROUPS + EXP_PER_GROUP * gsel
        le = jnp.where((lane >= lo) & (lane < lo + EXP_PER_GROUP), logit, NEG_INF)
        v1, i1 = first_max(le)
        v2, i2 = first_max(jnp.where(lane == i1, NEG_INF, le))
        e = jnp.exp(v2 - v1)
        w1 = p_grp / (1.0 + e)
        w2 = p_grp * e / (1.0 + e)
        ids_ref[...] = jnp.where(lane == 0, i1 - N_GROUPS, jnp.where(lane == 1, i2 - N_GROUPS, 0))
        wts_ref[...] = jnp.where(lane == 0, w1, jnp.where(lane == 1, w2, 0.0))

    i = pl.program_id(0)
    pl.when(i < ctx_tiles)(lambda: body(mhg_c, mda_c, x_c))
    pl.when(i >= ctx_tiles)(lambda: body(mhg_l, mda_l, x_l))


def _out_proj(mhg_c, mda_c, x_c, mhg_l, mda_l, x_l, w_bf16, mods, g, rw, rb, lat_t):
    n_ctx, d = x_c.shape
    n_lat = x_l.shape[0]
    n = n_ctx + n_lat
    ctx_tiles = n_ctx // OUT_TM
    lat_tiles = lat_t // OUT_TM
    row = lambda i: (i, 0)
    const = lambda i: (0, 0)
    crow = lambda i: (jnp.minimum(i, ctx_tiles - 1), 0)
    lrow = lambda i: (jnp.maximum(i - ctx_tiles, 0), 0)
    seg = lambda i: (jnp.where(i < ctx_tiles, 0, 1 + (i - ctx_tiles) // lat_tiles), 0, 0)
    half = lambda m: pl.BlockSpec((OUT_TM, HG_WIDTH), m)
    full = lambda m: pl.BlockSpec((OUT_TM, d), m)
    return pl.pallas_call(
        functools.partial(_out_kernel, ctx_tiles=ctx_tiles),
        out_shape=[jax.ShapeDtypeStruct((n, d), F32), jax.ShapeDtypeStruct((n, d), F32),
                   jax.ShapeDtypeStruct((n, LANES), I32), jax.ShapeDtypeStruct((n, LANES), F32)],
        grid=(n // OUT_TM,),
        in_specs=[half(crow), half(crow), full(crow), half(lrow), half(lrow), full(lrow),
                  pl.BlockSpec((d, d), const), pl.BlockSpec((1, N_ADA, d), seg), pl.BlockSpec((1, d), const),
                  pl.BlockSpec((d, LANES), const), pl.BlockSpec((1, LANES), const)],
        out_specs=[full(row), full(row), pl.BlockSpec((OUT_TM, LANES), row), pl.BlockSpec((OUT_TM, LANES), row)],
        compiler_params=_cparams(("arbitrary",)),
        name="out_proj",
    )(mhg_c, mda_c, x_c, mhg_l, mda_l, x_l, w_bf16, mods, g.reshape(1, d), rw, rb)


def _gmm_kernel(src_ref, texp_ref, ntile_ref, h_hbm, w1_ref, w3_ref, w2_ref, y_hbm,
                x0, x1, y0, y1, sems, *, n_tok):
    i = pl.program_id(0)
    n_tiles = ntile_ref[0]
    xs = (x0, x1)
    ys = (y0, y1)

    def gather_copy(tile, r, slot):
        p = jnp.maximum(src_ref[(tile + 1) * MOE_TM + r], 0)
        tok = jnp.where(p >= n_tok, p - n_tok, p)
        return pltpu.make_async_copy(h_hbm.at[pl.ds(tok, 1)], xs[slot].at[pl.ds(r, 1)], sems.at[slot])

    def scatter_copy(tile, r, slot):
        p = src_ref[(tile + 1) * MOE_TM + r]
        dst = jnp.where(p >= 0, p, TOP_K * n_tok + r)
        return pltpu.make_async_copy(ys[slot].at[pl.ds(r, 1)], y_hbm.at[pl.ds(dst, 1)], sems.at[2 + slot])

    @pl.when(i == 0)
    def _():
        y1[...] = jnp.zeros_like(y1)
        for r in range(MOE_TM):
            gather_copy(0, r, 0).start()

    def wait_gather(slot):
        pltpu.make_async_copy(h_hbm.at[pl.ds(0, MOE_TM)], xs[slot], sems.at[slot]).wait()

    def wait_scatter(slot):
        pltpu.make_async_copy(ys[slot], y_hbm.at[pl.ds(0, MOE_TM)], sems.at[2 + slot]).wait()

    def step(cur):
        nxt = 1 - cur
        wait_gather(cur)
        for r in range(MOE_TM):
            gather_copy(i + 1, r, nxt).start()
        for r in range(MOE_TM):
            scatter_copy(i - 1, r, nxt).start()
        x = xs[cur][...].astype(BF16)
        a = jnp.dot(x, w1_ref[0].astype(BF16), preferred_element_type=F32)
        b = jnp.dot(x, w3_ref[0].astype(BF16), preferred_element_type=F32)
        hid = (_silu(a) * b).astype(BF16)
        ys[cur][...] = jnp.dot(hid, w2_ref[0].astype(BF16), preferred_element_type=F32)
        wait_scatter(nxt)

        @pl.when(i == n_tiles)
        def _():
            wait_gather(nxt)

    live = i <= n_tiles
    pl.when(live & (i % 2 == 0))(lambda: step(0))
    pl.when(live & (i % 2 == 1))(lambda: step(1))


def _gmm(src, tile_expert, n_tiles, h2, w1, w3, w2, n_tok):
    d = h2.shape[1]
    steps = tile_expert.shape[0]
    wspec = lambda shape: pl.BlockSpec((1,) + shape, lambda i, src, te, nt: (te[i], 0, 0))
    return pl.pallas_call(
        functools.partial(_gmm_kernel, n_tok=n_tok),
        out_shape=jax.ShapeDtypeStruct((TOP_K * n_tok + MOE_TM, d), F32),
        grid_spec=pltpu.PrefetchScalarGridSpec(
            num_scalar_prefetch=3,
            grid=(steps,),
            in_specs=[pl.BlockSpec(memory_space=pl.ANY),
                      wspec((d, D_EXPERT)), wspec((d, D_EXPERT)), wspec((D_EXPERT, d))],
            out_specs=pl.BlockSpec(memory_space=pl.ANY),
            scratch_shapes=[pltpu.VMEM((MOE_TM, d), F32)] * 4 + [pltpu.SemaphoreType.DMA((4,))],
        ),
        compiler_params=_cparams(("arbitrary",)),
        name="gmm",
    )(src, tile_expert, n_tiles, h2, w1, w3, w2)


def _route_tables(ids, n_tok):
    n_pairs = n_tok * TOP_K
    max_tiles = n_pairs // MOE_TM + N_EXPERTS
    e_flat = ids[:, :TOP_K].T.reshape(n_pairs)
    onehot = (e_flat[:, None] == jnp.arange(N_EXPERTS, dtype=I32)[None, :]).astype(I32)
    csum = jnp.cumsum(onehot, axis=0)
    counts = csum[-1]
    rank = jnp.sum(csum * onehot, axis=1) - 1
    tiles_per = (counts + MOE_TM - 1) // MOE_TM
    tile_end = jnp.cumsum(tiles_per)
    row_start = (tile_end - tiles_per + 1) * MOE_TM
    pos = row_start[e_flat] + rank
    src = jnp.full(((max_tiles + 3) * MOE_TM,), -1, I32).at[pos].set(jnp.arange(n_pairs, dtype=I32))
    steps = jnp.arange(max_tiles + 1, dtype=I32)
    owner = jnp.sum((tile_end[None, :] <= steps[:, None]).astype(I32), axis=1)
    tile_expert = jnp.minimum(owner, N_EXPERTS - 1)
    return src, tile_expert, tile_end[-1:].astype(I32)


def _combine_kernel(x1_ref, y0_ref, y1_ref, wts_ref, mod_ref, g_ref, o_ref):
    w = wts_ref[...]
    moe = w[:, 0:1] * y0_ref[...] + w[:, 1:2] * y1_ref[...]
    x2 = x1_ref[...] + mod_ref[0, 5:6, :] * moe
    o_ref[...] = _rms(x2) * g_ref[...]


def _combine(x1, y_pairs, wts, mods, mod_row0, g, row0, batch, t, n_tok):
    d = x1.shape[1]
    nt = t // CMB_TM
    rb0 = row0 // CMB_TM
    slot1 = n_tok // CMB_TM
    return pl.pallas_call(
        _combine_kernel,
        out_shape=jax.ShapeDtypeStruct((batch * t, d), F32),
        grid=(batch * nt,),
        in_specs=[pl.BlockSpec((CMB_TM, d), lambda i: (rb0 + i, 0)),
                  pl.BlockSpec((CMB_TM, d), lambda i: (rb0 + i, 0)),
                  pl.BlockSpec((CMB_TM, d), lambda i: (slot1 + rb0 + i, 0)),
                  pl.BlockSpec((CMB_TM, LANES), lambda i: (rb0 + i, 0)),
                  pl.BlockSpec((1, N_ADA, d), lambda i: (mod_row0 + i // nt, 0, 0)),
                  pl.BlockSpec((1, d), lambda i: (0, 0))],
        out_specs=pl.BlockSpec((CMB_TM, d), lambda i: (i, 0)),
        compiler_params=_cparams(("arbitrary",)),
        name="combine",
    )(x1, y_pairs, y_pairs, wts, mods, g.reshape(1, d))


def _rope_tables(n_lat):
    half = DA_DK // 2
    nf = half // 2
    t = jnp.arange(n_lat)
    row = (t // GRID_W).astype(F32)
    col = (t % GRID_W).astype(F32)
    inv = ROPE_THETA ** (-jnp.arange(nf, dtype=F32) / nf)
    lane = np.arange(HEAD_W)
    freq = inv[lane % nf]
    pos = jnp.where(((lane // half) % 2 == 0)[None, :], row[:, None], col[:, None])
    ang = pos * freq[None, :]
    sign = np.where((lane % half) < nf, -1.0, 1.0).astype(np.float32)
    return jnp.cos(ang), jnp.sin(ang) * sign[None, :]


def kernel(x_prompt, x_sample, cache_k, cache_v, state_hgrn_fwd, state_hgrn_bwd, c, c_ctx, w_ada, b_ada,
           norm1_g, norm2_g, norm_final_g, w_in, hg_lb_fwd, hg_lb_bwd, hg_norm_g, da_lambda_q1, da_lambda_k1,
           da_lambda_q2, da_lambda_k2, da_norm_g, w_out, router_g_w, router_g_b, router_e_w, router_e_b,
           exp_w1, exp_w3, exp_w2):
    l = 0
    batch, seq, d = x_prompt.shape
    dec_batch, dec_seq, _ = x_sample.shape
    n_ctx = batch * seq
    n_lat = dec_batch * dec_seq
    n_tok = n_ctx + n_lat
    lam_init = 0.8 - 0.6 * math.exp(-0.3 * l)

    cond = jnp.zeros((8, d), F32).at[0].set(c_ctx).at[1:1 + dec_batch].set(c)
    mods = _ada(cond, w_ada[l], b_ada[l]).reshape(8, N_ADA, d)

    w_in_bf16 = w_in[l].astype(BF16)
    proj_c = _in_proj(x_prompt.reshape(1, n_ctx, d), mods, 0, norm1_g[l], w_in_bf16, 1024, "in_proj_ctx")
    proj_l = _in_proj(x_sample, mods, 1, norm1_g[l], w_in_bf16, 1024, "in_proj_lat")
    lam_params = (da_lambda_q1[l], da_lambda_k1[l], da_lambda_q2[l], da_lambda_k2[l])

    da_ctx, new_k, new_v = _attn(proj_c, batch, seq, lam_params, da_norm_g[l], lam_init, emit_cache=True)
    hg_ctx, new_sf, new_sb = _hgrn(proj_c, batch, seq, hg_lb_fwd, hg_lb_bwd, hg_norm_g[l], emit_state=True)

    cos, sin = _rope_tables(dec_seq)
    (da_lat,) = _attn(proj_l, dec_batch, dec_seq, lam_params, da_norm_g[l], lam_init, cos=cos, sin=sin,
                      cache_k=cache_k[:, l:l + 1], cache_v=cache_v[:, l:l + 1])
    (hg_lat,) = _hgrn(proj_l, dec_batch, dec_seq, hg_lb_fwd, hg_lb_bwd, hg_norm_g[l],
                      s0f=state_hgrn_fwd[:, l:l + 1], s0b=state_hgrn_bwd[:, l:l + 1])

    rw = jnp.zeros((d, LANES), F32).at[:, :N_GROUPS].set(router_g_w[l]).at[:, N_GROUPS:N_GROUPS + N_EXPERTS].set(
        router_e_w[l]).astype(BF16)
    rb = jnp.zeros((1, LANES), F32).at[0, :N_GROUPS].set(router_g_b[l]).at[0, N_GROUPS:N_GROUPS + N_EXPERTS].set(
        router_e_b[l])
    x1, h2, ids, wts = _out_proj(hg_ctx, da_ctx, x_prompt.reshape(n_ctx, d), hg_lat, da_lat,
                                 x_sample.reshape(n_lat, d), w_out[l].astype(BF16), mods, norm2_g[l], rw, rb, dec_seq)

    src, tile_expert, n_tiles = _route_tables(ids, n_tok)
    y_pairs = _gmm(src, tile_expert, n_tiles, h2, exp_w1[l], exp_w3[l], exp_w2[l], n_tok)

    y_ctx = _combine(x1, y_pairs, wts, mods, 0, norm_final_g, 0, 1, n_ctx, n_tok)
    y_lat = _combine(x1, y_pairs, wts, mods, 1, norm_final_g, n_ctx, dec_batch, dec_seq, n_tok)
    return (y_ctx.reshape(batch, seq, d), y_lat.reshape(dec_batch, dec_seq, d), new_k, new_v, new_sf, new_sb)
```

```python
import functools
import math

import numpy as np
import jax
import jax.numpy as jnp
from jax import lax
from jax.experimental import pallas as pl
from jax.experimental.pallas import tpu as pltpu

F32 = jnp.float32
BF16 = jnp.bfloat16
I32 = jnp.int32

GRID_W = 64
HG_WIDTH = 1024
HG_DK = 128
HG_HEADS = 8
DA_HEADS = 8
DA_DK = 64
HEAD_W = 128
ROPE_THETA = 10000.0
N_GROUPS = 4
EXP_PER_GROUP = 8
N_EXPERTS = 32
TOP_K = 2
D_EXPERT = 512
N_ADA = 6
RMS_EPS = 1e-6
CB_Q_HG, CB_F_FW, CB_F_BW, CB_I_HG, CB_G_HG, CB_Q_DA, CB_K_DA, CB_V_DA = (8 * i for i in range(8))

LANES = 128
SUBLANES = 8
VMEM_LIMIT = 56 * 1024 * 1024

ADA_TN = 1536
IN_TN = 1024
OUT_TM = 256
HG_CHUNK = 64
HG_GROUP = 4
ATT_TQ = 512
ATT_TK = 512
MOE_TM = 256
CMB_TM = 512
NEG_INF = float("-inf")


def _cparams(sem):
    return pltpu.CompilerParams(dimension_semantics=sem, vmem_limit_bytes=VMEM_LIMIT)


def _silu(x):
    return x * jax.nn.sigmoid(x)


def _rms(x):
    return x * lax.rsqrt(jnp.mean(x * x, axis=-1, keepdims=True) + RMS_EPS)


def _ada_kernel(cond_ref, w_ref, b_ref, o_ref):
    s = _silu(cond_ref[...]).astype(BF16)
    o_ref[...] = jnp.dot(s, w_ref[...].astype(BF16), preferred_element_type=F32) + b_ref[...]


def _ada(cond, w, b):
    rows, d = cond.shape
    n = w.shape[1]
    return pl.pallas_call(
        _ada_kernel,
        out_shape=jax.ShapeDtypeStruct((rows, n), F32),
        grid=(n // ADA_TN,),
        in_specs=[pl.BlockSpec((rows, d), lambda j: (0, 0)),
                  pl.BlockSpec((d, ADA_TN), lambda j: (0, j)),
                  pl.BlockSpec((1, ADA_TN), lambda j: (0, j))],
        out_specs=pl.BlockSpec((rows, ADA_TN), lambda j: (0, j)),
        compiler_params=_cparams(("arbitrary",)),
        name="ada",
    )(cond, w, b.reshape(1, n))


def _in_kernel(x_ref, mod_ref, g_ref, w_ref, o_ref, h_ref):
    @pl.when(pl.program_id(1) == 0)
    def _():
        y = _rms(x_ref[...]) * g_ref[...]
        h_ref[...] = (y * (1.0 + mod_ref[0, 1:2, :]) + mod_ref[0, 0:1, :]).astype(BF16)

    o_ref[...] = jnp.dot(h_ref[...], w_ref[...], preferred_element_type=F32)


def _in_proj(x, mods, mod_row0, g, w_bf16, tm, name):
    batch, t, d = x.shape
    cols = w_bf16.shape[1]
    nt = t // tm
    return pl.pallas_call(
        _in_kernel,
        out_shape=jax.ShapeDtypeStruct((batch * t, cols), F32),
        grid=(batch * nt, cols // IN_TN),
        in_specs=[pl.BlockSpec((tm, d), lambda i, j: (i, 0)),
                  pl.BlockSpec((1, N_ADA, d), lambda i, j: (mod_row0 + i // nt, 0, 0)),
                  pl.BlockSpec((1, d), lambda i, j: (0, 0)),
                  pl.BlockSpec((d, IN_TN), lambda i, j: (0, j))],
        out_specs=pl.BlockSpec((tm, IN_TN), lambda i, j: (i, j)),
        scratch_shapes=[pltpu.VMEM((tm, d), BF16)],
        compiler_params=_cparams(("arbitrary", "arbitrary")),
        name=name,
    )(x.reshape(batch * t, d), mods, g.reshape(1, d), w_bf16)


def _rope(x, cos, sin_signed):
    lane = lax.broadcasted_iota(I32, x.shape, 1)
    first = (lane % 32) < 16
    partner = jnp.where(first, pltpu.roll(x, LANES - 16, 1), pltpu.roll(x, 16, 1))
    return x * cos + partner * sin_signed


def _split_maps(k):
    lane = lax.broadcasted_iota(I32, k.shape, 1)
    m1 = lane < DA_DK
    return jnp.where(m1, k, 0.0).astype(BF16), jnp.where(m1, 0.0, k).astype(BF16)


def _lower_bound(lb_ref):
    p = lb_ref[...]
    e = jnp.exp(p - jnp.max(p, axis=0, keepdims=True))
    return e[0:1, :] / jnp.sum(e, axis=0, keepdims=True)


def _hgrn_pair_masks(mask_ref, c):
    ti = lax.broadcasted_iota(I32, (c, c), 0)
    si = lax.broadcasted_iota(I32, (c, c), 1)
    for d in range(2):
        mask_ref[d, 0] = (ti == si).astype(F32)
        for j in range(c.bit_length() - 1):
            same = (ti >> (j + 1)) == (si >> (j + 1))
            t_bit = ((ti >> j) & 1) == 1
            s_bit = ((si >> j) & 1) == 1
            pair = (s_bit & jnp.logical_not(t_bit)) if d else (t_bit & jnp.logical_not(s_bit))
            mask_ref[d, 1 + j] = (same & pair).astype(F32)


def _hgrn_chunk(q, f_logit, v, lb, mask_ref, rev):
    c = q.shape[0]
    n_piece = c // SUBLANES
    levels = c.bit_length() - 1
    low = SUBLANES.bit_length() - 1
    d = 1 if rev else 0
    nt = (((1,), (1,)), ((), ()))
    pieces = lambda x: [x[SUBLANES * i:SUBLANES * (i + 1)] for i in range(n_piece)]
    whole = lambda xs: jnp.concatenate(xs, axis=0)

    fg = lb + (1.0 - lb) * jax.nn.sigmoid(f_logit)
    k = 1.0 - fg
    q_bf = q.astype(BF16)
    k_bf = k.astype(BF16)
    qs, ks, fgs = pieces(q), pieces(k), pieces(fg)
    tot = pieces(jnp.log(fg))
    pre = list(tot)
    suf = [jnp.zeros((SUBLANES, LANES), F32)] * n_piece
    sub = lax.broadcasted_iota(I32, (SUBLANES, LANES), 0)

    def pair_scores(qt, kt, idx):
        return lax.dot_general(qt, kt, nt, preferred_element_type=F32) * mask_ref[d, idx]

    scores = pair_scores(q_bf, k_bf, 0)
    yield
    for j in range(levels):
        if j < low:
            sh = 1 << j
            bit = ((sub >> j) & 1) == 1
            late = jnp.logical_not(bit) if rev else bit
            if j == 0:
                qt = whole([qs[i] * jnp.where(late, fgs[i], 1.0) for i in range(n_piece)]).astype(BF16)
                kt = k_bf
            else:
                es = [jnp.exp(jnp.where(late, pre[i], suf[i])) for i in range(n_piece)]
                qt = whole([qs[i] * es[i] for i in range(n_piece)]).astype(BF16)
                kt = whole([ks[i] * es[i] for i in range(n_piece)]).astype(BF16)
            for i in range(n_piece):
                up = pltpu.roll(tot[i], sh, 0)
                dn = pltpu.roll(tot[i], SUBLANES - sh, 0)
                sib = jnp.where(late, dn, up) if rev else jnp.where(late, up, dn)
                pre[i] = pre[i] + jnp.where(late, sib, 0.0)
                suf[i] = suf[i] + jnp.where(late, 0.0, sib)
                tot[i] = tot[i] + sib
        else:
            half = 1 << (j - low)
            upper = [(i // half) % 2 == 1 for i in range(n_piece)]
            late = [(not u) if rev else u for u in upper]
            es = [jnp.exp(pre[i] if late[i] else suf[i]) for i in range(n_piece)]
            qt = whole([qs[i] * es[i] for i in range(n_piece)]).astype(BF16)
            kt = whole([ks[i] * es[i] for i in range(n_piece)]).astype(BF16)
            sib = [tot[i - half] if upper[i] else tot[i + half] for i in range(n_piece)]
            pre = [pre[i] + sib[i] if late[i] else pre[i] for i in range(n_piece)]
            suf = [suf[i] if late[i] else suf[i] + sib[i] for i in range(n_piece)]
            tot = [tot[i] + sib[i] for i in range(n_piece)]
        scores = scores + pair_scores(qt, kt, 1 + j)
        yield

    q_dec = whole([qs[i] * jnp.exp(pre[i]) for i in range(n_piece)]).astype(BF16)
    k_dec = whole([ks[i] * jnp.exp(suf[i]) for i in range(n_piece)]).astype(BF16)
    return dict(q_dec=q_dec, k_dec=k_dec, v_t=v.T.astype(BF16), v=v.astype(BF16),
                scores=scores.astype(BF16), decay=jnp.exp(tot[0][0:1, :]))


def _hgrn_state_step(p, st_ref):
    st = st_ref[...]
    o = lax.dot_general(p["q_dec"], st.astype(BF16), (((1,), (1,)), ((), ())), preferred_element_type=F32)
    st_ref[...] = p["decay"] * st + jnp.dot(p["v_t"], p["k_dec"], preferred_element_type=F32)
    return o


def _lockstep(gens):
    results = [None] * len(gens)
    live = list(range(len(gens)))
    while live:
        for idx in list(live):
            try:
                next(gens[idx])
            except StopIteration as stop:
                results[idx] = stop.value
                live.remove(idx)
    return results


def _hgrn_kernel(*refs, t, has_init, emit_state):
    q_ref, ff_ref, fb_ref, i_ref, g_ref, lbf_ref, lbb_ref, ng_ref = refs[:8]
    pos = 8
    if has_init:
        s0f_ref, s0b_ref = refs[pos:pos + 2]
        pos += 2
    o_ref = refs[pos]
    pos += 1
    if emit_state:
        sf_ref, sb_ref = refs[pos:pos + 2]
        pos += 2
    of_scr, ob_scr, stf, stb, mask_scr = refs[pos:pos + 5]

    c = HG_CHUNK
    n = t // c

    @pl.when((pl.program_id(0) == 0) & (pl.program_id(1) == 0))
    def _():
        _hgrn_pair_masks(mask_scr, c)

    if has_init:
        stf[...] = s0f_ref[0, 0, 0].T
        stb[...] = s0b_ref[0, 0, 0].T
    else:
        stf[...] = jnp.zeros_like(stf)
        stb[...] = jnp.zeros_like(stb)
    lbf = _lower_bound(lbf_ref)
    lbb = _lower_bound(lbb_ref)

    def body(i, carry):
        work = []
        for u in range(HG_GROUP):
            ci = i * HG_GROUP + u
            work.append((pl.ds(pl.multiple_of(ci * c, c), c), ff_ref, lbf, stf, of_scr, False))
            work.append((pl.ds(pl.multiple_of((n - 1 - ci) * c, c), c), fb_ref, lbb, stb, ob_scr, True))
        parts = _lockstep([_hgrn_chunk(_silu(q_ref[sl, :]), f_ref[sl, :], i_ref[sl, :], lb, mask_scr, rev)
                           for sl, f_ref, lb, _, _, rev in work])
        outs = [_hgrn_state_step(p, w[3]) for p, w in zip(parts, work)]
        for p, w, o in zip(parts, work, outs):
            w[4][w[0], :] = o + jnp.dot(p["scores"], p["v"], preferred_element_type=F32)
        return carry

    lax.fori_loop(0, n // HG_GROUP, body, 0)

    fin = min(t, 256)

    def finish(i, carry):
        sl = pl.ds(pl.multiple_of(i * fin, fin), fin)
        o = _rms(of_scr[sl, :] + ob_scr[sl, :]) * ng_ref[...]
        o_ref[sl, :] = (o * _silu(g_ref[sl, :])).astype(BF16)
        return carry

    lax.fori_loop(0, t // fin, finish, 0)
    if emit_state:
        sf_ref[0, 0, 0] = stf[...].T
        sb_ref[0, 0, 0] = stb[...].T


def _hgrn(proj, batch, t, lb_fwd, lb_bwd, norm_g, s0f=None, s0b=None, emit_state=False):
    has_init = s0f is not None

    def col(cb):
        return pl.BlockSpec((t, HEAD_W), lambda b, h: (b, cb + h))

    lb_spec = pl.BlockSpec((lb_fwd.shape[0], HEAD_W), lambda b, h: (0, h))
    in_specs = [col(CB_Q_HG), col(CB_F_FW), col(CB_F_BW), col(CB_I_HG), col(CB_G_HG),
                lb_spec, lb_spec, pl.BlockSpec((1, HEAD_W), lambda b, h: (0, 0))]
    args = [proj] * 5 + [lb_fwd, lb_bwd, norm_g.reshape(1, HEAD_W)]
    st_spec = pl.BlockSpec((1, 1, 1, HG_DK, HEAD_W), lambda b, h: (b, 0, h, 0, 0))
    if has_init:
        in_specs += [st_spec, st_spec]
        args += [s0f, s0b]
    out_shape = [jax.ShapeDtypeStruct((batch * t, HG_WIDTH), BF16)]
    out_specs = [pl.BlockSpec((t, HEAD_W), lambda b, h: (b, h))]
    if emit_state:
        st = jax.ShapeDtypeStruct((batch, 1, HG_HEADS, HG_DK, HEAD_W), F32)
        out_shape += [st, st]
        out_specs += [st_spec, st_spec]
    return pl.pallas_call(
        functools.partial(_hgrn_kernel, t=t, has_init=has_init, emit_state=emit_state),
        out_shape=out_shape,
        grid=(batch, HG_HEADS),
        in_specs=in_specs,
        out_specs=out_specs,
        scratch_shapes=[pltpu.VMEM((t, HEAD_W), F32), pltpu.VMEM((t, HEAD_W), F32),
                        pltpu.VMEM((HEAD_W, HG_DK), F32), pltpu.VMEM((HEAD_W, HG_DK), F32),
                        pltpu.VMEM((2, HG_CHUNK.bit_length(), HG_CHUNK, HG_CHUNK), F32)],
        compiler_params=_cparams(("arbitrary", "arbitrary")),
        name="hgrn_lat" if has_init else "hgrn_ctx",
    )(*args)


def _attn_kernel(*refs, t, n_cache, use_rope, emit_cache, lam_init):
    it = iter(refs)
    q_ref, k_ref, v_ref = next(it), next(it), next(it)
    if use_rope:
        cosq_ref, sinq_ref, cos_ref, sin_ref = next(it), next(it), next(it), next(it)
    if n_cache:
        ck_ref, cv_ref = next(it), next(it)
    lq1_ref, lk1_ref, lq2_ref, lk2_ref, ng_ref, o_ref = (next(it) for _ in range(6))
    if emit_cache:
        nk_ref, nv_ref = next(it), next(it)
    k1_scr, k2_scr, v_scr, s_scr, p_scr = (next(it) for _ in range(5))
    k_scrs = (k1_scr, k2_scr)
    tq = q_ref.shape[0]

    @pl.when(pl.program_id(2) == 0)
    def _():
        rows = min(t, ATT_TK)

        def stage(i, carry):
            sl = pl.ds(pl.multiple_of(i * rows, rows), rows)
            k = k_ref[sl, :]
            v = v_ref[sl, :]
            if emit_cache:
                nk_ref[0, 0, 0, sl, :] = k
                nv_ref[0, 0, 0, sl, :] = v
            if use_rope:
                k = _rope(k, cos_ref[sl, :], sin_ref[sl, :])
            k1_scr[sl, :], k2_scr[sl, :] = _split_maps(k)
            v_scr[sl, :] = v.astype(BF16)
            return carry

        lax.fori_loop(0, t // rows, stage, 0)
        if n_cache:
            k1_scr[t:t + n_cache, :], k2_scr[t:t + n_cache, :] = _split_maps(ck_ref[0, 0, 0])
            v_scr[t:t + n_cache, :] = cv_ref[0, 0, 0].astype(BF16)

    q = q_ref[...]
    if use_rope:
        q = _rope(q, cosq_ref[...], sinq_ref[...])
    q = (q * (DA_DK ** -0.5)).astype(BF16)
    nt = (((1,), (1,)), ((), ()))
    n_keys = t + n_cache
    tiles = [(st, min(ATT_TK, n_keys - st)) for st in range(0, n_keys, ATT_TK)]

    def scores_tile(mp, st, sz, mx):
        s = lax.dot_general(q, k_scrs[mp][st:st + sz, :], nt, preferred_element_type=F32)
        s_scr[mp, :, st:st + sz] = s
        for j in range(sz // LANES):
            mx = jnp.maximum(mx, s[:, j * LANES:(j + 1) * LANES])
        return mx

    def exp_tile(mp, st, sz, m, part):
        for lo in range(st, st + sz, LANES):
            e = jnp.exp(s_scr[mp, :, lo:lo + LANES] - m)
            part = part + e
            p_scr[mp, :, lo:lo + LANES] = e.astype(BF16)
        return part

    def value_tile(mp, st, sz, acc):
        return acc + jnp.dot(p_scr[mp, :, st:st + sz], v_scr[st:st + sz, :], preferred_element_type=F32)

    def row_stat(x, op):
        return jnp.broadcast_to(op(x, axis=-1, keepdims=True), (tq, LANES))

    neg = jnp.full((tq, LANES), NEG_INF, F32)
    zero = jnp.zeros((tq, LANES), F32)
    mx = neg
    for st, sz in tiles:
        mx = scores_tile(0, st, sz, mx)
    m0 = row_stat(mx, jnp.max)
    mx, part0 = neg, zero
    for st, sz in tiles:
        mx = scores_tile(1, st, sz, mx)
        part0 = exp_tile(0, st, sz, m0, part0)
    m1 = row_stat(mx, jnp.max)
    acc0, part1 = zero, zero
    for st, sz in tiles:
        acc0 = value_tile(0, st, sz, acc0)
        part1 = exp_tile(1, st, sz, m1, part1)
    acc1 = jnp.dot(p_scr[1], v_scr[...], preferred_element_type=F32)

    lam = (jnp.exp(jnp.sum(lq1_ref[...] * lk1_ref[...], axis=-1, keepdims=True))
           - jnp.exp(jnp.sum(lq2_ref[...] * lk2_ref[...], axis=-1, keepdims=True)) + lam_init)
    o = acc0 / row_stat(part0, jnp.sum) - lam * (acc1 / row_stat(part1, jnp.sum))
    o_ref[...] = (_rms(o) * ng_ref[...] * (1.0 - lam_init)).astype(BF16)


def _attn(proj, batch, t, lam_params, norm_g, lam_init, cos=None, sin=None, cache_k=None, cache_v=None,
          emit_cache=False):
    use_rope = cos is not None
    n_cache = 0 if cache_k is None else cache_k.shape[3]
    tq = min(ATT_TQ, t)
    nq = t // tq
    n_keys = t + n_cache

    def col(cb, rows, row_map):
        return pl.BlockSpec((rows, HEAD_W), lambda b, h, i: (row_map(b, i), cb + h))

    in_specs = [col(CB_Q_DA, tq, lambda b, i: b * nq + i),
                col(CB_K_DA, t, lambda b, i: b), col(CB_V_DA, t, lambda b, i: b)]
    args = [proj, proj, proj]
    if use_rope:
        in_specs += [pl.BlockSpec((tq, HEAD_W), lambda b, h, i: (i, 0))] * 2
        in_specs += [pl.BlockSpec((t, HEAD_W), lambda b, h, i: (0, 0))] * 2
        args += [cos, sin, cos, sin]
    cache_spec = lambda n: pl.BlockSpec((1, 1, 1, n, HEAD_W), lambda b, h, i: (b, 0, h, 0, 0))
    if n_cache:
        in_specs += [cache_spec(n_cache)] * 2
        args += [cache_k, cache_v]
    small = pl.BlockSpec((1, DA_DK), lambda b, h, i: (0, 0))
    in_specs += [small] * 4 + [pl.BlockSpec((1, HEAD_W), lambda b, h, i: (0, 0))]
    args += [p.reshape(1, DA_DK) for p in lam_params] + [norm_g.reshape(1, HEAD_W)]
    out_shape = [jax.ShapeDtypeStruct((batch * t, DA_HEADS * HEAD_W), BF16)]
    out_specs = [pl.BlockSpec((tq, HEAD_W), lambda b, h, i: (b * nq + i, h))]
    if emit_cache:
        out_shape += [jax.ShapeDtypeStruct((batch, 1, DA_HEADS, t, HEAD_W), F32)] * 2
        out_specs += [cache_spec(t)] * 2
    scratch = [pltpu.VMEM((n_keys, HEAD_W), BF16)] * 3
    scratch += [pltpu.VMEM((2, tq, n_keys), F32), pltpu.VMEM((2, tq, n_keys), BF16)]
    return pl.pallas_call(
        functools.partial(_attn_kernel, t=t, n_cache=n_cache, use_rope=use_rope, emit_cache=emit_cache,
                          lam_init=lam_init),
        out_shape=out_shape,
        grid=(batch, DA_HEADS, nq),
        in_specs=in_specs,
        out_specs=out_specs,
        scratch_shapes=scratch,
        compiler_params=_cparams(("arbitrary",) * 3),
        name="attn_lat" if use_rope else "attn_ctx",
    )(*args)


def _out_kernel(mhg_c, mda_c, x_c, mhg_l, mda_l, x_l, w_ref, mod_ref, g_ref, rw_ref, rb_ref,
                x1_ref, h2_ref, ids_ref, wts_ref, *, ctx_tiles):
    def body(mhg_ref, mda_ref, x_ref):
        mix = jnp.dot(mhg_ref[...], w_ref[0:HG_WIDTH, :], preferred_element_type=F32)
        mix = mix + jnp.dot(mda_ref[...], w_ref[HG_WIDTH:, :], preferred_element_type=F32)
        x1 = x_ref[...] + mod_ref[0, 2:3, :] * mix
        x1_ref[...] = x1
        h2 = (_rms(x1) * g_ref[...]) * (1.0 + mod_ref[0, 4:5, :]) + mod_ref[0, 3:4, :]
        h2_ref[...] = h2

        h_hi = h2.astype(BF16)
        h_lo = (h2 - h_hi.astype(F32)).astype(BF16)
        logit = (jnp.dot(h_hi, rw_ref[...], preferred_element_type=F32)
                 + jnp.dot(h_lo, rw_ref[...], preferred_element_type=F32)) + rb_ref[...]
        lane = lax.broadcasted_iota(I32, logit.shape, 1)

        def first_max(x):
            m = jnp.max(x, axis=-1, keepdims=True)
            return m, jnp.min(jnp.where(x == m, lane, LANES), axis=-1, keepdims=True)

        gmask = lane < N_GROUPS
        gmax, gsel = first_max(jnp.where(gmask, logit, NEG_INF))
        p_grp = 1.0 / jnp.sum(jnp.where(gmask, jnp.exp(logit - gmax), 0.0), axis=-1, keepdims=True)
        lo = N_GROUPS + EXP_PER_GROUP * gsel
        le = jnp.where((lane >= lo) & (lane < lo + EXP_PER_GROUP), logit, NEG_INF)
        v1, i1 = first_max(le)
        v2, i2 = first_max(jnp.where(lane == i1, NEG_INF, le))
        e = jnp.exp(v2 - v1)
        w1 = p_grp / (1.0 + e)
        w2 = p_grp * e / (1.0 + e)
        ids_ref[...] = jnp.where(lane == 0, i1 - N_GROUPS, jnp.where(lane == 1, i2 - N_GROUPS, 0))
        wts_ref[...] = jnp.where(lane == 0, w1, jnp.where(lane == 1, w2, 0.0))

    i = pl.program_id(0)
    pl.when(i < ctx_tiles)(lambda: body(mhg_c, mda_c, x_c))
    pl.when(i >= ctx_tiles)(lambda: body(mhg_l, mda_l, x_l))


def _out_proj(mhg_c, mda_c, x_c, mhg_l, mda_l, x_l, w_bf16, mods, g, rw, rb, lat_t):
    n_ctx, d = x_c.shape
    n_lat = x_l.shape[0]
    n = n_ctx + n_lat
    ctx_tiles = n_ctx // OUT_TM
    lat_tiles = lat_t // OUT_TM
    row = lambda i: (i, 0)
    const = lambda i: (0, 0)
    crow = lambda i: (jnp.minimum(i, ctx_tiles - 1), 0)
    lrow = lambda i: (jnp.maximum(i - ctx_tiles, 0), 0)
    seg = lambda i: (jnp.where(i < ctx_tiles, 0, 1 + (i - ctx_tiles) // lat_tiles), 0, 0)
    half = lambda m: pl.BlockSpec((OUT_TM, HG_WIDTH), m)
    full = lambda m: pl.BlockSpec((OUT_TM, d), m)
    return pl.pallas_call(
        functools.partial(_out_kernel, ctx_tiles=ctx_tiles),
        out_shape=[jax.ShapeDtypeStruct((n, d), F32), jax.ShapeDtypeStruct((n, d), F32),
                   jax.ShapeDtypeStruct((n, LANES), I32), jax.ShapeDtypeStruct((n, LANES), F32)],
        grid=(n // OUT_TM,),
        in_specs=[half(crow), half(crow), full(crow), half(lrow), half(lrow), full(lrow),
                  pl.BlockSpec((d, d), const), pl.BlockSpec((1, N_ADA, d), seg), pl.BlockSpec((1, d), const),
                  pl.BlockSpec((d, LANES), const), pl.BlockSpec((1, LANES), const)],
        out_specs=[full(row), full(row), pl.BlockSpec((OUT_TM, LANES), row), pl.BlockSpec((OUT_TM, LANES), row)],
        compiler_params=_cparams(("arbitrary",)),
        name="out_proj",
    )(mhg_c, mda_c, x_c, mhg_l, mda_l, x_l, w_bf16, mods, g.reshape(1, d), rw, rb)


def _gmm_kernel(src_ref, texp_ref, ntile_ref, h_hbm, w1_ref, w3_ref, w2_ref, y_hbm,
                x0, x1, y0, y1, sems, *, n_tok):
    i = pl.program_id(0)
    n_tiles = ntile_ref[0]
    xs = (x0, x1)
    ys = (y0, y1)

    def gather_copy(tile, r, slot):
        p = jnp.maximum(src_ref[(tile + 1) * MOE_TM + r], 0)
        tok = jnp.where(p >= n_tok, p - n_tok, p)
        return pltpu.make_async_copy(h_hbm.at[pl.ds(tok, 1)], xs[slot].at[pl.ds(r, 1)], sems.at[slot])

    def scatter_copy(tile, r, slot):
        p = src_ref[(tile + 1) * MOE_TM + r]
        dst = jnp.where(p >= 0, p, TOP_K * n_tok + r)
        return pltpu.make_async_copy(ys[slot].at[pl.ds(r, 1)], y_hbm.at[pl.ds(dst, 1)], sems.at[2 + slot])

    @pl.when(i == 0)
    def _():
        y1[...] = jnp.zeros_like(y1)
        for r in range(MOE_TM):
            gather_copy(0, r, 0).start()

    def wait_gather(slot):
        pltpu.make_async_copy(h_hbm.at[pl.ds(0, MOE_TM)], xs[slot], sems.at[slot]).wait()

    def wait_scatter(slot):
        pltpu.make_async_copy(ys[slot], y_hbm.at[pl.ds(0, MOE_TM)], sems.at[2 + slot]).wait()

    def step(cur):
        nxt = 1 - cur
        wait_gather(cur)
        for r in range(MOE_TM):
            gather_copy(i + 1, r, nxt).start()
        for r in range(MOE_TM):
            scatter_copy(i - 1, r, nxt).start()
        x = xs[cur][...].astype(BF16)
        a = jnp.dot(x, w1_ref[0].astype(BF16), preferred_element_type=F32)
        b = jnp.dot(x, w3_ref[0].astype(BF16), preferred_element_type=F32)
        hid = (_silu(a) * b).astype(BF16)
        ys[cur][...] = jnp.dot(hid, w2_ref[0].astype(BF16), preferred_element_type=F32)
        wait_scatter(nxt)

        @pl.when(i == n_tiles)
        def _():
            wait_gather(nxt)

    live = i <= n_tiles
    pl.when(live & (i % 2 == 0))(lambda: step(0))
    pl.when(live & (i % 2 == 1))(lambda: step(1))


def _gmm(src, tile_expert, n_tiles, h2, w1, w3, w2, n_tok):
    d = h2.shape[1]
    steps = tile_expert.shape[0]
    wspec = lambda shape: pl.BlockSpec((1,) + shape, lambda i, src, te, nt: (te[i], 0, 0))
    return pl.pallas_call(
        functools.partial(_gmm_kernel, n_tok=n_tok),
        out_shape=jax.ShapeDtypeStruct((TOP_K * n_tok + MOE_TM, d), F32),
        grid_spec=pltpu.PrefetchScalarGridSpec(
            num_scalar_prefetch=3,
            grid=(steps,),
            in_specs=[pl.BlockSpec(memory_space=pl.ANY),
                      wspec((d, D_EXPERT)), wspec((d, D_EXPERT)), wspec((D_EXPERT, d))],
            out_specs=pl.BlockSpec(memory_space=pl.ANY),
            scratch_shapes=[pltpu.VMEM((MOE_TM, d), F32)] * 4 + [pltpu.SemaphoreType.DMA((4,))],
        ),
        compiler_params=_cparams(("arbitrary",)),
        name="gmm",
    )(src, tile_expert, n_tiles, h2, w1, w3, w2)


def _route_tables(ids, n_tok):
    n_pairs = n_tok * TOP_K
    max_tiles = n_pairs // MOE_TM + N_EXPERTS
    e_flat = ids[:, :TOP_K].T.reshape(n_pairs)
    onehot = (e_flat[:, None] == jnp.arange(N_EXPERTS, dtype=I32)[None, :]).astype(I32)
    csum = jnp.cumsum(onehot, axis=0)
    counts = csum[-1]
    rank = jnp.sum(csum * onehot, axis=1) - 1
    tiles_per = (counts + MOE_TM - 1) // MOE_TM
    tile_end = jnp.cumsum(tiles_per)
    row_start = (tile_end - tiles_per + 1) * MOE_TM
    pos = row_start[e_flat] + rank
    src = jnp.full(((max_tiles + 3) * MOE_TM,), -1, I32).at[pos].set(jnp.arange(n_pairs, dtype=I32))
    steps = jnp.arange(max_tiles + 1, dtype=I32)
    owner = jnp.sum((tile_end[None, :] <= steps[:, None]).astype(I32), axis=1)
    tile_expert = jnp.minimum(owner, N_EXPERTS - 1)
    return src, tile_expert, tile_end[-1:].astype(I32)


def _combine_kernel(x1_ref, y0_ref, y1_ref, wts_ref, mod_ref, g_ref, o_ref):
    w = wts_ref[...]
    moe = w[:, 0:1] * y0_ref[...] + w[:, 1:2] * y1_ref[...]
    x2 = x1_ref[...] + mod_ref[0, 5:6, :] * moe
    o_ref[...] = _rms(x2) * g_ref[...]


def _combine(x1, y_pairs, wts, mods, mod_row0, g, row0, batch, t, n_tok):
    d = x1.shape[1]
    nt = t // CMB_TM
    rb0 = row0 // CMB_TM
    slot1 = n_tok // CMB_TM
    return pl.pallas_call(
        _combine_kernel,
        out_shape=jax.ShapeDtypeStruct((batch * t, d), F32),
        grid=(batch * nt,),
        in_specs=[pl.BlockSpec((CMB_TM, d), lambda i: (rb0 + i, 0)),
                  pl.BlockSpec((CMB_TM, d), lambda i: (rb0 + i, 0)),
                  pl.BlockSpec((CMB_TM, d), lambda i: (slot1 + rb0 + i, 0)),
                  pl.BlockSpec((CMB_TM, LANES), lambda i: (rb0 + i, 0)),
                  pl.BlockSpec((1, N_ADA, d), lambda i: (mod_row0 + i // nt, 0, 0)),
                  pl.BlockSpec((1, d), lambda i: (0, 0))],
        out_specs=pl.BlockSpec((CMB_TM, d), lambda i: (i, 0)),
        compiler_params=_cparams(("arbitrary",)),
        name="combine",
    )(x1, y_pairs, y_pairs, wts, mods, g.reshape(1, d))


def _rope_tables(n_lat):
    half = DA_DK // 2
    nf = half // 2
    t = jnp.arange(n_lat)
    row = (t // GRID_W).astype(F32)
    col = (t % GRID_W).astype(F32)
    inv = ROPE_THETA ** (-jnp.arange(nf, dtype=F32) / nf)
    lane = np.arange(HEAD_W)
    freq = inv[lane % nf]
    pos = jnp.where(((lane // half) % 2 == 0)[None, :], row[:, None], col[:, None])
    ang = pos * freq[None, :]
    sign = np.where((lane % half) < nf, -1.0, 1.0).astype(np.float32)
    return jnp.cos(ang), jnp.sin(ang) * sign[None, :]


def kernel(x_prompt, x_sample, cache_k, cache_v, state_hgrn_fwd, state_hgrn_bwd, c, c_ctx, w_ada, b_ada,
           norm1_g, norm2_g, norm_final_g, w_in, hg_lb_fwd, hg_lb_bwd, hg_norm_g, da_lambda_q1, da_lambda_k1,
           da_lambda_q2, da_lambda_k2, da_norm_g, w_out, router_g_w, router_g_b, router_e_w, router_e_b,
           exp_w1, exp_w3, exp_w2):
    l = 0
    batch, seq, d = x_prompt.shape
    dec_batch, dec_seq, _ = x_sample.shape
    n_ctx = batch * seq
    n_lat = dec_batch * dec_seq
    n_tok = n_ctx + n_lat
    lam_init = 0.8 - 0.6 * math.exp(-0.3 * l)

    cond = jnp.zeros((8, d), F32).at[0].set(c_ctx).at[1:1 + dec_batch].set(c)
    mods = _ada(cond, w_ada[l], b_ada[l]).reshape(8, N_ADA, d)

    w_in_bf16 = w_in[l].astype(BF16)
    proj_c = _in_proj(x_prompt.reshape(1, n_ctx, d), mods, 0, norm1_g[l], w_in_bf16, 1024, "in_proj_ctx")
    proj_l = _in_proj(x_sample, mods, 1, norm1_g[l], w_in_bf16, 1024, "in_proj_lat")
    lam_params = (da_lambda_q1[l], da_lambda_k1[l], da_lambda_q2[l], da_lambda_k2[l])

    da_ctx, new_k, new_v = _attn(proj_c, batch, seq, lam_params, da_norm_g[l], lam_init, emit_cache=True)
    hg_ctx, new_sf, new_sb = _hgrn(proj_c, batch, seq, hg_lb_fwd, hg_lb_bwd, hg_norm_g[l], emit_state=True)

    cos, sin = _rope_tables(dec_seq)
    (da_lat,) = _attn(proj_l, dec_batch, dec_seq, lam_params, da_norm_g[l], lam_init, cos=cos, sin=sin,
                      cache_k=cache_k[:, l:l + 1], cache_v=cache_v[:, l:l + 1])
    (hg_lat,) = _hgrn(proj_l, dec_batch, dec_seq, hg_lb_fwd, hg_lb_bwd, hg_norm_g[l],
                      s0f=state_hgrn_fwd[:, l:l + 1], s0b=state_hgrn_bwd[:, l:l + 1])

    rw = jnp.zeros((d, LANES), F32).at[:, :N_GROUPS].set(router_g_w[l]).at[:, N_GROUPS:N_GROUPS + N_EXPERTS].set(
        router_e_w[l]).astype(BF16)
    rb = jnp.zeros((1, LANES), F32).at[0, :N_GROUPS].set(router_g_b[l]).at[0, N_GROUPS:N_GROUPS + N_EXPERTS].set(
        router_e_b[l])
    x1, h2, ids, wts = _out_proj(hg_ctx, da_ctx, x_prompt.reshape(n_ctx, d), hg_lat, da_lat,
                                 x_sample.reshape(n_lat, d), w_out[l].astype(BF16), mods, norm2_g[l], rw, rb, dec_seq)

    src, tile_expert, n_tiles = _route_tables(ids, n_tok)
    y_pairs = _gmm(src, tile_expert, n_tiles, h2, exp_w1[l], exp_w3[l], exp_w2[l], n_tok)

    y_ctx = _combine(x1, y_pairs, wts, mods, 0, norm_final_g, 0, 1, n_ctx, n_tok)
    y_lat = _combine(x1, y_pairs, wts, mods, 1, norm_final_g, n_ctx, dec_batch, dec_seq, n_tok)
    return (y_ctx.reshape(batch, seq, d), y_lat.reshape(dec_batch, dec_seq, d), new_k, new_v, new_sf, new_sb)
```

```python
import functools
import math

import numpy as np
import jax
import jax.numpy as jnp
from jax import lax
from jax.experimental import pallas as pl
from jax.experimental.pallas import tpu as pltpu

F32 = jnp.float32
BF16 = jnp.bfloat16
I32 = jnp.int32

GRID_W = 64
HG_WIDTH = 1024
HG_DK = 128
HG_HEADS = 8
DA_HEADS = 8
DA_DK = 64
HEAD_W = 128
ROPE_THETA = 10000.0
N_GROUPS = 4
EXP_PER_GROUP = 8
N_EXPERTS = 32
TOP_K = 2
D_EXPERT = 512
N_ADA = 6
RMS_EPS = 1e-6
CB_Q_HG, CB_F_FW, CB_F_BW, CB_I_HG, CB_G_HG, CB_Q_DA, CB_K_DA, CB_V_DA = (8 * i for i in range(8))

LANES = 128
SUBLANES = 8
VMEM_LIMIT = 56 * 1024 * 1024

ADA_TN = 1536
IN_TN = 1024
OUT_TM = 256
HG_CHUNK = 64
HG_GROUP = 4
HG_STEP_ROWS = 1024
ATT_TQ = 256
ATT_TK = 512
MOE_TM = 256
CMB_TM = 512
NEG_INF = float("-inf")


def _cparams(sem):
    return pltpu.CompilerParams(dimension_semantics=sem, vmem_limit_bytes=VMEM_LIMIT)


def _silu(x):
    return x * jax.nn.sigmoid(x)


def _rms(x):
    return x * lax.rsqrt(jnp.mean(x * x, axis=-1, keepdims=True) + RMS_EPS)


def _ada_kernel(cond_ref, w_ref, b_ref, o_ref):
    s = _silu(cond_ref[...]).astype(BF16)
    o_ref[...] = jnp.dot(s, w_ref[...].astype(BF16), preferred_element_type=F32) + b_ref[...]


def _ada(cond, w, b):
    rows, d = cond.shape
    n = w.shape[1]
    return pl.pallas_call(
        _ada_kernel,
        out_shape=jax.ShapeDtypeStruct((rows, n), F32),
        grid=(n // ADA_TN,),
        in_specs=[pl.BlockSpec((rows, d), lambda j: (0, 0)),
                  pl.BlockSpec((d, ADA_TN), lambda j: (0, j)),
                  pl.BlockSpec((1, ADA_TN), lambda j: (0, j))],
        out_specs=pl.BlockSpec((rows, ADA_TN), lambda j: (0, j)),
        compiler_params=_cparams(("arbitrary",)),
        name="ada",
    )(cond, w, b.reshape(1, n))


def _in_kernel(x_ref, mod_ref, g_ref, w_ref, o_ref, h_ref):
    @pl.when(pl.program_id(1) == 0)
    def _():
        y = _rms(x_ref[...]) * g_ref[...]
        h_ref[...] = (y * (1.0 + mod_ref[0, 1:2, :]) + mod_ref[0, 0:1, :]).astype(BF16)

    o_ref[...] = jnp.dot(h_ref[...], w_ref[...], preferred_element_type=F32)


def _in_proj(x, mods, mod_row0, g, w_bf16, tm, name):
    batch, t, d = x.shape
    cols = w_bf16.shape[1]
    nt = t // tm
    return pl.pallas_call(
        _in_kernel,
        out_shape=jax.ShapeDtypeStruct((batch * t, cols), F32),
        grid=(batch * nt, cols // IN_TN),
        in_specs=[pl.BlockSpec((tm, d), lambda i, j: (i, 0)),
                  pl.BlockSpec((1, N_ADA, d), lambda i, j: (mod_row0 + i // nt, 0, 0)),
                  pl.BlockSpec((1, d), lambda i, j: (0, 0)),
                  pl.BlockSpec((d, IN_TN), lambda i, j: (0, j))],
        out_specs=pl.BlockSpec((tm, IN_TN), lambda i, j: (i, j)),
        scratch_shapes=[pltpu.VMEM((tm, d), BF16)],
        compiler_params=_cparams(("arbitrary", "arbitrary")),
        name=name,
    )(x.reshape(batch * t, d), mods, g.reshape(1, d), w_bf16)


def _rope(x, cos, sin_signed):
    lane = lax.broadcasted_iota(I32, x.shape, 1)
    first = (lane % 32) < 16
    partner = jnp.where(first, pltpu.roll(x, LANES - 16, 1), pltpu.roll(x, 16, 1))
    return x * cos + partner * sin_signed


def _split_maps(k):
    lane = lax.broadcasted_iota(I32, k.shape, 1)
    m1 = lane < DA_DK
    return jnp.where(m1, k, 0.0).astype(BF16), jnp.where(m1, 0.0, k).astype(BF16)


def _lower_bound(lb_ref):
    p = lb_ref[...]
    e = jnp.exp(p - jnp.max(p, axis=0, keepdims=True))
    return e[0:1, :] / jnp.sum(e, axis=0, keepdims=True)


def _hgrn_pair_masks(mask_ref, c):
    ti = lax.broadcasted_iota(I32, (c, c), 0)
    si = lax.broadcasted_iota(I32, (c, c), 1)
    for d in range(2):
        mask_ref[d, 0] = (ti == si).astype(F32)
        for j in range(c.bit_length() - 1):
            same = (ti >> (j + 1)) == (si >> (j + 1))
            t_bit = ((ti >> j) & 1) == 1
            s_bit = ((si >> j) & 1) == 1
            pair = (s_bit & jnp.logical_not(t_bit)) if d else (t_bit & jnp.logical_not(s_bit))
            mask_ref[d, 1 + j] = (same & pair).astype(F32)


def _hgrn_chunk(q, f_logit, v, lb, mask_ref, rev):
    c = q.shape[0]
    n_piece = c // SUBLANES
    levels = c.bit_length() - 1
    low = SUBLANES.bit_length() - 1
    d = 1 if rev else 0
    nt = (((1,), (1,)), ((), ()))
    pieces = lambda x: [x[SUBLANES * i:SUBLANES * (i + 1)] for i in range(n_piece)]
    whole = lambda xs: jnp.concatenate(xs, axis=0)

    fg = lb + (1.0 - lb) * jax.nn.sigmoid(f_logit)
    k = 1.0 - fg
    q_bf = q.astype(BF16)
    k_bf = k.astype(BF16)
    qs, ks, fgs = pieces(q), pieces(k), pieces(fg)
    tot = pieces(jnp.log(fg))
    pre = list(tot)
    suf = [jnp.zeros((SUBLANES, LANES), F32)] * n_piece
    sub = lax.broadcasted_iota(I32, (SUBLANES, LANES), 0)

    def pair_scores(qt, kt, idx):
        return lax.dot_general(qt, kt, nt, preferred_element_type=F32) * mask_ref[d, idx]

    scores = pair_scores(q_bf, k_bf, 0)
    yield
    for j in range(levels):
        if j < low:
            sh = 1 << j
            bit = ((sub >> j) & 1) == 1
            late = jnp.logical_not(bit) if rev else bit
            if j == 0:
                qt = whole([qs[i] * jnp.where(late, fgs[i], 1.0) for i in range(n_piece)]).astype(BF16)
                kt = k_bf
            else:
                es = [jnp.exp(jnp.where(late, pre[i], suf[i])) for i in range(n_piece)]
                qt = whole([qs[i] * es[i] for i in range(n_piece)]).astype(BF16)
                kt = whole([ks[i] * es[i] for i in range(n_piece)]).astype(BF16)
            for i in range(n_piece):
                up = pltpu.roll(tot[i], sh, 0)
                dn = pltpu.roll(tot[i], SUBLANES - sh, 0)
                sib = jnp.where(late, dn, up) if rev else jnp.where(late, up, dn)
                pre[i] = pre[i] + jnp.where(late, sib, 0.0)
                suf[i] = suf[i] + jnp.where(late, 0.0, sib)
                tot[i] = tot[i] + sib
        else:
            half = 1 << (j - low)
            upper = [(i // half) % 2 == 1 for i in range(n_piece)]
            late = [(not u) if rev else u for u in upper]
            es = [jnp.exp(pre[i] if late[i] else suf[i]) for i in range(n_piece)]
            qt = whole([qs[i] * es[i] for i in range(n_piece)]).astype(BF16)
            kt = whole([ks[i] * es[i] for i in range(n_piece)]).astype(BF16)
            sib = [tot[i - half] if upper[i] else tot[i + half] for i in range(n_piece)]
            pre = [pre[i] + sib[i] if late[i] else pre[i] for i in range(n_piece)]
            suf = [suf[i] if late[i] else suf[i] + sib[i] for i in range(n_piece)]
            tot = [tot[i] + sib[i] for i in range(n_piece)]
        scores = scores + pair_scores(qt, kt, 1 + j)
        yield

    q_dec = whole([qs[i] * jnp.exp(pre[i]) for i in range(n_piece)]).astype(BF16)
    k_dec = whole([ks[i] * jnp.exp(suf[i]) for i in range(n_piece)]).astype(BF16)
    return dict(q_dec=q_dec, k_dec=k_dec, v_t=v.T.astype(BF16), v=v.astype(BF16),
                scores=scores.astype(BF16), decay=jnp.exp(tot[0][0:1, :]))


def _hgrn_state_step(p, st_ref):
    st = st_ref[...]
    o = lax.dot_general(p["q_dec"], st.astype(BF16), (((1,), (1,)), ((), ())), preferred_element_type=F32)
    st_ref[...] = p["decay"] * st + jnp.dot(p["v_t"], p["k_dec"], preferred_element_type=F32)
    return o


def _lockstep(gens):
    results = [None] * len(gens)
    live = list(range(len(gens)))
    while live:
        for idx in list(live):
            try:
                next(gens[idx])
            except StopIteration as stop:
                results[idx] = stop.value
                live.remove(idx)
    return results


def _hgrn_kernel(*refs, t, seqs, has_init, emit_state):
    q_all, ff_all, fb_all, i_all, g_all, lbf_ref, lbb_ref, ng_ref = refs[:8]
    pos = 8
    if has_init:
        s0f_ref, s0b_ref = refs[pos:pos + 2]
        pos += 2
    o_all = refs[pos]
    pos += 1
    if emit_state:
        sf_ref, sb_ref = refs[pos:pos + 2]
        pos += 2
    of_scr, ob_scr, stf, stb, mask_scr = refs[pos:pos + 5]

    c = HG_CHUNK
    n = t // c

    @pl.when((pl.program_id(0) == 0) & (pl.program_id(1) == 0))
    def _():
        _hgrn_pair_masks(mask_scr, c)

    lbf = _lower_bound(lbf_ref)
    lbb = _lower_bound(lbb_ref)
    for s in range(seqs):
        rows = pl.ds(s * t, t)
        views = [r.at[rows] for r in (q_all, ff_all, fb_all, i_all, g_all, o_all)]
        _hgrn_sequence(s, views, lbf, lbb, ng_ref, s0f_ref if has_init else None, s0b_ref if has_init else None,
                       sf_ref if emit_state else None, sb_ref if emit_state else None,
                       of_scr, ob_scr, stf, stb, mask_scr, t)


def _hgrn_sequence(s, views, lbf, lbb, ng_ref, s0f_ref, s0b_ref, sf_ref, sb_ref,
                   of_scr, ob_scr, stf, stb, mask_scr, t):
    q_ref, ff_ref, fb_ref, i_ref, g_ref, o_ref = views
    c = HG_CHUNK
    n = t // c
    if s0f_ref is not None:
        stf[...] = s0f_ref[s, 0, 0].T
        stb[...] = s0b_ref[s, 0, 0].T
    else:
        stf[...] = jnp.zeros_like(stf)
        stb[...] = jnp.zeros_like(stb)

    def body(i, carry):
        work = []
        for u in range(HG_GROUP):
            ci = i * HG_GROUP + u
            work.append((pl.ds(pl.multiple_of(ci * c, c), c), ff_ref, lbf, stf, of_scr, False))
            work.append((pl.ds(pl.multiple_of((n - 1 - ci) * c, c), c), fb_ref, lbb, stb, ob_scr, True))
        parts = _lockstep([_hgrn_chunk(_silu(q_ref[sl, :]), f_ref[sl, :], i_ref[sl, :], lb, mask_scr, rev)
                           for sl, f_ref, lb, _, _, rev in work])
        outs = [_hgrn_state_step(p, w[3]) for p, w in zip(parts, work)]
        for p, w, o in zip(parts, work, outs):
            w[4][w[0], :] = o + jnp.dot(p["scores"], p["v"], preferred_element_type=F32)
        return carry

    lax.fori_loop(0, n // HG_GROUP, body, 0)

    fin = min(t, 256)

    def finish(i, carry):
        sl = pl.ds(pl.multiple_of(i * fin, fin), fin)
        o = _rms(of_scr[sl, :] + ob_scr[sl, :]) * ng_ref[...]
        o_ref[sl, :] = (o * _silu(g_ref[sl, :])).astype(BF16)
        return carry

    lax.fori_loop(0, t // fin, finish, 0)
    if sf_ref is not None:
        sf_ref[s, 0, 0] = stf[...].T
        sb_ref[s, 0, 0] = stb[...].T


def _hgrn(proj, batch, t, lb_fwd, lb_bwd, norm_g, s0f=None, s0b=None, emit_state=False):
    has_init = s0f is not None
    seqs = max(1, HG_STEP_ROWS // t)
    rows = seqs * t

    def col(cb):
        return pl.BlockSpec((rows, HEAD_W), lambda b, h: (b, cb + h))

    lb_spec = pl.BlockSpec((lb_fwd.shape[0], HEAD_W), lambda b, h: (0, h))
    in_specs = [col(CB_Q_HG), col(CB_F_FW), col(CB_F_BW), col(CB_I_HG), col(CB_G_HG),
                lb_spec, lb_spec, pl.BlockSpec((1, HEAD_W), lambda b, h: (0, 0))]
    args = [proj] * 5 + [lb_fwd, lb_bwd, norm_g.reshape(1, HEAD_W)]
    st_spec = pl.BlockSpec((seqs, 1, 1, HG_DK, HEAD_W), lambda b, h: (b, 0, h, 0, 0))
    if has_init:
        in_specs += [st_spec, st_spec]
        args += [s0f, s0b]
    out_shape = [jax.ShapeDtypeStruct((batch * t, HG_WIDTH), BF16)]
    out_specs = [pl.BlockSpec((rows, HEAD_W), lambda b, h: (b, h))]
    if emit_state:
        st = jax.ShapeDtypeStruct((batch, 1, HG_HEADS, HG_DK, HEAD_W), F32)
        out_shape += [st, st]
        out_specs += [st_spec, st_spec]
    return pl.pallas_call(
        functools.partial(_hgrn_kernel, t=t, seqs=seqs, has_init=has_init, emit_state=emit_state),
        out_shape=out_shape,
        grid=(batch // seqs, HG_HEADS),
        in_specs=in_specs,
        out_specs=out_specs,
        scratch_shapes=[pltpu.VMEM((t, HEAD_W), F32), pltpu.VMEM((t, HEAD_W), F32),
                        pltpu.VMEM((HEAD_W, HG_DK), F32), pltpu.VMEM((HEAD_W, HG_DK), F32),
                        pltpu.VMEM((2, HG_CHUNK.bit_length(), HG_CHUNK, HG_CHUNK), F32)],
        compiler_params=_cparams(("arbitrary", "arbitrary")),
        name="hgrn_lat" if has_init else "hgrn_ctx",
    )(*args)


def _attn_kernel(*refs, t, n_cache, use_rope, emit_cache, lam_init):
    it = iter(refs)
    q_ref, k_ref, v_ref = next(it), next(it), next(it)
    if use_rope:
        cosq_ref, sinq_ref, cos_ref, sin_ref = next(it), next(it), next(it), next(it)
    if n_cache:
        ck_ref, cv_ref = next(it), next(it)
    lq1_ref, lk1_ref, lq2_ref, lk2_ref, ng_ref, o_ref = (next(it) for _ in range(6))
    if emit_cache:
        nk_ref, nv_ref = next(it), next(it)
    k1_scr, k2_scr, v_scr, s_scr, p_scr = (next(it) for _ in range(5))
    k_scrs = (k1_scr, k2_scr)
    tq = q_ref.shape[0]

    @pl.when(pl.program_id(2) == 0)
    def _():
        rows = min(t, ATT_TK)

        def stage(i, carry):
            sl = pl.ds(pl.multiple_of(i * rows, rows), rows)
            k = k_ref[sl, :]
            v = v_ref[sl, :]
            if emit_cache:
                nk_ref[0, 0, 0, sl, :] = k
                nv_ref[0, 0, 0, sl, :] = v
            if use_rope:
                k = _rope(k, cos_ref[sl, :], sin_ref[sl, :])
            k1_scr[sl, :], k2_scr[sl, :] = _split_maps(k)
            v_scr[sl, :] = v.astype(BF16)
            return carry

        lax.fori_loop(0, t // rows, stage, 0)
        if n_cache:
            k1_scr[t:t + n_cache, :], k2_scr[t:t + n_cache, :] = _split_maps(ck_ref[0, 0, 0])
            v_scr[t:t + n_cache, :] = cv_ref[0, 0, 0].astype(BF16)

    q = q_ref[...]
    if use_rope:
        q = _rope(q, cosq_ref[...], sinq_ref[...])
    q = (q * (DA_DK ** -0.5)).astype(BF16)
    nt = (((1,), (1,)), ((), ()))
    n_keys = t + n_cache
    tiles = [(st, min(ATT_TK, n_keys - st)) for st in range(0, n_keys, ATT_TK)]

    def scores_tile(mp, st, sz, mx):
        s = lax.dot_general(q, k_scrs[mp][st:st + sz, :], nt, preferred_element_type=F32)
        s_scr[mp, :, st:st + sz] = s
        for j in range(sz // LANES):
            mx = jnp.maximum(mx, s[:, j * LANES:(j + 1) * LANES])
        return mx

    def exp_tile(mp, st, sz, m, part):
        for lo in range(st, st + sz, LANES):
            e = jnp.exp(s_scr[mp, :, lo:lo + LANES] - m)
            part = part + e
            p_scr[mp, :, lo:lo + LANES] = e.astype(BF16)
        return part

    def value_tile(mp, st, sz, acc):
        return acc + jnp.dot(p_scr[mp, :, st:st + sz], v_scr[st:st + sz, :], preferred_element_type=F32)

    def row_stat(x, op):
        return jnp.broadcast_to(op(x, axis=-1, keepdims=True), (tq, LANES))

    neg = jnp.full((tq, LANES), NEG_INF, F32)
    zero = jnp.zeros((tq, LANES), F32)
    mx = neg
    for st, sz in tiles:
        mx = scores_tile(0, st, sz, mx)
    m0 = row_stat(mx, jnp.max)
    mx, part0 = neg, zero
    for st, sz in tiles:
        mx = scores_tile(1, st, sz, mx)
        part0 = exp_tile(0, st, sz, m0, part0)
    m1 = row_stat(mx, jnp.max)
    acc0, part1 = zero, zero
    for st, sz in tiles:
        acc0 = value_tile(0, st, sz, acc0)
        part1 = exp_tile(1, st, sz, m1, part1)
    acc1 = jnp.dot(p_scr[1], v_scr[...], preferred_element_type=F32)

    lam = (jnp.exp(jnp.sum(lq1_ref[...] * lk1_ref[...], axis=-1, keepdims=True))
           - jnp.exp(jnp.sum(lq2_ref[...] * lk2_ref[...], axis=-1, keepdims=True)) + lam_init)
    o = acc0 / row_stat(part0, jnp.sum) - lam * (acc1 / row_stat(part1, jnp.sum))
    o_ref[...] = (_rms(o) * ng_ref[...] * (1.0 - lam_init)).astype(BF16)


def _attn(proj, batch, t, lam_params, norm_g, lam_init, cos=None, sin=None, cache_k=None, cache_v=None,
          emit_cache=False):
    use_rope = cos is not None
    n_cache = 0 if cache_k is None else cache_k.shape[3]
    nq = t // ATT_TQ
    n_keys = t + n_cache

    def col(cb, rows, row_map):
        return pl.BlockSpec((rows, HEAD_W), lambda b, h, i: (row_map(b, i), cb + h))

    in_specs = [col(CB_Q_DA, ATT_TQ, lambda b, i: b * nq + i),
                col(CB_K_DA, t, lambda b, i: b), col(CB_V_DA, t, lambda b, i: b)]
    args = [proj, proj, proj]
    if use_rope:
        in_specs += [pl.BlockSpec((ATT_TQ, HEAD_W), lambda b, h, i: (i, 0))] * 2
        in_specs += [pl.BlockSpec((t, HEAD_W), lambda b, h, i: (0, 0))] * 2
        args += [cos, sin, cos, sin]
    cache_spec = lambda n: pl.BlockSpec((1, 1, 1, n, HEAD_W), lambda b, h, i: (b, 0, h, 0, 0))
    if n_cache:
        in_specs += [cache_spec(n_cache)] * 2
        args += [cache_k, cache_v]
    small = pl.BlockSpec((1, DA_DK), lambda b, h, i: (0, 0))
    in_specs += [small] * 4 + [pl.BlockSpec((1, HEAD_W), lambda b, h, i: (0, 0))]
    args += [p.reshape(1, DA_DK) for p in lam_params] + [norm_g.reshape(1, HEAD_W)]
    out_shape = [jax.ShapeDtypeStruct((batch * t, DA_HEADS * HEAD_W), BF16)]
    out_specs = [pl.BlockSpec((ATT_TQ, HEAD_W), lambda b, h, i: (b * nq + i, h))]
    if emit_cache:
        out_shape += [jax.ShapeDtypeStruct((batch, 1, DA_HEADS, t, HEAD_W), F32)] * 2
        out_specs += [cache_spec(t)] * 2
    scratch = [pltpu.VMEM((n_keys, HEAD_W), BF16)] * 3
    scratch += [pltpu.VMEM((2, ATT_TQ, n_keys), F32), pltpu.VMEM((2, ATT_TQ, n_keys), BF16)]
    return pl.pallas_call(
        functools.partial(_attn_kernel, t=t, n_cache=n_cache, use_rope=use_rope, emit_cache=emit_cache,
                          lam_init=lam_init),
        out_shape=out_shape,
        grid=(batch, DA_HEADS, nq),
        in_specs=in_specs,
        out_specs=out_specs,
        scratch_shapes=scratch,
        compiler_params=_cparams(("arbitrary",) * 3),
        name="attn_lat" if use_rope else "attn_ctx",
    )(*args)


def _out_kernel(mhg_c, mda_c, x_c, mhg_l, mda_l, x_l, w_ref, mod_ref, g_ref, rw_ref, rb_ref,
                x1_ref, h2_ref, ids_ref, wts_ref, *, ctx_tiles):
    def body(mhg_ref, mda_ref, x_ref):
        mix = jnp.dot(mhg_ref[...], w_ref[0:HG_WIDTH, :], preferred_element_type=F32)
        mix = mix + jnp.dot(mda_ref[...], w_ref[HG_WIDTH:, :], preferred_element_type=F32)
        x1 = x_ref[...] + mod_ref[0, 2:3, :] * mix
        x1_ref[...] = x1
        h2 = (_rms(x1) * g_ref[...]) * (1.0 + mod_ref[0, 4:5, :]) + mod_ref[0, 3:4, :]
        h2_ref[...] = h2

        h_hi = h2.astype(BF16)
        h_lo = (h2 - h_hi.astype(F32)).astype(BF16)
        logit = (jnp.dot(h_hi, rw_ref[...], preferred_element_type=F32)
                 + jnp.dot(h_lo, rw_ref[...], preferred_element_type=F32)) + rb_ref[...]
        lane = lax.broadcasted_iota(I32, logit.shape, 1)

        def first_max(x):
            m = jnp.max(x, axis=-1, keepdims=True)
            return m, jnp.min(jnp.where(x == m, lane, LANES), axis=-1, keepdims=True)

        gmask = lane < N_GROUPS
        gmax, gsel = first_max(jnp.where(gmask, logit, NEG_INF))
        p_grp = 1.0 / jnp.sum(jnp.where(gmask, jnp.exp(logit - gmax), 0.0), axis=-1, keepdims=True)
        lo = N_GROUPS + EXP_PER_GROUP * gsel
        le = jnp.where((lane >= lo) & (lane < lo + EXP_PER_GROUP), logit, NEG_INF)
        v1, i1 = first_max(le)
        v2, i2 = first_max(jnp.where(lane == i1, NEG_INF, le))
        e = jnp.exp(v2 - v1)
        w1 = p_grp / (1.0 + e)
        w2 = p_grp * e / (1.0 + e)
        ids_ref[...] = jnp.where(lane == 0, i1 - N_GROUPS, jnp.where(lane == 1, i2 - N_GROUPS, 0))
        wts_ref[...] = jnp.where(lane == 0, w1, jnp.where(lane == 1, w2, 0.0))

    i = pl.program_id(0)
    pl.when(i < ctx_tiles)(lambda: body(mhg_c, mda_c, x_c))
    pl.when(i >= ctx_tiles)(lambda: body(mhg_l, mda_l, x_l))


def _out_proj(mhg_c, mda_c, x_c, mhg_l, mda_l, x_l, w_bf16, mods, g, rw, rb, lat_t):
    n_ctx, d = x_c.shape
    n_lat = x_l.shape[0]
    n = n_ctx + n_lat
    ctx_tiles = n_ctx // OUT_TM
    lat_tiles = lat_t // OUT_TM
    row = lambda i: (i, 0)
    const = lambda i: (0, 0)
    crow = lambda i: (jnp.minimum(i, ctx_tiles - 1), 0)
    lrow = lambda i: (jnp.maximum(i - ctx_tiles, 0), 0)
    seg = lambda i: (jnp.where(i < ctx_tiles, 0, 1 + (i - ctx_tiles) // lat_tiles), 0, 0)
    half = lambda m: pl.BlockSpec((OUT_TM, HG_WIDTH), m)
    full = lambda m: pl.BlockSpec((OUT_TM, d), m)
    return pl.pallas_call(
        functools.partial(_out_kernel, ctx_tiles=ctx_tiles),
        out_shape=[jax.ShapeDtypeStruct((n, d), F32), jax.ShapeDtypeStruct((n, d), F32),
                   jax.ShapeDtypeStruct((n, LANES), I32), jax.ShapeDtypeStruct((n, LANES), F32)],
        grid=(n // OUT_TM,),
        in_specs=[half(crow), half(crow), full(crow), half(lrow), half(lrow), full(lrow),
                  pl.BlockSpec((d, d), const), pl.BlockSpec((1, N_ADA, d), seg), pl.BlockSpec((1, d), const),
                  pl.BlockSpec((d, LANES), const), pl.BlockSpec((1, LANES), const)],
        out_specs=[full(row), full(row), pl.BlockSpec((OUT_TM, LANES), row), pl.BlockSpec((OUT_TM, LANES), row)],
        compiler_params=_cparams(("arbitrary",)),
        name="out_proj",
    )(mhg_c, mda_c, x_c, mhg_l, mda_l, x_l, w_bf16, mods, g.reshape(1, d), rw, rb)


def _gmm_kernel(src_ref, texp_ref, ntile_ref, h_hbm, w1_ref, w3_ref, w2_ref, y_hbm,
                x0, x1, y0, y1, sems, *, n_tok):
    i = pl.program_id(0)
    n_tiles = ntile_ref[0]
    xs = (x0, x1)
    ys = (y0, y1)

    def gather_copy(tile, r, slot):
        p = jnp.maximum(src_ref[(tile + 1) * MOE_TM + r], 0)
        tok = jnp.where(p >= n_tok, p - n_tok, p)
        return pltpu.make_async_copy(h_hbm.at[pl.ds(tok, 1)], xs[slot].at[pl.ds(r, 1)], sems.at[slot])

    def scatter_copy(tile, r, slot):
        p = src_ref[(tile + 1) * MOE_TM + r]
        dst = jnp.where(p >= 0, p, TOP_K * n_tok + r)
        return pltpu.make_async_copy(ys[slot].at[pl.ds(r, 1)], y_hbm.at[pl.ds(dst, 1)], sems.at[2 + slot])

    @pl.when(i == 0)
    def _():
        y1[...] = jnp.zeros_like(y1)
        for r in range(MOE_TM):
            gather_copy(0, r, 0).start()

    def wait_gather(slot):
        pltpu.make_async_copy(h_hbm.at[pl.ds(0, MOE_TM)], xs[slot], sems.at[slot]).wait()

    def wait_scatter(slot):
        pltpu.make_async_copy(ys[slot], y_hbm.at[pl.ds(0, MOE_TM)], sems.at[2 + slot]).wait()

    def step(cur):
        nxt = 1 - cur
        wait_gather(cur)
        for r in range(MOE_TM):
            gather_copy(i + 1, r, nxt).start()
        for r in range(MOE_TM):
            scatter_copy(i - 1, r, nxt).start()
        x = xs[cur][...].astype(BF16)
        a = jnp.dot(x, w1_ref[0].astype(BF16), preferred_element_type=F32)
        b = jnp.dot(x, w3_ref[0].astype(BF16), preferred_element_type=F32)
        hid = (_silu(a) * b).astype(BF16)
        ys[cur][...] = jnp.dot(hid, w2_ref[0].astype(BF16), preferred_element_type=F32)
        wait_scatter(nxt)

        @pl.when(i == n_tiles)
        def _():
            wait_gather(nxt)

    live = i <= n_tiles
    pl.when(live & (i % 2 == 0))(lambda: step(0))
    pl.when(live & (i % 2 == 1))(lambda: step(1))


def _gmm(src, tile_expert, n_tiles, h2, w1, w3, w2, n_tok):
    d = h2.shape[1]
    steps = tile_expert.shape[0]
    wspec = lambda shape: pl.BlockSpec((1,) + shape, lambda i, src, te, nt: (te[i], 0, 0))
    return pl.pallas_call(
        functools.partial(_gmm_kernel, n_tok=n_tok),
        out_shape=jax.ShapeDtypeStruct((TOP_K * n_tok + MOE_TM, d), F32),
        grid_spec=pltpu.PrefetchScalarGridSpec(
            num_scalar_prefetch=3,
            grid=(steps,),
            in_specs=[pl.BlockSpec(memory_space=pl.ANY),
                      wspec((d, D_EXPERT)), wspec((d, D_EXPERT)), wspec((D_EXPERT, d))],
            out_specs=pl.BlockSpec(memory_space=pl.ANY),
            scratch_shapes=[pltpu.VMEM((MOE_TM, d), F32)] * 4 + [pltpu.SemaphoreType.DMA((4,))],
        ),
        compiler_params=_cparams(("arbitrary",)),
        name="gmm",
    )(src, tile_expert, n_tiles, h2, w1, w3, w2)


def _route_tables(ids, n_tok):
    n_pairs = n_tok * TOP_K
    max_tiles = n_pairs // MOE_TM + N_EXPERTS
    e_flat = ids[:, :TOP_K].T.reshape(n_pairs)
    onehot = (e_flat[:, None] == jnp.arange(N_EXPERTS, dtype=I32)[None, :]).astype(I32)
    csum = jnp.cumsum(onehot, axis=0)
    counts = csum[-1]
    rank = jnp.sum(csum * onehot, axis=1) - 1
    tiles_per = (counts + MOE_TM - 1) // MOE_TM
    tile_end = jnp.cumsum(tiles_per)
    row_start = (tile_end - tiles_per + 1) * MOE_TM
    pos = row_start[e_flat] + rank
    src = jnp.full(((max_tiles + 3) * MOE_TM,), -1, I32).at[pos].set(jnp.arange(n_pairs, dtype=I32))
    steps = jnp.arange(max_tiles + 1, dtype=I32)
    owner = jnp.sum((tile_end[None, :] <= steps[:, None]).astype(I32), axis=1)
    tile_expert = jnp.minimum(owner, N_EXPERTS - 1)
    return src, tile_expert, tile_end[-1:].astype(I32)


def _combine_kernel(x1_ref, y0_ref, y1_ref, wts_ref, mod_ref, g_ref, o_ref):
    w = wts_ref[...]
    moe = w[:, 0:1] * y0_ref[...] + w[:, 1:2] * y1_ref[...]
    x2 = x1_ref[...] + mod_ref[0, 5:6, :] * moe
    o_ref[...] = _rms(x2) * g_ref[...]


def _combine(x1, y_pairs, wts, mods, mod_row0, g, row0, batch, t, n_tok):
    d = x1.shape[1]
    nt = t // CMB_TM
    rb0 = row0 // CMB_TM
    slot1 = n_tok // CMB_TM
    return pl.pallas_call(
        _combine_kernel,
        out_shape=jax.ShapeDtypeStruct((batch * t, d), F32),
        grid=(batch * nt,),
        in_specs=[pl.BlockSpec((CMB_TM, d), lambda i: (rb0 + i, 0)),
                  pl.BlockSpec((CMB_TM, d), lambda i: (rb0 + i, 0)),
                  pl.BlockSpec((CMB_TM, d), lambda i: (slot1 + rb0 + i, 0)),
                  pl.BlockSpec((CMB_TM, LANES), lambda i: (rb0 + i, 0)),
                  pl.BlockSpec((1, N_ADA, d), lambda i: (mod_row0 + i // nt, 0, 0)),
                  pl.BlockSpec((1, d), lambda i: (0, 0))],
        out_specs=pl.BlockSpec((CMB_TM, d), lambda i: (i, 0)),
        compiler_params=_cparams(("arbitrary",)),
        name="combine",
    )(x1, y_pairs, y_pairs, wts, mods, g.reshape(1, d))


def _rope_tables(n_lat):
    half = DA_DK // 2
    nf = half // 2
    t = jnp.arange(n_lat)
    row = (t // GRID_W).astype(F32)
    col = (t % GRID_W).astype(F32)
    inv = ROPE_THETA ** (-jnp.arange(nf, dtype=F32) / nf)
    lane = np.arange(HEAD_W)
    freq = inv[lane % nf]
    pos = jnp.where(((lane // half) % 2 == 0)[None, :], row[:, None], col[:, None])
    ang = pos * freq[None, :]
    sign = np.where((lane % half) < nf, -1.0, 1.0).astype(np.float32)
    return jnp.cos(ang), jnp.sin(ang) * sign[None, :]


def kernel(x_prompt, x_sample, cache_k, cache_v, state_hgrn_fwd, state_hgrn_bwd, c, c_ctx, w_ada, b_ada,
           norm1_g, norm2_g, norm_final_g, w_in, hg_lb_fwd, hg_lb_bwd, hg_norm_g, da_lambda_q1, da_lambda_k1,
           da_lambda_q2, da_lambda_k2, da_norm_g, w_out, router_g_w, router_g_b, router_e_w, router_e_b,
           exp_w1, exp_w3, exp_w2):
    l = 0
    batch, seq, d = x_prompt.shape
    dec_batch, dec_seq, _ = x_sample.shape
    n_ctx = batch * seq
    n_lat = dec_batch * dec_seq
    n_tok = n_ctx + n_lat
    lam_init = 0.8 - 0.6 * math.exp(-0.3 * l)

    cond = jnp.zeros((8, d), F32).at[0].set(c_ctx).at[1:1 + dec_batch].set(c)
    mods = _ada(cond, w_ada[l], b_ada[l]).reshape(8, N_ADA, d)

    w_in_bf16 = w_in[l].astype(BF16)
    proj_c = _in_proj(x_prompt.reshape(1, n_ctx, d), mods, 0, norm1_g[l], w_in_bf16, 1024, "in_proj_ctx")
    proj_l = _in_proj(x_sample, mods, 1, norm1_g[l], w_in_bf16, 1024, "in_proj_lat")
    lam_params = (da_lambda_q1[l], da_lambda_k1[l], da_lambda_q2[l], da_lambda_k2[l])

    da_ctx, new_k, new_v = _attn(proj_c, batch, seq, lam_params, da_norm_g[l], lam_init, emit_cache=True)
    hg_ctx, new_sf, new_sb = _hgrn(proj_c, batch, seq, hg_lb_fwd, hg_lb_bwd, hg_norm_g[l], emit_state=True)

    cos, sin = _rope_tables(dec_seq)
    (da_lat,) = _attn(proj_l, dec_batch, dec_seq, lam_params, da_norm_g[l], lam_init, cos=cos, sin=sin,
                      cache_k=cache_k[:, l:l + 1], cache_v=cache_v[:, l:l + 1])
    (hg_lat,) = _hgrn(proj_l, dec_batch, dec_seq, hg_lb_fwd, hg_lb_bwd, hg_norm_g[l],
                      s0f=state_hgrn_fwd[:, l:l + 1], s0b=state_hgrn_bwd[:, l:l + 1])

    rw = jnp.zeros((d, LANES), F32).at[:, :N_GROUPS].set(router_g_w[l]).at[:, N_GROUPS:N_GROUPS + N_EXPERTS].set(
        router_e_w[l]).astype(BF16)
    rb = jnp.zeros((1, LANES), F32).at[0, :N_GROUPS].set(router_g_b[l]).at[0, N_GROUPS:N_GROUPS + N_EXPERTS].set(
        router_e_b[l])
    x1, h2, ids, wts = _out_proj(hg_ctx, da_ctx, x_prompt.reshape(n_ctx, d), hg_lat, da_lat,
                                 x_sample.reshape(n_lat, d), w_out[l].astype(BF16), mods, norm2_g[l], rw, rb, dec_seq)

    src, tile_expert, n_tiles = _route_tables(ids, n_tok)
    y_pairs = _gmm(src, tile_expert, n_tiles, h2, exp_w1[l], exp_w3[l], exp_w2[l], n_tok)

    y_ctx = _combine(x1, y_pairs, wts, mods, 0, norm_final_g, 0, 1, n_ctx, n_tok)
    y_lat = _combine(x1, y_pairs, wts, mods, 1, norm_final_g, n_ctx, dec_batch, dec_seq, n_tok)
    return (y_ctx.reshape(batch, seq, d), y_lat.reshape(dec_batch, dec_seq, d), new_k, new_v, new_sf, new_sb)
```
